```python
import jax, jax.numpy as jnp
from jax import lax
import numpy as np

D_MODEL = 1024
BATCH = 4
SEQ = 8192
DEPTH = 1

CHUNK = 64
Q_BLOCK = 128
FOX_HEADS = 8
FOX_HEAD_DIM = 64
FOX_WIDTH = FOX_HEADS * FOX_HEAD_DIM
GLA_HEADS = 4
GLA_KEY_WIDTH = D_MODEL // 2
GLA_VALUE_WIDTH = D_MODEL
GLA_DK = GLA_KEY_WIDTH // GLA_HEADS
GLA_DV = GLA_VALUE_WIDTH // GLA_HEADS
GLA_GATE_RANK = 16
GLA_GATE_TAU = 16.0
N_EXPERTS = 32
TOP_K = 4
D_EXPERT = D_MODEL
SWIGLU_LIMIT = 7.0
SWIGLU_ALPHA = 1.702
EXPERT_BLOCK = 256
NORM_EPS = 1e-5
DEEPNORM_ALPHA = (2 * DEPTH) ** 0.25
DEEPNORM_BETA = (8 * DEPTH) ** -0.25
IN_SPLITS = (FOX_WIDTH, FOX_WIDTH, FOX_WIDTH, FOX_HEADS,
             GLA_KEY_WIDTH, GLA_KEY_WIDTH, GLA_VALUE_WIDTH, GLA_VALUE_WIDTH, GLA_GATE_RANK,
             D_MODEL, D_MODEL)
IN_WIDTH = sum(IN_SPLITS)

kernel_name = "hybrid_fox_gla_moe_deepnorm_adaln"


def _normalize(x):
    xf = x.astype(jnp.float32)
    mu = jnp.mean(xf, axis=-1, keepdims=True)
    var = jnp.mean(jnp.square(xf - mu), axis=-1, keepdims=True)
    return (xf - mu) * lax.rsqrt(var + NORM_EPS)


def _modulate(x, shift, scale):
    return (_normalize(x) * (1.0 + scale[:, None, :]) + shift[:, None, :]).astype(x.dtype)


def _post_norm(z, g, b):
    return (_normalize(z) * g + b).astype(z.dtype)


def _forgetting_attention(q, k, v, f_logit):
    B, S, H, Dh = q.shape
    n_blk = S // Q_BLOCK
    cum = jnp.cumsum(jax.nn.log_sigmoid(f_logit.astype(jnp.float32)), axis=1)
    cum = cum.transpose(0, 2, 1)
    qh = q.transpose(0, 2, 1, 3) * (Dh ** -0.5)
    kh = k.transpose(0, 2, 1, 3)
    vh = v.transpose(0, 2, 1, 3)
    q_blocks = qh.reshape(B, H, n_blk, Q_BLOCK, Dh).transpose(2, 0, 1, 3, 4)
    c_blocks = cum.reshape(B, H, n_blk, Q_BLOCK).transpose(2, 0, 1, 3)
    kpos = jnp.arange(S)

    def one_block(args):
        blk, qb, cb = args
        qpos = blk * Q_BLOCK + jnp.arange(Q_BLOCK)
        s = jnp.einsum('bhqd,bhkd->bhqk', qb, kh, preferred_element_type=jnp.float32)
        s = s + cb[..., :, None] - cum[..., None, :]
        s = jnp.where(kpos[None, :] <= qpos[:, None], s, -jnp.inf)
        p = jax.nn.softmax(s, axis=-1)
        return jnp.einsum('bhqk,bhkd->bhqd', p.astype(vh.dtype), vh)

    out = lax.map(one_block, (jnp.arange(n_blk), q_blocks, c_blocks))
    return out.transpose(1, 0, 3, 2, 4).reshape(B, S, H * Dh)


def _gla_chunked(q, k, v, log_a):
    B, S, H, dk = q.shape
    dv = v.shape[-1]
    n = S // CHUNK

    def to_chunks(t):
        return t.reshape(B, n, CHUNK, H, t.shape[-1]).transpose(1, 0, 3, 2, 4)

    qc = to_chunks(q.astype(jnp.float32) * (dk ** -0.5))
    kc = to_chunks(k.astype(jnp.float32))
    vc = to_chunks(v.astype(jnp.float32))
    ac = to_chunks(log_a.astype(jnp.float32))
    causal = jnp.tril(jnp.ones((CHUNK, CHUNK), dtype=bool))

    def step(state, inp):
        qi, ki, vi, ai = inp
        b = jnp.cumsum(ai, axis=2)
        b_last = b[:, :, -1:, :]
        o_inter = jnp.einsum('bhtk,bhkv->bhtv', qi * jnp.exp(b), state)
        rel = jnp.where(causal[:, :, None], b[:, :, :, None, :] - b[:, :, None, :, :], -jnp.inf)
        scores = jnp.einsum('bhtk,bhsk,bhtsk->bhts', qi, ki, jnp.exp(rel))
        o_intra = jnp.einsum('bhts,bhsv->bhtv', scores, vi)
        new_state = (jnp.exp(b_last)[:, :, 0, :, None] * state
                     + jnp.einsum('bhsk,bhsv->bhkv', ki * jnp.exp(b_last - b), vi))
        return new_state, o_inter + o_intra

    state0 = jnp.zeros((B, H, dk, dv), jnp.float32)
    _, out = lax.scan(step, state0, (qc, kc, vc, ac))
    return out.transpose(1, 0, 3, 2, 4).reshape(B, S, H, dv)


def _mixer(u, w_in, fox_f_bias, w_gla_gate, b_gla_gate, gla_norm_g,
           w_branch_a, w_branch_b, w_out):
    B, S, _ = u.shape
    points = np.cumsum(IN_SPLITS)[:-1].tolist()
    proj = jnp.einsum('bsd,de->bse', u, w_in)
    fq, fk, fv, ff, gq, gk, gv, gr, glr, gate_a, gate_b = jnp.split(proj, points, axis=-1)

    fshape = (B, S, FOX_HEADS, FOX_HEAD_DIM)
    y_a = _forgetting_attention(fq.reshape(fshape), fk.reshape(fshape), fv.reshape(fshape),
                                ff + fox_f_bias)

    log_a = jax.nn.log_sigmoid((glr @ w_gla_gate + b_gla_gate).astype(jnp.float32)) / GLA_GATE_TAU
    kshape = (B, S, GLA_HEADS, GLA_DK)
    y_b = _gla_chunked(gq.reshape(kshape), gk.reshape(kshape),
                       gv.reshape(B, S, GLA_HEADS, GLA_DV), log_a.reshape(kshape))
    y_b = y_b * lax.rsqrt(jnp.mean(jnp.square(y_b), axis=-1, keepdims=True) + NORM_EPS)
    y_b = (y_b.reshape(B, S, GLA_VALUE_WIDTH) * gla_norm_g * jax.nn.silu(gr.astype(jnp.float32))).astype(u.dtype)

    br_a = jnp.einsum('bse,ed->bsd', y_a, w_branch_a)
    br_b = jnp.einsum('bse,ed->bsd', y_b, w_branch_b)
    merged = jax.nn.sigmoid(gate_a) * br_a + jax.nn.sigmoid(gate_b) * br_b
    return jnp.einsum('bsd,de->bse', merged, w_out)


def _moe(u, w_router, b_router, w_up, b_up, w_down, b_down):
    B, S, D = u.shape
    T = B * S
    A = T * TOP_K
    xt = u.reshape(T, D)
    logits = (xt @ w_router).astype(jnp.float32) + b_router
    top_vals, top_idx = lax.top_k(logits, TOP_K)
    gates = jax.nn.softmax(top_vals, axis=-1)

    e_flat = top_idx.reshape(A)
    tok_flat = jnp.arange(A, dtype=jnp.int32) // TOP_K
    g_flat = gates.reshape(A)
    order = jnp.argsort(e_flat)
    e_sorted = e_flat[order]
    counts = jnp.bincount(e_flat, length=N_EXPERTS)
    starts = jnp.cumsum(counts) - counts
    padded = (counts + EXPERT_BLOCK - 1) // EXPERT_BLOCK * EXPERT_BLOCK
    padded_end = jnp.cumsum(padded)
    padded_start = padded_end - padded
    dest = padded_start[e_sorted] + (jnp.arange(A) - starts[e_sorted])
    n_blocks = -(-A // EXPERT_BLOCK) + N_EXPERTS
    rows = n_blocks * EXPERT_BLOCK
    row_tok = jnp.zeros((rows,), jnp.int32).at[dest].set(tok_flat[order])
    row_gate = jnp.zeros((rows,), jnp.float32).at[dest].set(g_flat[order])
    block_expert = jnp.minimum(
        jnp.searchsorted(padded_end, jnp.arange(n_blocks) * EXPERT_BLOCK, side='right'),
        N_EXPERTS - 1)

    def expert_block(args):
        tok, e = args
        xb = xt[tok]
        h = xb @ w_up[e] + b_up[e]
        h_glu = jnp.minimum(h[:, :D_EXPERT], SWIGLU_LIMIT)
        h_lin = jnp.clip(h[:, D_EXPERT:], -SWIGLU_LIMIT, SWIGLU_LIMIT)
        act = h_glu * jax.nn.sigmoid(SWIGLU_ALPHA * h_glu) * (h_lin + 1.0)
        return act @ w_down[e] + b_down[e]

    out = lax.map(expert_block, (row_tok.reshape(n_blocks, EXPERT_BLOCK), block_expert))
    out = out.reshape(rows, D) * row_gate[:, None]
    y = jax.ops.segment_sum(out, row_tok, num_segments=T)
    return y.reshape(B, S, D).astype(u.dtype)


def setup_inputs(seed: int = 0) -> dict:
    key = jax.random.key(seed)
    ks = jax.random.split(key, 24)
    L, D, E, F = DEPTH, D_MODEL, N_EXPERTS, D_EXPERT

    def nrm(k, shape, scale):
        return scale * jax.random.normal(k, shape, jnp.float32)

    return {
        "x": nrm(ks[0], (BATCH, SEQ, D), 1.0),
        "c": nrm(ks[1], (BATCH, D), 1.0),
        "w_ada": nrm(ks[2], (L, D, 6 * D), 0.1 * D ** -0.5),
        "b_ada": nrm(ks[3], (L, 6 * D), 0.02),
        "w_in": nrm(ks[4], (L, D, IN_WIDTH), D ** -0.5),
        "fox_f_bias": 2.0 + nrm(ks[5], (L, FOX_HEADS), 0.5),
        "w_gla_gate": nrm(ks[6], (L, GLA_GATE_RANK, GLA_KEY_WIDTH), GLA_GATE_RANK ** -0.5),
        "b_gla_gate": nrm(ks[7], (L, GLA_KEY_WIDTH), 0.1),
        "gla_norm_g": 1.0 + nrm(ks[8], (L, GLA_VALUE_WIDTH), 0.02),
        "w_branch_a": nrm(ks[9], (L, FOX_WIDTH, D), FOX_WIDTH ** -0.5),
        "w_branch_b": nrm(ks[10], (L, GLA_VALUE_WIDTH, D), GLA_VALUE_WIDTH ** -0.5),
        "w_out": nrm(ks[11], (L, D, D), DEEPNORM_BETA * D ** -0.5),
        "ln1_g": 1.0 + nrm(ks[12], (L, D), 0.02),
        "ln1_b": nrm(ks[13], (L, D), 0.02),
        "w_router": nrm(ks[14], (L, D, E), D ** -0.5),
        "b_router": nrm(ks[15], (L, E), 0.01),
        "w_up": nrm(ks[16], (L, E, D, 2 * F), D ** -0.5),
        "b_up": nrm(ks[17], (L, E, 2 * F), 0.01),
        "w_down": nrm(ks[18], (L, E, F, D), DEEPNORM_BETA * F ** -0.5),
        "b_down": nrm(ks[19], (L, E, D), 0.01),
        "ln2_g": 1.0 + nrm(ks[20], (L, D), 0.02),
        "ln2_b": nrm(ks[21], (L, D), 0.02),
    }


def reference(x, c, w_ada, b_ada, w_in, fox_f_bias, w_gla_gate, b_gla_gate, gla_norm_g,
              w_branch_a, w_branch_b, w_out, ln1_g, ln1_b, w_router, b_router,
              w_up, b_up, w_down, b_down, ln2_g, ln2_b):
    c_act = jax.nn.silu(c)
    for l in range(DEPTH):
        mod = c_act @ w_ada[l] + b_ada[l]
        sh1, sc1, g1, sh2, sc2, g2 = jnp.split(mod, 6, axis=-1)
        u = _modulate(x, sh1, sc1)
        mix = _mixer(u, w_in[l], fox_f_bias[l], w_gla_gate[l], b_gla_gate[l], gla_norm_g[l],
                     w_branch_a[l], w_branch_b[l], w_out[l])
        x = _post_norm(DEEPNORM_ALPHA * x + (1.0 + g1[:, None, :]) * mix, ln1_g[l], ln1_b[l])
        u = _modulate(x, sh2, sc2)
        ffn = _moe(u, w_router[l], b_router[l], w_up[l], b_up[l], w_down[l], b_down[l])
        x = _post_norm(DEEPNORM_ALPHA * x + (1.0 + g2[:, None, :]) * ffn, ln2_g[l], ln2_b[l])
    return x
```

```python
import functools

import numpy as np
import jax
import jax.numpy as jnp
from jax import lax
from jax.experimental import pallas as pl
from jax.experimental.pallas import tpu as pltpu

F32 = jnp.float32
BF16 = jnp.bfloat16
HIGHEST = lax.Precision.HIGHEST

D = 1024
FOX_H = 8
FOX_DH = 64
FOX_W = FOX_H * FOX_DH
GLA_H = 4
GLA_DK = 128
GLA_DV = 256
GLA_KW = GLA_H * GLA_DK
GLA_VW = GLA_H * GLA_DV
GLA_RANK = 16
GLA_TAU = 16.0
N_EXP = 32
TOP_K = 4
SWIGLU_LIMIT = 7.0
SWIGLU_ALPHA = 1.702
EPS = 1e-5
DEPTH = 1
ALPHA = (2 * DEPTH) ** 0.25
LANES = 128

GLA_CHUNK = 64
GLA_BLOCK = 512
MOE_BLOCK = 256
TM_IN = 256
TM_MERGE = 256
TM_FINAL = 256
TQ = 512
VMEM_LIMIT = 56 * 1024 * 1024

NEG = -1e30


def _ln(x):
    mu = jnp.mean(x, axis=-1, keepdims=True)
    xc = x - mu
    var = jnp.mean(xc * xc, axis=-1, keepdims=True)
    return xc * lax.rsqrt(var + EPS)


def _sigmoid(x):
    return 1.0 / (1.0 + jnp.exp(-x))


def _log_sigmoid(x):
    return jnp.minimum(x, 0.0) - jnp.log(1.0 + jnp.exp(-jnp.abs(x)))


def _split3(x):
    hi = x.astype(BF16)
    r = x - hi.astype(F32)
    mid = r.astype(BF16)
    lo = (r - mid.astype(F32)).astype(BF16)
    return hi, mid, lo


def _lane_col(x, idx, lane):
    return jnp.sum(jnp.where(lane == idx, x, 0.0), axis=1, keepdims=True)


def _dot(a, b):
    return jnp.dot(a, b, preferred_element_type=F32)


def _dot_nt(a, b):
    return lax.dot_general(a, b, (((1,), (1,)), ((), ())), preferred_element_type=F32)


def _dot_tn(a, b):
    return lax.dot_general(a, b, (((0,), (0,)), ((), ())), preferred_element_type=F32)


def _ada_kernel(c_ref, w_ref, b_ref, o_ref):
    c = c_ref[...]
    ca = c * _sigmoid(c)
    o_ref[...] = jnp.dot(ca, w_ref[...], precision=HIGHEST,
                         preferred_element_type=F32) + b_ref[...]


def _ada(c_pad, w, b):
    n = w.shape[1]
    tn = 1536
    return pl.pallas_call(
        _ada_kernel,
        grid=(n // tn,),
        in_specs=[pl.BlockSpec((8, D), lambda j: (0, 0)),
                  pl.BlockSpec((D, tn), lambda j: (0, j)),
                  pl.BlockSpec((1, tn), lambda j: (0, j))],
        out_specs=pl.BlockSpec((8, tn), lambda j: (0, j)),
        out_shape=jax.ShapeDtypeStruct((8, n), F32),
        compiler_params=pltpu.CompilerParams(
            dimension_semantics=("arbitrary",), vmem_limit_bytes=VMEM_LIMIT),
        name="ada",
    )(c_pad, w, b)


def _inproj_kernel(x_ref, sh_ref, sc_ref, wfox_ref, wgla_ref, wsm_ref, fb_ref, tri_ref,
                   qa_ref, ka_ref, va_ref, gq_ref, gk_ref, gv_ref, gr_ref, glr_ref,
                   carry_ref):
    tm = x_ref.shape[1]

    @pl.when(pl.program_id(1) == 0)
    def _():
        carry_ref[...] = jnp.zeros_like(carry_ref)

    u = _ln(x_ref[0]) * (1.0 + sc_ref[0]) + sh_ref[0]
    ub = u.astype(BF16)

    sm = _dot(ub, wsm_ref[...])
    glr_ref[0] = sm[:, LANES:LANES + GLA_RANK]
    lane = lax.broadcasted_iota(jnp.int32, (tm, LANES), 1)
    lf = jnp.where(lane < FOX_H, _log_sigmoid(sm[:, :LANES] + fb_ref[...]), 0.0)
    hi, mid, lo = _split3(lf)
    tri = tri_ref[...]
    cum = _dot(tri, hi) + _dot(tri, mid) + _dot(tri, lo) + carry_ref[...]
    carry_ref[...] = cum[tm - 1:tm, :]
    chi, cmid, clo = _split3(cum)
    chi, cmid, clo = chi.astype(F32), cmid.astype(F32), clo.astype(F32)

    ex_v = jnp.where(lane == FOX_DH, 1.0, 0.0)
    is_q1 = (lane >= FOX_DH + 3) & (lane < FOX_DH + 6)
    is_k1 = (lane >= FOX_DH) & (lane < FOX_DH + 3)
    for hp in range(FOX_H // 2):
        qp = _dot(ub, wfox_ref[:, 256 * hp:256 * hp + 256])
        kp = _dot(ub, wfox_ref[:, 1024 + 256 * hp:1024 + 256 * hp + 256])
        vp = _dot(ub, wfox_ref[:, 2048 + 256 * hp:2048 + 256 * hp + 256])
        for hh in range(2):
            h = 2 * hp + hh
            c0 = _lane_col(chi, h, lane)
            c1 = _lane_col(cmid, h, lane)
            c2 = _lane_col(clo, h, lane)
            ex_q = jnp.where(lane == FOX_DH, c0,
                             jnp.where(lane == FOX_DH + 1, c1,
                                       jnp.where(lane == FOX_DH + 2, c2,
                                                 jnp.where(is_q1, 1.0, 0.0))))
            ex_k = jnp.where(lane == FOX_DH + 3, -c0,
                             jnp.where(lane == FOX_DH + 4, -c1,
                                       jnp.where(lane == FOX_DH + 5, -c2,
                                                 jnp.where(is_k1, 1.0, 0.0))))
            sl = slice(LANES * hh, LANES * hh + LANES)
            qa_ref[0, h] = (qp[:, sl] + ex_q).astype(BF16)
            ka_ref[0, h] = (kp[:, sl] + ex_k).astype(BF16)
            va_ref[0, h] = (vp[:, sl] + ex_v).astype(BF16)

    for j in range(GLA_KW // 256):
        gq_ref[0, :, 256 * j:256 * j + 256] = _dot(
            ub, wgla_ref[:, 256 * j:256 * j + 256]).astype(BF16)
        gk_ref[0, :, 256 * j:256 * j + 256] = _dot(
            ub, wgla_ref[:, GLA_KW + 256 * j:GLA_KW + 256 * j + 256]).astype(BF16)
    for j in range(GLA_VW // 256):
        o = 2 * GLA_KW + 256 * j
        gv_ref[0, :, 256 * j:256 * j + 256] = _dot(ub, wgla_ref[:, o:o + 256]).astype(BF16)
        o = 2 * GLA_KW + GLA_VW + 256 * j
        gr_ref[0, :, 256 * j:256 * j + 256] = _dot(ub, wgla_ref[:, o:o + 256]).astype(BF16)


def _inproj(x, sh1, sc1, wfox, wgla, wsm, fb, tri):
    B, S, _ = x.shape
    tm = TM_IN
    const = lambda b, s: (0, 0)
    tok = lambda b, s: (b, s, 0)
    head = lambda b, s: (b, 0, s, 0)
    vec = lambda b, s: (b, 0, 0)
    hs = jax.ShapeDtypeStruct((B, FOX_H, S, LANES), BF16)
    return pl.pallas_call(
        _inproj_kernel,
        grid=(B, S // tm),
        in_specs=[pl.BlockSpec((1, tm, D), tok),
                  pl.BlockSpec((1, 1, D), vec),
                  pl.BlockSpec((1, 1, D), vec),
                  pl.BlockSpec(wfox.shape, const),
                  pl.BlockSpec(wgla.shape, const),
                  pl.BlockSpec(wsm.shape, const),
                  pl.BlockSpec((1, LANES), const),
                  pl.BlockSpec((tm, tm), const)],
        out_specs=[pl.BlockSpec((1, FOX_H, tm, LANES), head),
                   pl.BlockSpec((1, FOX_H, tm, LANES), head),
                   pl.BlockSpec((1, FOX_H, tm, LANES), head),
                   pl.BlockSpec((1, tm, GLA_KW), tok),
                   pl.BlockSpec((1, tm, GLA_KW), tok),
                   pl.BlockSpec((1, tm, GLA_VW), tok),
                   pl.BlockSpec((1, tm, GLA_VW), tok),
                   pl.BlockSpec((1, tm, GLA_RANK), tok)],
        out_shape=[hs, hs, hs,
                   jax.ShapeDtypeStruct((B, S, GLA_KW), BF16),
                   jax.ShapeDtypeStruct((B, S, GLA_KW), BF16),
                   jax.ShapeDtypeStruct((B, S, GLA_VW), BF16),
                   jax.ShapeDtypeStruct((B, S, GLA_VW), BF16),
                   jax.ShapeDtypeStruct((B, S, GLA_RANK), F32)],
        scratch_shapes=[pltpu.VMEM((1, LANES), F32)],
        compiler_params=pltpu.CompilerParams(
            dimension_semantics=("parallel", "arbitrary"), vmem_limit_bytes=VMEM_LIMIT),
        name="inproj",
    )(x, sh1, sc1, wfox, wgla, wsm, fb, tri)


def _fox_kernel(q_ref, k_ref, v_ref, o_ref, m_ref, acc_ref):
    qi = pl.program_id(2)
    tq = q_ref.shape[2]
    lane = lax.broadcasted_iota(jnp.int32, (tq, LANES), 1)
    outs = []
    for hh in range(2):
        q = q_ref[0, hh]
        m_ref[...] = jnp.full(m_ref.shape, -jnp.inf, F32)
        acc_ref[...] = jnp.zeros_like(acc_ref)

        def step(s, k0):
            m_old = m_ref[...]
            m_new = jnp.maximum(m_old, jnp.max(s, axis=1, keepdims=True))
            p = jnp.exp(s - m_new)
            v = v_ref[0, hh, pl.ds(k0, tq), :]
            acc_ref[...] = jnp.exp(m_old - m_new) * acc_ref[...] + _dot(p.astype(BF16), v)
            m_ref[...] = m_new

        def body(j, carry):
            k0 = pl.multiple_of(j * tq, tq)
            step(_dot_nt(q, k_ref[0, hh, pl.ds(k0, tq), :]), k0)
            return carry

        lax.fori_loop(0, qi, body, 0)
        k0 = pl.multiple_of(qi * tq, tq)
        s = _dot_nt(q, k_ref[0, hh, pl.ds(k0, tq), :])
        row = lax.broadcasted_iota(jnp.int32, (tq, tq), 0)
        col = lax.broadcasted_iota(jnp.int32, (tq, tq), 1)
        step(jnp.where(col <= row, s, -jnp.inf), k0)

        acc = acc_ref[...]
        outs.append(acc / _lane_col(acc, FOX_DH, lane))
    o_ref[0] = jnp.where(lane < FOX_DH, outs[0],
                         pltpu.roll(outs[1], FOX_DH, 1)).astype(BF16)


def _fox(qa, ka, va):
    B, H, S, _ = qa.shape
    tq = TQ
    return pl.pallas_call(
        _fox_kernel,
        grid=(B, H // 2, S // tq),
        in_specs=[pl.BlockSpec((1, 2, tq, LANES), lambda b, h, q: (b, h, q, 0)),
                  pl.BlockSpec((1, 2, S, LANES), lambda b, h, q: (b, h, 0, 0)),
                  pl.BlockSpec((1, 2, S, LANES), lambda b, h, q: (b, h, 0, 0))],
        out_specs=pl.BlockSpec((1, tq, LANES), lambda b, h, q: (b, q, h)),
        out_shape=jax.ShapeDtypeStruct((B, S, FOX_W), BF16),
        scratch_shapes=[pltpu.VMEM((tq, 1), F32), pltpu.VMEM((tq, LANES), F32)],
        compiler_params=pltpu.CompilerParams(
            dimension_semantics=("parallel", "parallel", "arbitrary"),
            vmem_limit_bytes=VMEM_LIMIT),
        name="fox",
    )(qa, ka, va)


def _gla_tables():
    C = GLA_CHUNK
    t = np.arange(C)[:, None]
    j = np.arange(C)[None, :]
    slabs = [(j <= t), (j > t)]
    masks = [np.eye(C, dtype=bool)]
    m = C // 2
    while m >= 1:
        g0 = (t // (2 * m)) * (2 * m)
        piv = g0 + m - 1
        upper = (t - g0) >= m
        slabs.append(np.where(upper, (j > piv) & (j <= t), (j > t) & (j <= piv)))
        s = np.arange(C)[None, :]
        masks.append(upper & ((s // (2 * m)) == (t // (2 * m))) & ((s % (2 * m)) < m))
        m //= 2
    cm = np.concatenate(slabs, axis=0).astype(np.float32)
    cm3 = np.concatenate([cm, cm, cm], axis=1)
    return cm3, np.stack(masks).astype(np.float32)


def _gla_kernel(gq_ref, gk_ref, gv_ref, gr_ref, glr_ref, wg_ref, bg_ref, ng_ref,
                cm_ref, mask_ref, o_ref, st_ref, la3_ref):
    C = GLA_CHUNK
    L = gq_ref.shape[1]
    n_lvl = mask_ref.shape[0] - 1

    @pl.when(pl.program_id(1) == 0)
    def _():
        st_ref[...] = jnp.zeros_like(st_ref)

    xg = jnp.dot(glr_ref[0], wg_ref[...], precision=HIGHEST,
                 preferred_element_type=F32) + bg_ref[...]
    la = _log_sigmoid(xg) * (1.0 / GLA_TAU)
    hi, mid, lo = _split3(la)
    for c in range(L // C):
        la3_ref[3 * C * c:3 * C * c + C, :] = hi[C * c:C * c + C]
        la3_ref[3 * C * c + C:3 * C * c + 2 * C, :] = mid[C * c:C * c + C]
        la3_ref[3 * C * c + 2 * C:3 * C * c + 3 * C, :] = lo[C * c:C * c + C]

    def chunk(ci, carry):
        r0 = pl.multiple_of(ci * C, C)
        la3 = la3_ref[pl.ds(pl.multiple_of(ci * 3 * C, 3 * C), 3 * C), :]
        w = jnp.exp(_dot(cm_ref[...], la3))
        q = gq_ref[0, pl.ds(r0, C), :].astype(F32)
        k = gk_ref[0, pl.ds(r0, C), :].astype(F32)
        for h in range(GLA_H):
            ks = slice(GLA_DK * h, GLA_DK * h + GLA_DK)
            vs = slice(GLA_DV * h, GLA_DV * h + GLA_DV)
            wh = w[:, ks]
            qh, kh = q[:, ks], k[:, ks]
            w_start, w_end = wh[0:C], wh[C:2 * C]
            st = st_ref[h]
            o = _dot_nt((qh * w_start).astype(BF16), st.astype(BF16))
            sc = mask_ref[0] * _dot_nt(qh.astype(BF16), kh.astype(BF16))
            for lv in range(n_lvl):
                wl = wh[(2 + lv) * C:(3 + lv) * C]
                sc = sc + mask_ref[1 + lv] * _dot_nt((qh * wl).astype(BF16),
                                                      (kh * wl).astype(BF16))
            vh = gv_ref[0, pl.ds(r0, C), vs]
            o = o + _dot(sc.astype(BF16), vh)
            st_ref[h] = st * w_start[C - 1:C, :] + _dot_tn(vh, (kh * w_end).astype(BF16))
            y = o * lax.rsqrt(jnp.mean(o * o, axis=1, keepdims=True) + EPS)
            g = gr_ref[0, pl.ds(r0, C), vs].astype(F32)
            o_ref[0, pl.ds(r0, C), vs] = (y * ng_ref[:, vs] * (g * _sigmoid(g))).astype(BF16)
        return carry

    lax.fori_loop(0, L // C, chunk, 0)


def _gla(gq, gk, gv, gr, glr, wg, bg, ng, cm3, masks):
    B, S, _ = gq.shape
    L = GLA_BLOCK
    tok = lambda b, s: (b, s, 0)
    c2 = lambda b, s: (0, 0)
    return pl.pallas_call(
        _gla_kernel,
        grid=(B, S // L),
        in_specs=[pl.BlockSpec((1, L, GLA_KW), tok),
                  pl.BlockSpec((1, L, GLA_KW), tok),
                  pl.BlockSpec((1, L, GLA_VW), tok),
                  pl.BlockSpec((1, L, GLA_VW), tok),
                  pl.BlockSpec((1, L, GLA_RANK), tok),
                  pl.BlockSpec(wg.shape, c2),
                  pl.BlockSpec(bg.shape, c2),
                  pl.BlockSpec(ng.shape, c2),
                  pl.BlockSpec(cm3.shape, c2),
                  pl.BlockSpec(masks.shape, lambda b, s: (0, 0, 0))],
        out_specs=pl.BlockSpec((1, L, GLA_VW), tok),
        out_shape=jax.ShapeDtypeStruct((B, S, GLA_VW), BF16),
        scratch_shapes=[pltpu.VMEM((GLA_H, GLA_DV, GLA_DK), F32),
                        pltpu.VMEM((3 * L, GLA_KW), BF16)],
        compiler_params=pltpu.CompilerParams(
            dimension_semantics=("parallel", "arbitrary"), vmem_limit_bytes=VMEM_LIMIT),
        name="gla",
    )(gq, gk, gv, gr, glr, wg, bg, ng, cm3, masks)


def _merge_kernel(x_ref, ya_ref, yb_ref, mod_ref, wgate_ref, wa_ref, wb_ref, wo_ref,
                  lng_ref, lnb_ref, wr_ref, br_ref,
                  x1_ref, u2_ref, topv_ref, topi_ref):
    tm = x_ref.shape[1]
    x = x_ref[0]
    mod = mod_ref[0]
    sh1, sc1, g1 = mod[0:1], mod[1:2], mod[2:3]
    sh2, sc2 = mod[3:4], mod[4:5]
    ub = (_ln(x) * (1.0 + sc1) + sh1).astype(BF16)
    br_a = _dot(ya_ref[0], wa_ref[...])
    br_b = _dot(yb_ref[0], wb_ref[...])
    merged = (_sigmoid(_dot(ub, wgate_ref[:, :D])) * br_a
              + _sigmoid(_dot(ub, wgate_ref[:, D:])) * br_b)
    mix = _dot(merged.astype(BF16), wo_ref[...])
    x1 = _ln(ALPHA * x + (1.0 + g1) * mix) * lng_ref[...] + lnb_ref[...]
    x1_ref[0] = x1
    u2 = _ln(x1) * (1.0 + sc2) + sh2
    u2_ref[...] = u2.reshape(u2_ref.shape)

    logits = jnp.dot(u2, wr_ref[...], precision=HIGHEST,
                     preferred_element_type=F32) + br_ref[...]
    lane = lax.broadcasted_iota(jnp.int32, (tm, LANES), 1)
    vals = jnp.zeros((tm, LANES), F32)
    idxs = jnp.zeros((tm, LANES), jnp.int32)
    cur = logits
    for k in range(TOP_K):
        mx = jnp.max(cur, axis=1, keepdims=True)
        ix = jnp.min(jnp.where(cur == mx, lane, LANES), axis=1, keepdims=True)
        vals = jnp.where(lane == k, mx, vals)
        idxs = jnp.where(lane == k, ix, idxs)
        cur = jnp.where(lane == ix, -jnp.inf, cur)
    v0 = jnp.max(jnp.where(lane < TOP_K, vals, -jnp.inf), axis=1, keepdims=True)
    e = jnp.where(lane < TOP_K, jnp.exp(vals - v0), 0.0)
    topv_ref[0] = e / jnp.sum(e, axis=1, keepdims=True)
    topi_ref[0] = idxs


def _merge(x, ya, yb, mod, wgate, wa, wb, wo, lng, lnb, wr, br):
    B, S, _ = x.shape
    tm = TM_MERGE
    tok = lambda b, s: (b, s, 0)
    c2 = lambda b, s: (0, 0)
    nst = S // tm
    return pl.pallas_call(
        _merge_kernel,
        grid=(B, nst),
        in_specs=[pl.BlockSpec((1, tm, D), tok),
                  pl.BlockSpec((1, tm, FOX_W), tok),
                  pl.BlockSpec((1, tm, GLA_VW), tok),
                  pl.BlockSpec((1, 8, D), lambda b, s: (b, 0, 0)),
                  pl.BlockSpec(wgate.shape, c2),
                  pl.BlockSpec(wa.shape, c2),
                  pl.BlockSpec(wb.shape, c2),
                  pl.BlockSpec(wo.shape, c2),
                  pl.BlockSpec((1, D), c2),
                  pl.BlockSpec((1, D), c2),
                  pl.BlockSpec(wr.shape, c2),
                  pl.BlockSpec((1, LANES), c2)],
        out_specs=[pl.BlockSpec((1, tm, D), tok),
                   pl.BlockSpec((tm, 1, D), lambda b, s: (b * nst + s, 0, 0)),
                   pl.BlockSpec((1, tm, LANES), tok),
                   pl.BlockSpec((1, tm, LANES), tok)],
        out_shape=[jax.ShapeDtypeStruct((B, S, D), F32),
                   jax.ShapeDtypeStruct((B * S, 1, D), F32),
                   jax.ShapeDtypeStruct((B, S, LANES), F32),
                   jax.ShapeDtypeStruct((B, S, LANES), jnp.int32)],
        compiler_params=pltpu.CompilerParams(
            dimension_semantics=("parallel", "arbitrary"), vmem_limit_bytes=VMEM_LIMIT),
        name="merge",
    )(x, ya, yb, mod, wgate, wa, wb, wo, lng, lnb, wr, br)


def _gather_rows(idx_ref, n, src_hbm, buf, sem):
    def issue(r, carry):
        pltpu.make_async_copy(src_hbm.at[idx_ref[0, 0, r]], buf.at[r], sem).start()
        return carry
    lax.fori_loop(0, n, issue, 0)

    def drain(r, carry):
        pltpu.make_async_copy(src_hbm.at[0], buf.at[r], sem).wait()
        return carry
    lax.fori_loop(0, n, drain, 0)


def _moe_kernel(be_ref, nu_ref, tok_ref, u2_hbm, wup_ref, bup_ref, wdn_ref, bdn_ref,
                o_ref, buf, sem):
    i = pl.program_id(0)
    nb = buf.shape[0]

    @pl.when(i < nu_ref[0])
    def _():
        _gather_rows(tok_ref, nb, u2_hbm, buf, sem)
        xb = buf[...].reshape(nb, D).astype(BF16)
        h = _dot(xb, wup_ref[0]) + bup_ref[0]
        h_glu = jnp.minimum(h[:, :D], SWIGLU_LIMIT)
        h_lin = jnp.clip(h[:, D:], -SWIGLU_LIMIT, SWIGLU_LIMIT)
        act = h_glu * _sigmoid(SWIGLU_ALPHA * h_glu) * (h_lin + 1.0)
        out = _dot(act.astype(BF16), wdn_ref[0]) + bdn_ref[0]
        o_ref[...] = out.reshape(o_ref.shape)

    @pl.when(i >= nu_ref[0])
    def _():
        o_ref[...] = jnp.zeros_like(o_ref)


def _moe(block_expert, n_used, row_tok, u2, wup, bup, wdn, bdn):
    nblk = block_expert.shape[0]
    nb = MOE_BLOCK
    ex = lambda i, be, nu: (be[i], 0, 0)
    grid_spec = pltpu.PrefetchScalarGridSpec(
        num_scalar_prefetch=2,
        grid=(nblk,),
        in_specs=[pl.BlockSpec((1, 1, nb), lambda i, be, nu: (i, 0, 0),
                               memory_space=pltpu.SMEM),
                  pl.BlockSpec(memory_space=pl.ANY),
                  pl.BlockSpec((1, D, 2 * D), ex),
                  pl.BlockSpec((1, 1, 2 * D), ex),
                  pl.BlockSpec((1, D, D), ex),
                  pl.BlockSpec((1, 1, D), ex)],
        out_specs=pl.BlockSpec((nb, 1, D), lambda i, be, nu: (i, 0, 0)),
        scratch_shapes=[pltpu.VMEM((nb, 1, D), F32), pltpu.SemaphoreType.DMA(())],
    )
    return pl.pallas_call(
        _moe_kernel,
        grid_spec=grid_spec,
        out_shape=jax.ShapeDtypeStruct((nblk * nb, 1, D), F32),
        compiler_params=pltpu.CompilerParams(
            dimension_semantics=("arbitrary",), vmem_limit_bytes=VMEM_LIMIT),
        name="moe",
    )(block_expert, n_used, row_tok, u2, wup, bup, wdn, bdn)


def _final_kernel(dest_ref, rows_hbm, x1_ref, gate_ref, g2_ref, lng_ref, lnb_ref,
                  o_ref, buf, sem):
    tm = x1_ref.shape[1]
    _gather_rows(dest_ref, TOP_K * tm, rows_hbm, buf, sem)
    lane = lax.broadcasted_iota(jnp.int32, (tm, LANES), 1)
    gates = gate_ref[0]
    ffn = jnp.zeros((tm, D), F32)
    for j in range(TOP_K):
        ffn = ffn + _lane_col(gates, j, lane) * buf[tm * j:tm * j + tm].reshape(tm, D)
    z = ALPHA * x1_ref[0] + (1.0 + g2_ref[0]) * ffn
    o_ref[0] = _ln(z) * lng_ref[...] + lnb_ref[...]


def _final(dest, rows, x1, gates, g2, lng, lnb):
    B, S, _ = x1.shape
    tm = TM_FINAL
    nst = S // tm
    tok = lambda b, s: (b, s, 0)
    c2 = lambda b, s: (0, 0)
    return pl.pallas_call(
        _final_kernel,
        grid=(B, nst),
        in_specs=[pl.BlockSpec((1, 1, TOP_K * tm), lambda b, s: (b * nst + s, 0, 0),
                               memory_space=pltpu.SMEM),
                  pl.BlockSpec(memory_space=pl.ANY),
                  pl.BlockSpec((1, tm, D), tok),
                  pl.BlockSpec((1, tm, LANES), tok),
                  pl.BlockSpec((1, 1, D), lambda b, s: (b, 0, 0)),
                  pl.BlockSpec((1, D), c2),
                  pl.BlockSpec((1, D), c2)],
        out_specs=pl.BlockSpec((1, tm, D), tok),
        out_shape=jax.ShapeDtypeStruct((B, S, D), F32),
        scratch_shapes=[pltpu.VMEM((TOP_K * tm, 1, D), F32), pltpu.SemaphoreType.DMA(())],
        compiler_params=pltpu.CompilerParams(
            dimension_semantics=("arbitrary", "arbitrary"), vmem_limit_bytes=VMEM_LIMIT),
        name="final",
    )(dest, rows, x1, gates, g2, lng, lnb)


def _routing(top_idx):
    T = top_idx.shape[0]
    A = T * TOP_K
    nb = MOE_BLOCK
    nblk = A // nb + N_EXP
    e_flat = top_idx.reshape(A)
    onehot = (e_flat[:, None] == jnp.arange(N_EXP, dtype=jnp.int32)[None, :]).astype(jnp.int32)
    csum = jnp.cumsum(onehot, axis=0)
    counts = csum[-1]
    rank = jnp.sum(csum * onehot, axis=1) - 1
    padded = (counts + nb - 1) // nb * nb
    padded_end = jnp.cumsum(padded)
    padded_start = padded_end - padded
    dest = padded_start[e_flat] + rank
    row_tok = jnp.zeros((nblk * nb,), jnp.int32).at[dest].set(
        jnp.arange(A, dtype=jnp.int32) // TOP_K)
    block_expert = jnp.minimum(
        jnp.searchsorted(padded_end, jnp.arange(nblk, dtype=jnp.int32) * nb, side='right'),
        N_EXP - 1).astype(jnp.int32)
    n_used = (padded_end[-1] // nb).astype(jnp.int32).reshape(1)
    last_e = block_expert[jnp.maximum(n_used[0] - 1, 0)]
    block_expert = jnp.where(jnp.arange(nblk) < n_used[0], block_expert, last_e)
    return block_expert, n_used, row_tok.reshape(nblk, 1, nb), dest.reshape(T, TOP_K)


def kernel(x, c, w_ada, b_ada, w_in, fox_f_bias, w_gla_gate, b_gla_gate, gla_norm_g,
           w_branch_a, w_branch_b, w_out, ln1_g, ln1_b, w_router, b_router,
           w_up, b_up, w_down, b_down, ln2_g, ln2_b):
    B, S, _ = x.shape
    T = B * S
    l = 0

    c_pad = jnp.zeros((8, D), F32).at[:B].set(c)
    mod = _ada(c_pad, w_ada[l], b_ada[l][None, :])[:B]
    mod6 = mod.reshape(B, 6, D)
    mod8 = jnp.concatenate([mod6, jnp.zeros((B, 2, D), F32)], axis=1)
    sh1, sc1 = mod6[:, 0:1], mod6[:, 1:2]
    g2 = mod6[:, 5:6]

    w = w_in[l]
    o = 0
    parts = []
    for width in (FOX_W, FOX_W, FOX_W, FOX_H, GLA_KW, GLA_KW, GLA_VW, GLA_VW, GLA_RANK, D, D):
        parts.append(w[:, o:o + width])
        o += width
    wq, wk, wv, wff, wgq, wgk, wgv, wgr, wglr, wga, wgb = parts

    def head_pad(m):
        m = m.reshape(D, FOX_H, FOX_DH)
        return jnp.concatenate([m, jnp.zeros_like(m)], axis=2).reshape(D, FOX_H * LANES)

    wfox = jnp.concatenate([head_pad(wq * FOX_DH ** -0.5), head_pad(wk), head_pad(wv)],
                           axis=1).astype(BF16)
    wgla = jnp.concatenate([wgq * GLA_DK ** -0.5, wgk, wgv, wgr], axis=1).astype(BF16)
    wsm = jnp.zeros((D, 2 * LANES), F32).at[:, :FOX_H].set(wff)
    wsm = wsm.at[:, LANES:LANES + GLA_RANK].set(wglr).astype(BF16)
    fb = jnp.zeros((1, LANES), F32).at[0, :FOX_H].set(fox_f_bias[l])
    tri = jnp.asarray(np.tril(np.ones((TM_IN, TM_IN), np.float32)), dtype=BF16)

    qa, ka, va, gq, gk, gv, gr, glr = _inproj(x, sh1, sc1, wfox, wgla, wsm, fb, tri)

    ya = _fox(qa, ka, va)

    cm3, masks = _gla_tables()
    yb = _gla(gq, gk, gv, gr, glr, w_gla_gate[l], b_gla_gate[l][None, :],
              gla_norm_g[l][None, :], jnp.asarray(cm3, dtype=BF16), jnp.asarray(masks))

    wgate = jnp.concatenate([wga, wgb], axis=1).astype(BF16)
    wr = jnp.zeros((D, LANES), F32).at[:, :N_EXP].set(w_router[l])
    br = jnp.full((1, LANES), NEG, F32).at[0, :N_EXP].set(b_router[l])
    x1, u2, topv, topi = _merge(
        x, ya, yb, mod8, wgate, w_branch_a[l].astype(BF16), w_branch_b[l].astype(BF16),
        w_out[l].astype(BF16), ln1_g[l][None, :], ln1_b[l][None, :], wr, br)

    block_expert, n_used, row_tok, dest = _routing(topi.reshape(T, LANES)[:, :TOP_K])
    rows = _moe(block_expert, n_used, row_tok, u2,
                w_up[l].astype(BF16), b_up[l][:, None, :],
                w_down[l].astype(BF16), b_down[l][:, None, :])

    nt = T // TM_FINAL
    dest_t = dest.reshape(nt, TM_FINAL, TOP_K).transpose(0, 2, 1).reshape(nt, 1, TOP_K * TM_FINAL)
    return _final(dest_t, rows, x1, topv, g2, ln2_g[l][None, :], ln2_b[l][None, :])
```

```python
import functools

import numpy as np
import jax
import jax.numpy as jnp
from jax import lax
from jax.experimental import pallas as pl
from jax.experimental.pallas import tpu as pltpu

F32 = jnp.float32
BF16 = jnp.bfloat16
HIGHEST = lax.Precision.HIGHEST

D = 1024
FOX_H = 8
FOX_DH = 64
FOX_W = FOX_H * FOX_DH
GLA_H = 4
GLA_DK = 128
GLA_DV = 256
GLA_KW = GLA_H * GLA_DK
GLA_VW = GLA_H * GLA_DV
GLA_RANK = 16
GLA_TAU = 16.0
N_EXP = 32
TOP_K = 4
SWIGLU_LIMIT = 7.0
SWIGLU_ALPHA = 1.702
EPS = 1e-5
DEPTH = 1
ALPHA = (2 * DEPTH) ** 0.25
LANES = 128

GLA_CHUNK = 64
GLA_BLOCK = 512
MOE_BLOCK = 512
TM_IN = 256
TM_MERGE = 256
TM_FINAL = 256
TQ = 512
VMEM_LIMIT = 56 * 1024 * 1024

NEG = -1e30


def _ln(x):
    mu = jnp.mean(x, axis=-1, keepdims=True)
    xc = x - mu
    var = jnp.mean(xc * xc, axis=-1, keepdims=True)
    return xc * lax.rsqrt(var + EPS)


def _sigmoid(x):
    return 1.0 / (1.0 + jnp.exp(-x))


def _log_sigmoid(x):
    return jnp.minimum(x, 0.0) - jnp.log(1.0 + jnp.exp(-jnp.abs(x)))


def _split3(x):
    hi = x.astype(BF16)
    r = x - hi.astype(F32)
    mid = r.astype(BF16)
    lo = (r - mid.astype(F32)).astype(BF16)
    return hi, mid, lo


def _lane_col(x, idx, lane):
    return jnp.sum(jnp.where(lane == idx, x, 0.0), axis=1, keepdims=True)


def _dot(a, b):
    return jnp.dot(a, b, preferred_element_type=F32)


def _dot_nt(a, b):
    return lax.dot_general(a, b, (((1,), (1,)), ((), ())), preferred_element_type=F32)


def _dot_tn(a, b):
    return lax.dot_general(a, b, (((0,), (0,)), ((), ())), preferred_element_type=F32)


def _ada_kernel(c_ref, w_ref, b_ref, o_ref):
    c = c_ref[...]
    ca = c * _sigmoid(c)
    o_ref[...] = jnp.dot(ca, w_ref[...], precision=HIGHEST,
                         preferred_element_type=F32) + b_ref[...]


def _ada(c_pad, w, b):
    n = w.shape[1]
    tn = 1536
    return pl.pallas_call(
        _ada_kernel,
        grid=(n // tn,),
        in_specs=[pl.BlockSpec((8, D), lambda j: (0, 0)),
                  pl.BlockSpec((D, tn), lambda j: (0, j)),
                  pl.BlockSpec((1, tn), lambda j: (0, j))],
        out_specs=pl.BlockSpec((8, tn), lambda j: (0, j)),
        out_shape=jax.ShapeDtypeStruct((8, n), F32),
        compiler_params=pltpu.CompilerParams(
            dimension_semantics=("arbitrary",), vmem_limit_bytes=VMEM_LIMIT),
        name="ada",
    )(c_pad, w, b)


def _inproj_kernel(x_ref, sh_ref, sc_ref, wfox_ref, wgla_ref, wsm_ref, fb_ref, tri_ref,
                   qa_ref, ka_ref, vt_ref, gq_ref, gk_ref, gv_ref, gr_ref, glr_ref,
                   carry_ref):
    tm = x_ref.shape[1]

    @pl.when(pl.program_id(1) == 0)
    def _():
        carry_ref[...] = jnp.zeros_like(carry_ref)

    u = _ln(x_ref[0]) * (1.0 + sc_ref[0]) + sh_ref[0]
    ub = u.astype(BF16)

    sm = _dot(ub, wsm_ref[...])
    glr_ref[0] = sm[:, LANES:LANES + GLA_RANK]
    lane = lax.broadcasted_iota(jnp.int32, (tm, LANES), 1)
    lf = jnp.where(lane < FOX_H, _log_sigmoid(sm[:, :LANES] + fb_ref[...]), 0.0)
    hi, mid, lo = _split3(lf)
    tri = tri_ref[...]
    cum = _dot(tri, hi) + _dot(tri, mid) + _dot(tri, lo) + carry_ref[...]
    carry_ref[...] = cum[tm - 1:tm, :]
    chi, cmid, clo = _split3(cum)
    chi, cmid, clo = chi.astype(F32), cmid.astype(F32), clo.astype(F32)

    ex_v = jnp.where(lane == FOX_DH, 1.0, 0.0)
    is_q1 = (lane >= FOX_DH + 3) & (lane < FOX_DH + 6)
    is_k1 = (lane >= FOX_DH) & (lane < FOX_DH + 3)
    for hp in range(FOX_H // 2):
        qp = _dot(ub, wfox_ref[:, 256 * hp:256 * hp + 256])
        kp = _dot(ub, wfox_ref[:, 1024 + 256 * hp:1024 + 256 * hp + 256])
        vp = _dot(ub, wfox_ref[:, 2048 + 256 * hp:2048 + 256 * hp + 256])
        for hh in range(2):
            h = 2 * hp + hh
            c0 = _lane_col(chi, h, lane)
            c1 = _lane_col(cmid, h, lane)
            c2 = _lane_col(clo, h, lane)
            ex_q = jnp.where(lane == FOX_DH, c0,
                             jnp.where(lane == FOX_DH + 1, c1,
                                       jnp.where(lane == FOX_DH + 2, c2,
                                                 jnp.where(is_q1, 1.0, 0.0))))
            ex_k = jnp.where(lane == FOX_DH + 3, -c0,
                             jnp.where(lane == FOX_DH + 4, -c1,
                                       jnp.where(lane == FOX_DH + 5, -c2,
                                                 jnp.where(is_k1, 1.0, 0.0))))
            sl = slice(LANES * hh, LANES * hh + LANES)
            qa_ref[0, h] = (qp[:, sl] + ex_q).astype(BF16)
            ka_ref[0, h] = (kp[:, sl] + ex_k).astype(BF16)
            vt_ref[0, h, 0] = (vp[:, sl] + ex_v).T.astype(BF16)

    for j in range(GLA_KW // 256):
        gq_ref[0, :, 256 * j:256 * j + 256] = _dot(
            ub, wgla_ref[:, 256 * j:256 * j + 256]).astype(BF16)
        gk_ref[0, :, 256 * j:256 * j + 256] = _dot(
            ub, wgla_ref[:, GLA_KW + 256 * j:GLA_KW + 256 * j + 256]).astype(BF16)
    for j in range(GLA_VW // 256):
        o = 2 * GLA_KW + 256 * j
        gv_ref[0, :, 256 * j:256 * j + 256] = _dot(ub, wgla_ref[:, o:o + 256]).astype(BF16)
        o = 2 * GLA_KW + GLA_VW + 256 * j
        gr_ref[0, :, 256 * j:256 * j + 256] = _dot(ub, wgla_ref[:, o:o + 256]).astype(BF16)


def _inproj(x, sh1, sc1, wfox, wgla, wsm, fb, tri):
    B, S, _ = x.shape
    tm = TM_IN
    const = lambda b, s: (0, 0)
    tok = lambda b, s: (b, s, 0)
    head = lambda b, s: (b, 0, s, 0)
    vec = lambda b, s: (b, 0, 0)
    hs = jax.ShapeDtypeStruct((B, FOX_H, S, LANES), BF16)
    per_q = TQ // tm
    return pl.pallas_call(
        _inproj_kernel,
        grid=(B, S // tm),
        in_specs=[pl.BlockSpec((1, tm, D), tok),
                  pl.BlockSpec((1, 1, D), vec),
                  pl.BlockSpec((1, 1, D), vec),
                  pl.BlockSpec(wfox.shape, const),
                  pl.BlockSpec(wgla.shape, const),
                  pl.BlockSpec(wsm.shape, const),
                  pl.BlockSpec((1, LANES), const),
                  pl.BlockSpec((tm, tm), const)],
        out_specs=[pl.BlockSpec((1, FOX_H, tm, LANES), head),
                   pl.BlockSpec((1, FOX_H, tm, LANES), head),
                   pl.BlockSpec((1, FOX_H, 1, LANES, tm),
                                lambda b, s: (b, 0, s // per_q, 0, s % per_q)),
                   pl.BlockSpec((1, tm, GLA_KW), tok),
                   pl.BlockSpec((1, tm, GLA_KW), tok),
                   pl.BlockSpec((1, tm, GLA_VW), tok),
                   pl.BlockSpec((1, tm, GLA_VW), tok),
                   pl.BlockSpec((1, tm, GLA_RANK), tok)],
        out_shape=[hs, hs, jax.ShapeDtypeStruct((B, FOX_H, S // TQ, LANES, TQ), BF16),
                   jax.ShapeDtypeStruct((B, S, GLA_KW), BF16),
                   jax.ShapeDtypeStruct((B, S, GLA_KW), BF16),
                   jax.ShapeDtypeStruct((B, S, GLA_VW), BF16),
                   jax.ShapeDtypeStruct((B, S, GLA_VW), BF16),
                   jax.ShapeDtypeStruct((B, S, GLA_RANK), F32)],
        scratch_shapes=[pltpu.VMEM((1, LANES), F32)],
        compiler_params=pltpu.CompilerParams(
            dimension_semantics=("parallel", "arbitrary"), vmem_limit_bytes=VMEM_LIMIT),
        name="inproj",
    )(x, sh1, sc1, wfox, wgla, wsm, fb, tri)


def _fox_kernel(q_ref, k_ref, vt_ref, o_ref, m_ref, acc_ref):
    qi = pl.program_id(2)
    tq = q_ref.shape[2]
    m_ref[...] = jnp.full(m_ref.shape, -jnp.inf, F32)
    acc_ref[...] = jnp.zeros_like(acc_ref)

    def block(j, masked):
        k0 = pl.multiple_of(j * tq, tq)
        for hh in range(2):
            s_t = _dot_nt(k_ref[0, hh, pl.ds(k0, tq), :], q_ref[0, hh])
            if masked:
                key = lax.broadcasted_iota(jnp.int32, (tq, tq), 0)
                qry = lax.broadcasted_iota(jnp.int32, (tq, tq), 1)
                s_t = jnp.where(key <= qry, s_t, -jnp.inf)
            m_old = m_ref[hh]
            m_new = jnp.maximum(m_old, jnp.max(s_t, axis=0, keepdims=True))
            p_t = jnp.exp(s_t - m_new).astype(BF16)
            acc_ref[hh] = (jnp.exp(m_old - m_new) * acc_ref[hh]
                           + _dot(vt_ref[0, hh, j], p_t))
            m_ref[hh] = m_new

    def body(j, carry):
        block(j, False)
        return carry

    lax.fori_loop(0, qi, body, 0)
    block(qi, True)

    outs = []
    for hh in range(2):
        acc = acc_ref[hh]
        outs.append((acc / acc[FOX_DH:FOX_DH + 1, :])[:FOX_DH])
    o_ref[0] = jnp.concatenate(outs, axis=0).T.astype(BF16)


def _fox(qa, ka, vt):
    B, H, S, _ = qa.shape
    tq = TQ
    return pl.pallas_call(
        _fox_kernel,
        grid=(B, H // 2, S // tq),
        in_specs=[pl.BlockSpec((1, 2, tq, LANES), lambda b, h, q: (b, h, q, 0)),
                  pl.BlockSpec((1, 2, S, LANES), lambda b, h, q: (b, h, 0, 0)),
                  pl.BlockSpec((1, 2, S // tq, LANES, tq), lambda b, h, q: (b, h, 0, 0, 0))],
        out_specs=pl.BlockSpec((1, tq, LANES), lambda b, h, q: (b, q, h)),
        out_shape=jax.ShapeDtypeStruct((B, S, FOX_W), BF16),
        scratch_shapes=[pltpu.VMEM((2, 1, tq), F32), pltpu.VMEM((2, LANES, tq), F32)],
        compiler_params=pltpu.CompilerParams(
            dimension_semantics=("parallel", "parallel", "arbitrary"),
            vmem_limit_bytes=VMEM_LIMIT),
        name="fox",
    )(qa, ka, vt)


def _gla_tables():
    C = GLA_CHUNK
    t = np.arange(C)[:, None]
    j = np.arange(C)[None, :]
    slabs = [(j <= t), (j > t)]
    masks = [np.eye(C, dtype=bool)]
    m = C // 2
    while m >= 1:
        g0 = (t // (2 * m)) * (2 * m)
        piv = g0 + m - 1
        upper = (t - g0) >= m
        slabs.append(np.where(upper, (j > piv) & (j <= t), (j > t) & (j <= piv)))
        s = np.arange(C)[None, :]
        masks.append(upper & ((s // (2 * m)) == (t // (2 * m))) & ((s % (2 * m)) < m))
        m //= 2
    cm = np.concatenate(slabs, axis=0).astype(np.float32)
    cm3 = np.concatenate([cm, cm, cm], axis=1)
    return cm3, np.stack(masks).astype(np.float32)


def _gla_kernel(gq_ref, gk_ref, gv_ref, gr_ref, glr_ref, wg_ref, bg_ref, ng_ref,
                cm_ref, mask_ref, o_ref, st_ref, la3_ref):
    C = GLA_CHUNK
    L = gq_ref.shape[1]
    n_lvl = mask_ref.shape[0] - 1

    @pl.when(pl.program_id(1) == 0)
    def _():
        st_ref[...] = jnp.zeros_like(st_ref)

    xg = jnp.dot(glr_ref[0], wg_ref[...], precision=HIGHEST,
                 preferred_element_type=F32) + bg_ref[...]
    la = _log_sigmoid(xg) * (1.0 / GLA_TAU)
    hi, mid, lo = _split3(la)
    for c in range(L // C):
        la3_ref[3 * C * c:3 * C * c + C, :] = hi[C * c:C * c + C]
        la3_ref[3 * C * c + C:3 * C * c + 2 * C, :] = mid[C * c:C * c + C]
        la3_ref[3 * C * c + 2 * C:3 * C * c + 3 * C, :] = lo[C * c:C * c + C]

    def chunk(ci, carry):
        r0 = pl.multiple_of(ci * C, C)
        la3 = la3_ref[pl.ds(pl.multiple_of(ci * 3 * C, 3 * C), 3 * C), :]
        w = jnp.exp(_dot(cm_ref[...], la3))
        q = gq_ref[0, pl.ds(r0, C), :].astype(F32)
        k = gk_ref[0, pl.ds(r0, C), :].astype(F32)
        for h in range(GLA_H):
            ks = slice(GLA_DK * h, GLA_DK * h + GLA_DK)
            vs = slice(GLA_DV * h, GLA_DV * h + GLA_DV)
            wh = w[:, ks]
            qh, kh = q[:, ks], k[:, ks]
            w_start, w_end = wh[0:C], wh[C:2 * C]
            st = st_ref[h]
            o = _dot_nt((qh * w_start).astype(BF16), st.astype(BF16))
            sc = mask_ref[0] * _dot_nt(qh.astype(BF16), kh.astype(BF16))
            for lv in range(n_lvl):
                wl = wh[(2 + lv) * C:(3 + lv) * C]
                sc = sc + mask_ref[1 + lv] * _dot_nt((qh * wl).astype(BF16),
                                                      (kh * wl).astype(BF16))
            vh = gv_ref[0, pl.ds(r0, C), vs]
            o = o + _dot(sc.astype(BF16), vh)
            st_ref[h] = st * w_start[C - 1:C, :] + _dot_tn(vh, (kh * w_end).astype(BF16))
            y = o * lax.rsqrt(jnp.mean(o * o, axis=1, keepdims=True) + EPS)
            g = gr_ref[0, pl.ds(r0, C), vs].astype(F32)
            o_ref[0, pl.ds(r0, C), vs] = (y * ng_ref[:, vs] * (g * _sigmoid(g))).astype(BF16)
        return carry

    lax.fori_loop(0, L // C, chunk, 0)


def _gla(gq, gk, gv, gr, glr, wg, bg, ng, cm3, masks):
    B, S, _ = gq.shape
    L = GLA_BLOCK
    tok = lambda b, s: (b, s, 0)
    c2 = lambda b, s: (0, 0)
    return pl.pallas_call(
        _gla_kernel,
        grid=(B, S // L),
        in_specs=[pl.BlockSpec((1, L, GLA_KW), tok),
                  pl.BlockSpec((1, L, GLA_KW), tok),
                  pl.BlockSpec((1, L, GLA_VW), tok),
                  pl.BlockSpec((1, L, GLA_VW), tok),
                  pl.BlockSpec((1, L, GLA_RANK), tok),
                  pl.BlockSpec(wg.shape, c2),
                  pl.BlockSpec(bg.shape, c2),
                  pl.BlockSpec(ng.shape, c2),
                  pl.BlockSpec(cm3.shape, c2),
                  pl.BlockSpec(masks.shape, lambda b, s: (0, 0, 0))],
        out_specs=pl.BlockSpec((1, L, GLA_VW), tok),
        out_shape=jax.ShapeDtypeStruct((B, S, GLA_VW), BF16),
        scratch_shapes=[pltpu.VMEM((GLA_H, GLA_DV, GLA_DK), F32),
                        pltpu.VMEM((3 * L, GLA_KW), BF16)],
        compiler_params=pltpu.CompilerParams(
            dimension_semantics=("parallel", "arbitrary"), vmem_limit_bytes=VMEM_LIMIT),
        name="gla",
    )(gq, gk, gv, gr, glr, wg, bg, ng, cm3, masks)


def _merge_kernel(x_ref, ya_ref, yb_ref, mod_ref, wgate_ref, wa_ref, wb_ref, wo_ref,
                  lng_ref, lnb_ref, wr_ref, br_ref,
                  x1_ref, u2_ref, topv_ref, topi_ref):
    tm = x_ref.shape[1]
    x = x_ref[0]
    mod = mod_ref[0]
    sh1, sc1, g1 = mod[0:1], mod[1:2], mod[2:3]
    sh2, sc2 = mod[3:4], mod[4:5]
    ub = (_ln(x) * (1.0 + sc1) + sh1).astype(BF16)
    br_a = _dot(ya_ref[0], wa_ref[...])
    br_b = _dot(yb_ref[0], wb_ref[...])
    merged = (_sigmoid(_dot(ub, wgate_ref[:, :D])) * br_a
              + _sigmoid(_dot(ub, wgate_ref[:, D:])) * br_b)
    mix = _dot(merged.astype(BF16), wo_ref[...])
    x1 = _ln(ALPHA * x + (1.0 + g1) * mix) * lng_ref[...] + lnb_ref[...]
    x1_ref[0] = x1
    u2 = _ln(x1) * (1.0 + sc2) + sh2
    u2_ref[...] = u2.reshape(u2_ref.shape)

    logits = jnp.dot(u2, wr_ref[...], precision=HIGHEST,
                     preferred_element_type=F32) + br_ref[...]
    lane = lax.broadcasted_iota(jnp.int32, (tm, LANES), 1)
    vals = jnp.zeros((tm, LANES), F32)
    idxs = jnp.zeros((tm, LANES), jnp.int32)
    cur = logits
    for k in range(TOP_K):
        mx = jnp.max(cur, axis=1, keepdims=True)
        ix = jnp.min(jnp.where(cur == mx, lane, LANES), axis=1, keepdims=True)
        vals = jnp.where(lane == k, mx, vals)
        idxs = jnp.where(lane == k, ix, idxs)
        cur = jnp.where(lane == ix, -jnp.inf, cur)
    v0 = jnp.max(jnp.where(lane < TOP_K, vals, -jnp.inf), axis=1, keepdims=True)
    e = jnp.where(lane < TOP_K, jnp.exp(vals - v0), 0.0)
    topv_ref[0] = e / jnp.sum(e, axis=1, keepdims=True)
    topi_ref[0] = idxs


def _merge(x, ya, yb, mod, wgate, wa, wb, wo, lng, lnb, wr, br):
    B, S, _ = x.shape
    tm = TM_MERGE
    tok = lambda b, s: (b, s, 0)
    c2 = lambda b, s: (0, 0)
    nst = S // tm
    return pl.pallas_call(
        _merge_kernel,
        grid=(B, nst),
        in_specs=[pl.BlockSpec((1, tm, D), tok),
                  pl.BlockSpec((1, tm, FOX_W), tok),
                  pl.BlockSpec((1, tm, GLA_VW), tok),
                  pl.BlockSpec((1, 8, D), lambda b, s: (b, 0, 0)),
                  pl.BlockSpec(wgate.shape, c2),
                  pl.BlockSpec(wa.shape, c2),
                  pl.BlockSpec(wb.shape, c2),
                  pl.BlockSpec(wo.shape, c2),
                  pl.BlockSpec((1, D), c2),
                  pl.BlockSpec((1, D), c2),
                  pl.BlockSpec(wr.shape, c2),
                  pl.BlockSpec((1, LANES), c2)],
        out_specs=[pl.BlockSpec((1, tm, D), tok),
                   pl.BlockSpec((tm, 1, D), lambda b, s: (b * nst + s, 0, 0)),
                   pl.BlockSpec((1, tm, LANES), tok),
                   pl.BlockSpec((1, tm, LANES), tok)],
        out_shape=[jax.ShapeDtypeStruct((B, S, D), F32),
                   jax.ShapeDtypeStruct((B * S, 1, D), F32),
                   jax.ShapeDtypeStruct((B, S, LANES), F32),
                   jax.ShapeDtypeStruct((B, S, LANES), jnp.int32)],
        compiler_params=pltpu.CompilerParams(
            dimension_semantics=("parallel", "arbitrary"), vmem_limit_bytes=VMEM_LIMIT),
        name="merge",
    )(x, ya, yb, mod, wgate, wa, wb, wo, lng, lnb, wr, br)


GATHER_UNROLL = 8


def _issue_rows(idx_ref, src_hbm, buf, sem):
    def issue(g, carry):
        for u in range(GATHER_UNROLL):
            r = g * GATHER_UNROLL + u
            pltpu.make_async_copy(src_hbm.at[idx_ref[0, 0, r]], buf.at[r], sem).start()
        return carry
    lax.fori_loop(0, buf.shape[0] // GATHER_UNROLL, issue, 0)


def _wait_rows(src_hbm, buf, sem):
    pltpu.make_async_copy(src_hbm.at[pl.ds(0, buf.shape[0])], buf, sem).wait()


def _pipelined_gather(step, n_steps, idx_ref, idx_next_ref, src_hbm, bufs, sems, flat_ref):
    @pl.when(step == 0)
    def _():
        _issue_rows(idx_ref, src_hbm, bufs[0], sems.at[0])

    for par in range(2):
        @pl.when((step + 1 < n_steps) & (step % 2 == par))
        def _():
            _issue_rows(idx_next_ref, src_hbm, bufs[1 - par], sems.at[1 - par])

    for par in range(2):
        @pl.when((step < n_steps) & (step % 2 == par))
        def _():
            _wait_rows(src_hbm, bufs[par], sems.at[par])
            flat_ref[...] = bufs[par][...].reshape(flat_ref.shape)


def _moe_kernel(be_ref, nu_ref, tok_ref, tok_next_ref, u2_hbm, wup_ref, bup_ref, wdn_ref,
                bdn_ref, o_ref, buf0, buf1, xs_ref, sems):
    i = pl.program_id(0)
    _pipelined_gather(i, nu_ref[0], tok_ref, tok_next_ref, u2_hbm, (buf0, buf1), sems, xs_ref)

    @pl.when(i < nu_ref[0])
    def _():
        xb = xs_ref[...].astype(BF16)
        h = _dot(xb, wup_ref[0]) + bup_ref[0]
        h_glu = jnp.minimum(h[:, :D], SWIGLU_LIMIT)
        h_lin = jnp.clip(h[:, D:], -SWIGLU_LIMIT, SWIGLU_LIMIT)
        act = h_glu * _sigmoid(SWIGLU_ALPHA * h_glu) * (h_lin + 1.0)
        out = _dot(act.astype(BF16), wdn_ref[0]) + bdn_ref[0]
        o_ref[...] = out.reshape(o_ref.shape)

    @pl.when(i >= nu_ref[0])
    def _():
        o_ref[...] = jnp.zeros_like(o_ref)


def _moe(block_expert, n_used, row_tok, u2, wup, bup, wdn, bdn):
    nblk = block_expert.shape[0]
    nb = MOE_BLOCK
    ex = lambda i, be, nu: (be[i], 0, 0)
    grid_spec = pltpu.PrefetchScalarGridSpec(
        num_scalar_prefetch=2,
        grid=(nblk,),
        in_specs=[pl.BlockSpec((1, 1, nb), lambda i, be, nu: (i, 0, 0),
                               memory_space=pltpu.SMEM),
                  pl.BlockSpec((1, 1, nb), lambda i, be, nu: (jnp.minimum(i + 1, nblk - 1), 0, 0),
                               memory_space=pltpu.SMEM),
                  pl.BlockSpec(memory_space=pl.ANY),
                  pl.BlockSpec((1, D, 2 * D), ex),
                  pl.BlockSpec((1, 1, 2 * D), ex),
                  pl.BlockSpec((1, D, D), ex),
                  pl.BlockSpec((1, 1, D), ex)],
        out_specs=pl.BlockSpec((nb, 1, D), lambda i, be, nu: (i, 0, 0)),
        scratch_shapes=[pltpu.VMEM((nb, 1, D), F32), pltpu.VMEM((nb, 1, D), F32),
                        pltpu.VMEM((nb, D), F32), pltpu.SemaphoreType.DMA((2,))],
    )
    return pl.pallas_call(
        _moe_kernel,
        grid_spec=grid_spec,
        out_shape=jax.ShapeDtypeStruct((nblk * nb, 1, D), F32),
        compiler_params=pltpu.CompilerParams(
            dimension_semantics=("arbitrary",), vmem_limit_bytes=VMEM_LIMIT),
        name="moe",
    )(block_expert, n_used, row_tok, row_tok, u2, wup, bup, wdn, bdn)


def _final_kernel(dest_ref, dest_next_ref, rows_hbm, x1_ref, gate_ref, g2_ref, lng_ref,
                  lnb_ref, o_ref, buf0, buf1, flat_ref, sems):
    tm = x1_ref.shape[0]
    _pipelined_gather(pl.program_id(0), pl.num_programs(0), dest_ref, dest_next_ref,
                      rows_hbm, (buf0, buf1), sems, flat_ref)
    lane = lax.broadcasted_iota(jnp.int32, (tm, LANES), 1)
    gates = gate_ref[...]
    ffn = jnp.zeros((tm, D), F32)
    for j in range(TOP_K):
        ffn = ffn + _lane_col(gates, j, lane) * flat_ref[tm * j:tm * j + tm, :]
    z = ALPHA * x1_ref[...] + (1.0 + g2_ref[0]) * ffn
    o_ref[...] = _ln(z) * lng_ref[...] + lnb_ref[...]


def _final(dest, rows, x1, gates, g2, lng, lnb, tiles_per_seq):
    T = x1.shape[0]
    tm = TM_FINAL
    nt = T // tm
    tok = lambda i: (i, 0)
    c2 = lambda i: (0, 0)
    n = TOP_K * tm
    return pl.pallas_call(
        _final_kernel,
        grid=(nt,),
        in_specs=[pl.BlockSpec((1, 1, n), lambda i: (i, 0, 0), memory_space=pltpu.SMEM),
                  pl.BlockSpec((1, 1, n), lambda i: (jnp.minimum(i + 1, nt - 1), 0, 0),
                               memory_space=pltpu.SMEM),
                  pl.BlockSpec(memory_space=pl.ANY),
                  pl.BlockSpec((tm, D), tok),
                  pl.BlockSpec((tm, LANES), tok),
                  pl.BlockSpec((1, 1, D), lambda i: (i // tiles_per_seq, 0, 0)),
                  pl.BlockSpec((1, D), c2),
                  pl.BlockSpec((1, D), c2)],
        out_specs=pl.BlockSpec((tm, D), tok),
        out_shape=jax.ShapeDtypeStruct((T, D), F32),
        scratch_shapes=[pltpu.VMEM((n, 1, D), F32), pltpu.VMEM((n, 1, D), F32),
                        pltpu.VMEM((n, D), F32), pltpu.SemaphoreType.DMA((2,))],
        compiler_params=pltpu.CompilerParams(
            dimension_semantics=("arbitrary",), vmem_limit_bytes=VMEM_LIMIT),
        name="final",
    )(dest, dest, rows, x1, gates, g2, lng, lnb)


def _routing(top_idx):
    T = top_idx.shape[0]
    A = T * TOP_K
    nb = MOE_BLOCK
    nblk = A // nb + N_EXP
    e_flat = top_idx.reshape(A)
    onehot = (e_flat[:, None] == jnp.arange(N_EXP, dtype=jnp.int32)[None, :]).astype(jnp.int32)
    csum = jnp.cumsum(onehot, axis=0)
    counts = csum[-1]
    rank = jnp.sum(csum * onehot, axis=1) - 1
    padded = (counts + nb - 1) // nb * nb
    padded_end = jnp.cumsum(padded)
    padded_start = padded_end - padded
    dest = padded_start[e_flat] + rank
    row_tok = jnp.zeros((nblk * nb,), jnp.int32).at[dest].set(
        jnp.arange(A, dtype=jnp.int32) // TOP_K)
    blk_row0 = jnp.arange(nblk, dtype=jnp.int32) * nb
    block_expert = jnp.minimum(
        jnp.sum((padded_end[None, :] <= blk_row0[:, None]).astype(jnp.int32), axis=1),
        N_EXP - 1)
    n_used = (padded_end[-1] // nb).astype(jnp.int32).reshape(1)
    last_e = block_expert[jnp.maximum(n_used[0] - 1, 0)]
    block_expert = jnp.where(jnp.arange(nblk) < n_used[0], block_expert, last_e)
    return block_expert, n_used, row_tok.reshape(nblk, 1, nb), dest.reshape(T, TOP_K)


def kernel(x, c, w_ada, b_ada, w_in, fox_f_bias, w_gla_gate, b_gla_gate, gla_norm_g,
           w_branch_a, w_branch_b, w_out, ln1_g, ln1_b, w_router, b_router,
           w_up, b_up, w_down, b_down, ln2_g, ln2_b):
    B, S, _ = x.shape
    T = B * S
    l = 0

    c_pad = jnp.zeros((8, D), F32).at[:B].set(c)
    mod = _ada(c_pad, w_ada[l], b_ada[l][None, :])[:B]
    mod6 = mod.reshape(B, 6, D)
    mod8 = jnp.concatenate([mod6, jnp.zeros((B, 2, D), F32)], axis=1)
    sh1, sc1 = mod6[:, 0:1], mod6[:, 1:2]
    g2 = mod6[:, 5:6]

    w = w_in[l]
    o = 0
    parts = []
    for width in (FOX_W, FOX_W, FOX_W, FOX_H, GLA_KW, GLA_KW, GLA_VW, GLA_VW, GLA_RANK, D, D):
        parts.append(w[:, o:o + width])
        o += width
    wq, wk, wv, wff, wgq, wgk, wgv, wgr, wglr, wga, wgb = parts

    def head_pad(m):
        m = m.reshape(D, FOX_H, FOX_DH)
        return jnp.concatenate([m, jnp.zeros_like(m)], axis=2).reshape(D, FOX_H * LANES)

    wfox = jnp.concatenate([head_pad(wq * FOX_DH ** -0.5), head_pad(wk), head_pad(wv)],
                           axis=1).astype(BF16)
    wgla = jnp.concatenate([wgq * GLA_DK ** -0.5, wgk, wgv, wgr], axis=1).astype(BF16)
    wsm = jnp.zeros((D, 2 * LANES), F32).at[:, :FOX_H].set(wff)
    wsm = wsm.at[:, LANES:LANES + GLA_RANK].set(wglr).astype(BF16)
    fb = jnp.zeros((1, LANES), F32).at[0, :FOX_H].set(fox_f_bias[l])
    tri = jnp.asarray(np.tril(np.ones((TM_IN, TM_IN), np.float32)), dtype=BF16)

    qa, ka, va, gq, gk, gv, gr, glr = _inproj(x, sh1, sc1, wfox, wgla, wsm, fb, tri)

    ya = _fox(qa, ka, va)

    cm3, masks = _gla_tables()
    yb = _gla(gq, gk, gv, gr, glr, w_gla_gate[l], b_gla_gate[l][None, :],
              gla_norm_g[l][None, :], jnp.asarray(cm3, dtype=BF16), jnp.asarray(masks))

    wgate = jnp.concatenate([wga, wgb], axis=1).astype(BF16)
    wr = jnp.zeros((D, LANES), F32).at[:, :N_EXP].set(w_router[l])
    br = jnp.full((1, LANES), NEG, F32).at[0, :N_EXP].set(b_router[l])
    x1, u2, topv, topi = _merge(
        x, ya, yb, mod8, wgate, w_branch_a[l].astype(BF16), w_branch_b[l].astype(BF16),
        w_out[l].astype(BF16), ln1_g[l][None, :], ln1_b[l][None, :], wr, br)

    block_expert, n_used, row_tok, dest = _routing(topi.reshape(T, LANES)[:, :TOP_K])
    rows = _moe(block_expert, n_used, row_tok, u2,
                w_up[l].astype(BF16), b_up[l][:, None, :],
                w_down[l].astype(BF16), b_down[l][:, None, :])

    nt = T // TM_FINAL
    dest_t = dest.reshape(nt, TM_FINAL, TOP_K).transpose(0, 2, 1).reshape(nt, 1, TOP_K * TM_FINAL)
    out = _final(dest_t, rows, x1.reshape(T, D), topv.reshape(T, LANES), g2,
                 ln2_g[l][None, :], ln2_b[l][None, :], S // TM_FINAL)
    return out.reshape(B, S, D)
```

```python
import functools

import numpy as np
import jax
import jax.numpy as jnp
from jax import lax
from jax.experimental import pallas as pl
from jax.experimental.pallas import tpu as pltpu

F32 = jnp.float32
BF16 = jnp.bfloat16
HIGHEST = lax.Precision.HIGHEST

D = 1024
FOX_H = 8
FOX_DH = 64
FOX_W = FOX_H * FOX_DH
GLA_H = 4
GLA_DK = 128
GLA_DV = 256
GLA_KW = GLA_H * GLA_DK
GLA_VW = GLA_H * GLA_DV
GLA_RANK = 16
GLA_TAU = 16.0
N_EXP = 32
TOP_K = 4
SWIGLU_LIMIT = 7.0
SWIGLU_ALPHA = 1.702
EPS = 1e-5
DEPTH = 1
ALPHA = (2 * DEPTH) ** 0.25
LANES = 128

GLA_CHUNK = 64
GLA_BLOCK = 512
GLA_UNROLL = 2
MOE_BLOCK = 512
TM_IN = 256
TM_MERGE = 512
TM_FINAL = 256
TQ = 512
VMEM_LIMIT = 56 * 1024 * 1024

NEG = -1e30
LOG2E = 1.4426950408889634


def _ln(x):
    mu = jnp.mean(x, axis=-1, keepdims=True)
    xc = x - mu
    var = jnp.mean(xc * xc, axis=-1, keepdims=True)
    return xc * lax.rsqrt(var + EPS)


def _sigmoid(x):
    return 1.0 / (1.0 + jnp.exp(-x))


def _log_sigmoid(x):
    return jnp.minimum(x, 0.0) - jnp.log(1.0 + jnp.exp(-jnp.abs(x)))


def _split3(x):
    hi = x.astype(BF16)
    r = x - hi.astype(F32)
    mid = r.astype(BF16)
    lo = (r - mid.astype(F32)).astype(BF16)
    return hi, mid, lo


def _lane_col(x, idx, lane):
    return jnp.sum(jnp.where(lane == idx, x, 0.0), axis=1, keepdims=True)


def _dot(a, b):
    return jnp.dot(a, b, preferred_element_type=F32)


def _dot_nt(a, b):
    return lax.dot_general(a, b, (((1,), (1,)), ((), ())), preferred_element_type=F32)


def _dot_tn(a, b):
    return lax.dot_general(a, b, (((0,), (0,)), ((), ())), preferred_element_type=F32)


def _ada_kernel(c_ref, w_ref, b_ref, o_ref):
    c = c_ref[...]
    ca = c * _sigmoid(c)
    o_ref[...] = jnp.dot(ca, w_ref[...], precision=HIGHEST,
                         preferred_element_type=F32) + b_ref[...]


def _ada(c_pad, w, b):
    n = w.shape[1]
    tn = 1536
    return pl.pallas_call(
        _ada_kernel,
        grid=(n // tn,),
        in_specs=[pl.BlockSpec((8, D), lambda j: (0, 0)),
                  pl.BlockSpec((D, tn), lambda j: (0, j)),
                  pl.BlockSpec((1, tn), lambda j: (0, j))],
        out_specs=pl.BlockSpec((8, tn), lambda j: (0, j)),
        out_shape=jax.ShapeDtypeStruct((8, n), F32),
        compiler_params=pltpu.CompilerParams(
            dimension_semantics=("arbitrary",), vmem_limit_bytes=VMEM_LIMIT),
        name="ada",
    )(c_pad, w, b)


def _inproj_kernel(x_ref, sh_ref, sc_ref, wfox_ref, wgla_ref, wsm_ref, fb_ref, tri_ref,
                   qa_ref, ka_ref, vt_ref, gq_ref, gk_ref, gv_ref, gr_ref, glr_ref,
                   carry_ref):
    tm = x_ref.shape[1]

    @pl.when(pl.program_id(1) == 0)
    def _():
        carry_ref[...] = jnp.zeros_like(carry_ref)

    u = _ln(x_ref[0]) * (1.0 + sc_ref[0]) + sh_ref[0]
    ub = u.astype(BF16)

    sm = _dot(ub, wsm_ref[...])
    glr_ref[0] = sm[:, LANES:LANES + GLA_RANK]
    lane = lax.broadcasted_iota(jnp.int32, (tm, LANES), 1)
    lf = jnp.where(lane < FOX_H, _log_sigmoid(sm[:, :LANES] + fb_ref[...]), 0.0)
    hi, mid, lo = _split3(lf)
    tri = tri_ref[...]
    cum = _dot(tri, hi) + _dot(tri, mid) + _dot(tri, lo) + carry_ref[...]
    carry_ref[...] = cum[tm - 1:tm, :]
    chi, cmid, clo = _split3(cum * LOG2E)
    chi, cmid, clo = chi.astype(F32), cmid.astype(F32), clo.astype(F32)

    ex_v = jnp.where(lane == FOX_DH, 1.0, 0.0)
    is_q1 = (lane >= FOX_DH + 3) & (lane < FOX_DH + 6)
    is_k1 = (lane >= FOX_DH) & (lane < FOX_DH + 3)
    for hp in range(FOX_H // 2):
        qp = _dot(ub, wfox_ref[:, 256 * hp:256 * hp + 256])
        kp = _dot(ub, wfox_ref[:, 1024 + 256 * hp:1024 + 256 * hp + 256])
        vp = _dot(ub, wfox_ref[:, 2048 + 256 * hp:2048 + 256 * hp + 256])
        for hh in range(2):
            h = 2 * hp + hh
            c0 = _lane_col(chi, h, lane)
            c1 = _lane_col(cmid, h, lane)
            c2 = _lane_col(clo, h, lane)
            ex_q = jnp.where(lane == FOX_DH, c0,
                             jnp.where(lane == FOX_DH + 1, c1,
                                       jnp.where(lane == FOX_DH + 2, c2,
                                                 jnp.where(is_q1, 1.0, 0.0))))
            ex_k = jnp.where(lane == FOX_DH + 3, -c0,
                             jnp.where(lane == FOX_DH + 4, -c1,
                                       jnp.where(lane == FOX_DH + 5, -c2,
                                                 jnp.where(is_k1, 1.0, 0.0))))
            sl = slice(LANES * hh, LANES * hh + LANES)
            qa_ref[0, h] = (qp[:, sl] + ex_q).astype(BF16)
            ka_ref[0, h] = (kp[:, sl] + ex_k).astype(BF16)
            vt_ref[0, h, 0] = (vp[:, sl] + ex_v).T.astype(BF16)

    for j in range(GLA_KW // 256):
        gq_ref[0, :, 256 * j:256 * j + 256] = _dot(
            ub, wgla_ref[:, 256 * j:256 * j + 256]).astype(BF16)
        gk_ref[0, :, 256 * j:256 * j + 256] = _dot(
            ub, wgla_ref[:, GLA_KW + 256 * j:GLA_KW + 256 * j + 256]).astype(BF16)
    for j in range(GLA_VW // 256):
        o = 2 * GLA_KW + 256 * j
        gv_ref[0, :, 256 * j:256 * j + 256] = _dot(ub, wgla_ref[:, o:o + 256]).astype(BF16)
        o = 2 * GLA_KW + GLA_VW + 256 * j
        gr_ref[0, :, 256 * j:256 * j + 256] = _dot(ub, wgla_ref[:, o:o + 256]).astype(BF16)


def _inproj(x, sh1, sc1, wfox, wgla, wsm, fb, tri):
    B, S, _ = x.shape
    tm = TM_IN
    const = lambda b, s: (0, 0)
    tok = lambda b, s: (b, s, 0)
    head = lambda b, s: (b, 0, s, 0)
    vec = lambda b, s: (b, 0, 0)
    hs = jax.ShapeDtypeStruct((B, FOX_H, S, LANES), BF16)
    per_q = TQ // tm
    return pl.pallas_call(
        _inproj_kernel,
        grid=(B, S // tm),
        in_specs=[pl.BlockSpec((1, tm, D), tok),
                  pl.BlockSpec((1, 1, D), vec),
                  pl.BlockSpec((1, 1, D), vec),
                  pl.BlockSpec(wfox.shape, const),
                  pl.BlockSpec(wgla.shape, const),
                  pl.BlockSpec(wsm.shape, const),
                  pl.BlockSpec((1, LANES), const),
                  pl.BlockSpec((tm, tm), const)],
        out_specs=[pl.BlockSpec((1, FOX_H, tm, LANES), head),
                   pl.BlockSpec((1, FOX_H, tm, LANES), head),
                   pl.BlockSpec((1, FOX_H, 1, LANES, tm),
                                lambda b, s: (b, 0, s // per_q, 0, s % per_q)),
                   pl.BlockSpec((1, tm, GLA_KW), tok),
                   pl.BlockSpec((1, tm, GLA_KW), tok),
                   pl.BlockSpec((1, tm, GLA_VW), tok),
                   pl.BlockSpec((1, tm, GLA_VW), tok),
                   pl.BlockSpec((1, tm, GLA_RANK), tok)],
        out_shape=[hs, hs, jax.ShapeDtypeStruct((B, FOX_H, S // TQ, LANES, TQ), BF16),
                   jax.ShapeDtypeStruct((B, S, GLA_KW), BF16),
                   jax.ShapeDtypeStruct((B, S, GLA_KW), BF16),
                   jax.ShapeDtypeStruct((B, S, GLA_VW), BF16),
                   jax.ShapeDtypeStruct((B, S, GLA_VW), BF16),
                   jax.ShapeDtypeStruct((B, S, GLA_RANK), F32)],
        scratch_shapes=[pltpu.VMEM((1, LANES), F32)],
        compiler_params=pltpu.CompilerParams(
            dimension_semantics=("parallel", "arbitrary"), vmem_limit_bytes=VMEM_LIMIT),
        name="inproj",
    )(x, sh1, sc1, wfox, wgla, wsm, fb, tri)


def _fox_kernel(q_ref, k_ref, vt_ref, o_ref, m_ref, acc_ref, sa_ref, sb_ref):
    qi = pl.program_id(2)
    tq = q_ref.shape[2]
    m_ref[...] = jnp.full(m_ref.shape, -jnp.inf, F32)
    acc_ref[...] = jnp.zeros_like(acc_ref)

    def scores(j, s_ref):
        k0 = pl.multiple_of(j * tq, tq)
        for hh in range(2):
            s_ref[hh] = _dot_nt(k_ref[0, hh, pl.ds(k0, tq), :], q_ref[0, hh])

    def update(j, s_ref, masked):
        for hh in range(2):
            s_t = s_ref[hh]
            if masked:
                key = lax.broadcasted_iota(jnp.int32, (tq, tq), 0)
                qry = lax.broadcasted_iota(jnp.int32, (tq, tq), 1)
                s_t = jnp.where(key <= qry, s_t, -jnp.inf)
            m_old = m_ref[hh]
            m_new = jnp.maximum(m_old, jnp.max(s_t, axis=0, keepdims=True))
            p_t = jnp.exp2(s_t - m_new).astype(BF16)
            acc_ref[hh] = (jnp.exp2(m_old - m_new) * acc_ref[hh]
                           + _dot(vt_ref[0, hh, j], p_t))
            m_ref[hh] = m_new

    scores(0, sa_ref)

    def body(t, carry):
        j = 2 * t
        scores(j + 1, sb_ref)
        update(j, sa_ref, False)
        scores(j + 2, sa_ref)
        update(j + 1, sb_ref, False)
        return carry

    lax.fori_loop(0, qi // 2, body, 0)

    @pl.when(qi % 2 == 0)
    def _():
        update(qi, sa_ref, True)

    @pl.when(qi % 2 == 1)
    def _():
        scores(qi, sb_ref)
        update(qi - 1, sa_ref, False)
        update(qi, sb_ref, True)

    outs = []
    for hh in range(2):
        acc = acc_ref[hh]
        outs.append((acc / acc[FOX_DH:FOX_DH + 1, :])[:FOX_DH])
    o_ref[0] = jnp.concatenate(outs, axis=0).T.astype(BF16)


def _fox(qa, ka, vt):
    B, H, S, _ = qa.shape
    tq = TQ
    return pl.pallas_call(
        _fox_kernel,
        grid=(B, H // 2, S // tq),
        in_specs=[pl.BlockSpec((1, 2, tq, LANES), lambda b, h, q: (b, h, q, 0)),
                  pl.BlockSpec((1, 2, S, LANES), lambda b, h, q: (b, h, 0, 0)),
                  pl.BlockSpec((1, 2, S // tq, LANES, tq), lambda b, h, q: (b, h, 0, 0, 0))],
        out_specs=pl.BlockSpec((1, tq, LANES), lambda b, h, q: (b, q, h)),
        out_shape=jax.ShapeDtypeStruct((B, S, FOX_W), BF16),
        scratch_shapes=[pltpu.VMEM((2, 1, tq), F32), pltpu.VMEM((2, LANES, tq), F32),
                        pltpu.VMEM((2, tq, tq), F32), pltpu.VMEM((2, tq, tq), F32)],
        compiler_params=pltpu.CompilerParams(
            dimension_semantics=("parallel", "parallel", "arbitrary"),
            vmem_limit_bytes=VMEM_LIMIT),
        name="fox",
    )(qa, ka, vt)


def _gla_tables():
    C = GLA_CHUNK
    t = np.arange(C)[:, None]
    j = np.arange(C)[None, :]
    slabs = [(j <= t), (j > t)]
    masks = [np.eye(C, dtype=bool)]
    m = C // 2
    while m >= 1:
        g0 = (t // (2 * m)) * (2 * m)
        piv = g0 + m - 1
        upper = (t - g0) >= m
        slabs.append(np.where(upper, (j > piv) & (j <= t), (j > t) & (j <= piv)))
        s = np.arange(C)[None, :]
        masks.append(upper & ((s // (2 * m)) == (t // (2 * m))) & ((s % (2 * m)) < m))
        m //= 2
    cm = np.concatenate(slabs, axis=0).astype(np.float32)
    cm3 = np.concatenate([cm, cm, cm], axis=1)
    return cm3, np.stack(masks).astype(np.float32)


def _gla_kernel(gq_ref, gk_ref, gv_ref, gr_ref, glr_ref, wg_ref, bg_ref, ng_ref,
                cm_ref, mask_ref, o_ref, st_ref, la3_ref):
    C = GLA_CHUNK
    L = gq_ref.shape[1]
    n_lvl = mask_ref.shape[0] - 1

    @pl.when(pl.program_id(1) == 0)
    def _():
        st_ref[...] = jnp.zeros_like(st_ref)

    xg = jnp.dot(glr_ref[0], wg_ref[...], precision=HIGHEST,
                 preferred_element_type=F32) + bg_ref[...]
    la = _log_sigmoid(xg) * (1.0 / GLA_TAU)
    hi, mid, lo = _split3(la)
    for c in range(L // C):
        la3_ref[3 * C * c:3 * C * c + C, :] = hi[C * c:C * c + C]
        la3_ref[3 * C * c + C:3 * C * c + 2 * C, :] = mid[C * c:C * c + C]
        la3_ref[3 * C * c + 2 * C:3 * C * c + 3 * C, :] = lo[C * c:C * c + C]

    def chunk(ci):
        r0 = pl.multiple_of(ci * C, C)
        la3 = la3_ref[pl.ds(pl.multiple_of(ci * 3 * C, 3 * C), 3 * C), :]
        w = jnp.exp(_dot(cm_ref[...], la3))
        q = gq_ref[0, pl.ds(r0, C), :].astype(F32)
        k = gk_ref[0, pl.ds(r0, C), :].astype(F32)
        for h in range(GLA_H):
            ks = slice(GLA_DK * h, GLA_DK * h + GLA_DK)
            vs = slice(GLA_DV * h, GLA_DV * h + GLA_DV)
            wh = w[:, ks]
            qh, kh = q[:, ks], k[:, ks]
            w_start, w_end = wh[0:C], wh[C:2 * C]
            st = st_ref[h]
            o = _dot_nt((qh * w_start).astype(BF16), st.astype(BF16))
            sc = mask_ref[0] * _dot_nt(qh.astype(BF16), kh.astype(BF16))
            for lv in range(n_lvl):
                wl = wh[(2 + lv) * C:(3 + lv) * C]
                sc = sc + mask_ref[1 + lv] * _dot_nt((qh * wl).astype(BF16),
                                                      (kh * wl).astype(BF16))
            vh = gv_ref[0, pl.ds(r0, C), vs]
            o = o + _dot(sc.astype(BF16), vh)
            st_ref[h] = st * w_start[C - 1:C, :] + _dot_tn(vh, (kh * w_end).astype(BF16))
            y = o * lax.rsqrt(jnp.mean(o * o, axis=1, keepdims=True) + EPS)
            g = gr_ref[0, pl.ds(r0, C), vs].astype(F32)
            o_ref[0, pl.ds(r0, C), vs] = (y * ng_ref[:, vs] * (g * _sigmoid(g))).astype(BF16)

    def chunk_group(gi, carry):
        for u in range(GLA_UNROLL):
            chunk(gi * GLA_UNROLL + u)
        return carry

    lax.fori_loop(0, L // (C * GLA_UNROLL), chunk_group, 0)


def _gla(gq, gk, gv, gr, glr, wg, bg, ng, cm3, masks):
    B, S, _ = gq.shape
    L = GLA_BLOCK
    tok = lambda b, s: (b, s, 0)
    c2 = lambda b, s: (0, 0)
    return pl.pallas_call(
        _gla_kernel,
        grid=(B, S // L),
        in_specs=[pl.BlockSpec((1, L, GLA_KW), tok),
                  pl.BlockSpec((1, L, GLA_KW), tok),
                  pl.BlockSpec((1, L, GLA_VW), tok),
                  pl.BlockSpec((1, L, GLA_VW), tok),
                  pl.BlockSpec((1, L, GLA_RANK), tok),
                  pl.BlockSpec(wg.shape, c2),
                  pl.BlockSpec(bg.shape, c2),
                  pl.BlockSpec(ng.shape, c2),
                  pl.BlockSpec(cm3.shape, c2),
                  pl.BlockSpec(masks.shape, lambda b, s: (0, 0, 0))],
        out_specs=pl.BlockSpec((1, L, GLA_VW), tok),
        out_shape=jax.ShapeDtypeStruct((B, S, GLA_VW), BF16),
        scratch_shapes=[pltpu.VMEM((GLA_H, GLA_DV, GLA_DK), F32),
                        pltpu.VMEM((3 * L, GLA_KW), BF16)],
        compiler_params=pltpu.CompilerParams(
            dimension_semantics=("parallel", "arbitrary"), vmem_limit_bytes=VMEM_LIMIT),
        name="gla",
    )(gq, gk, gv, gr, glr, wg, bg, ng, cm3, masks)


def _merge_kernel(x_ref, ya_ref, yb_ref, mod_ref, wgate_ref, wa_ref, wb_ref, wo_ref,
                  lng_ref, lnb_ref, wr_ref, br_ref,
                  x1_ref, u2_ref, topv_ref, topi_ref):
    tm = x_ref.shape[1]
    x = x_ref[0]
    mod = mod_ref[0]
    sh1, sc1, g1 = mod[0:1], mod[1:2], mod[2:3]
    sh2, sc2 = mod[3:4], mod[4:5]
    ub = (_ln(x) * (1.0 + sc1) + sh1).astype(BF16)
    br_a = _dot(ya_ref[0], wa_ref[...])
    br_b = _dot(yb_ref[0], wb_ref[...])
    merged = (_sigmoid(_dot(ub, wgate_ref[:, :D])) * br_a
              + _sigmoid(_dot(ub, wgate_ref[:, D:])) * br_b)
    mix = _dot(merged.astype(BF16), wo_ref[...])
    x1 = _ln(ALPHA * x + (1.0 + g1) * mix) * lng_ref[...] + lnb_ref[...]
    x1_ref[0] = x1
    u2 = _ln(x1) * (1.0 + sc2) + sh2
    u2_ref[...] = u2.reshape(u2_ref.shape)

    u_hi = u2.astype(BF16)
    u_lo = (u2 - u_hi.astype(F32)).astype(BF16)
    logits = (_dot(u_hi, wr_ref[0]) + _dot(u_lo, wr_ref[0]) + _dot(u_hi, wr_ref[1])
              + br_ref[...])
    lane = lax.broadcasted_iota(jnp.int32, (tm, LANES), 1)
    vals = jnp.zeros((tm, LANES), F32)
    idxs = jnp.zeros((tm, LANES), jnp.int32)
    cur = logits
    for k in range(TOP_K):
        mx = jnp.max(cur, axis=1, keepdims=True)
        ix = jnp.min(jnp.where(cur == mx, lane, LANES), axis=1, keepdims=True)
        vals = jnp.where(lane == k, mx, vals)
        idxs = jnp.where(lane == k, ix, idxs)
        cur = jnp.where(lane == ix, -jnp.inf, cur)
    v0 = jnp.max(jnp.where(lane < TOP_K, vals, -jnp.inf), axis=1, keepdims=True)
    e = jnp.where(lane < TOP_K, jnp.exp(vals - v0), 0.0)
    topv_ref[0] = e / jnp.sum(e, axis=1, keepdims=True)
    topi_ref[0] = idxs


def _merge(x, ya, yb, mod, wgate, wa, wb, wo, lng, lnb, wr, br):
    B, S, _ = x.shape
    tm = TM_MERGE
    tok = lambda b, s: (b, s, 0)
    c2 = lambda b, s: (0, 0)
    nst = S // tm
    return pl.pallas_call(
        _merge_kernel,
        grid=(B, nst),
        in_specs=[pl.BlockSpec((1, tm, D), tok),
                  pl.BlockSpec((1, tm, FOX_W), tok),
                  pl.BlockSpec((1, tm, GLA_VW), tok),
                  pl.BlockSpec((1, 8, D), lambda b, s: (b, 0, 0)),
                  pl.BlockSpec(wgate.shape, c2, pipeline_mode=pl.Buffered(1)),
                  pl.BlockSpec(wa.shape, c2, pipeline_mode=pl.Buffered(1)),
                  pl.BlockSpec(wb.shape, c2, pipeline_mode=pl.Buffered(1)),
                  pl.BlockSpec(wo.shape, c2, pipeline_mode=pl.Buffered(1)),
                  pl.BlockSpec((1, D), c2),
                  pl.BlockSpec((1, D), c2),
                  pl.BlockSpec(wr.shape, lambda b, s: (0, 0, 0)),
                  pl.BlockSpec((1, LANES), c2)],
        out_specs=[pl.BlockSpec((1, tm, D), tok),
                   pl.BlockSpec((tm, 1, D), lambda b, s: (b * nst + s, 0, 0)),
                   pl.BlockSpec((1, tm, LANES), tok),
                   pl.BlockSpec((1, tm, LANES), tok)],
        out_shape=[jax.ShapeDtypeStruct((B, S, D), F32),
                   jax.ShapeDtypeStruct((B * S, 1, D), F32),
                   jax.ShapeDtypeStruct((B, S, LANES), F32),
                   jax.ShapeDtypeStruct((B, S, LANES), jnp.int32)],
        compiler_params=pltpu.CompilerParams(
            dimension_semantics=("parallel", "arbitrary"), vmem_limit_bytes=VMEM_LIMIT),
        name="merge",
    )(x, ya, yb, mod, wgate, wa, wb, wo, lng, lnb, wr, br)


GATHER_UNROLL = 8


def _issue_rows(idx_ref, src_hbm, buf, sem):
    def issue(g, carry):
        for u in range(GATHER_UNROLL):
            r = g * GATHER_UNROLL + u
            pltpu.make_async_copy(src_hbm.at[idx_ref[0, 0, r]], buf.at[r], sem).start()
        return carry
    lax.fori_loop(0, buf.shape[0] // GATHER_UNROLL, issue, 0)


def _wait_rows(src_hbm, buf, sem):
    pltpu.make_async_copy(src_hbm.at[pl.ds(0, buf.shape[0])], buf, sem).wait()


def _pipelined_gather(step, n_steps, idx_ref, idx_next_ref, src_hbm, bufs, sems, flat_ref):
    @pl.when(step == 0)
    def _():
        _issue_rows(idx_ref, src_hbm, bufs[0], sems.at[0])

    for par in range(2):
        @pl.when((step + 1 < n_steps) & (step % 2 == par))
        def _():
            _issue_rows(idx_next_ref, src_hbm, bufs[1 - par], sems.at[1 - par])

    for par in range(2):
        @pl.when((step < n_steps) & (step % 2 == par))
        def _():
            _wait_rows(src_hbm, bufs[par], sems.at[par])
            flat_ref[...] = bufs[par][...].reshape(flat_ref.shape)


def _moe_kernel(be_ref, nu_ref, tok_ref, tok_next_ref, u2_hbm, wup_ref, bup_ref, wdn_ref,
                bdn_ref, o_ref, buf0, buf1, xs_ref, sems):
    i = pl.program_id(0)
    _pipelined_gather(i, nu_ref[0], tok_ref, tok_next_ref, u2_hbm, (buf0, buf1), sems, xs_ref)

    @pl.when(i < nu_ref[0])
    def _():
        xb = xs_ref[...].astype(BF16)
        h = _dot(xb, wup_ref[0]) + bup_ref[0]
        h_glu = jnp.minimum(h[:, :D], SWIGLU_LIMIT)
        h_lin = jnp.clip(h[:, D:], -SWIGLU_LIMIT, SWIGLU_LIMIT)
        act = h_glu * _sigmoid(SWIGLU_ALPHA * h_glu) * (h_lin + 1.0)
        out = _dot(act.astype(BF16), wdn_ref[0]) + bdn_ref[0]
        o_ref[...] = out.reshape(o_ref.shape)

    @pl.when(i >= nu_ref[0])
    def _():
        o_ref[...] = jnp.zeros_like(o_ref)


def _moe(block_expert, n_used, row_tok, u2, wup, bup, wdn, bdn):
    nblk = block_expert.shape[0]
    nb = MOE_BLOCK
    ex = lambda i, be, nu: (be[i], 0, 0)
    grid_spec = pltpu.PrefetchScalarGridSpec(
        num_scalar_prefetch=2,
        grid=(nblk,),
        in_specs=[pl.BlockSpec((1, 1, nb), lambda i, be, nu: (i, 0, 0),
                               memory_space=pltpu.SMEM),
                  pl.BlockSpec((1, 1, nb), lambda i, be, nu: (jnp.minimum(i + 1, nblk - 1), 0, 0),
                               memory_space=pltpu.SMEM),
                  pl.BlockSpec(memory_space=pl.ANY),
                  pl.BlockSpec((1, D, 2 * D), ex),
                  pl.BlockSpec((1, 1, 2 * D), ex),
                  pl.BlockSpec((1, D, D), ex),
                  pl.BlockSpec((1, 1, D), ex)],
        out_specs=pl.BlockSpec((nb, 1, D), lambda i, be, nu: (i, 0, 0)),
        scratch_shapes=[pltpu.VMEM((nb, 1, D), F32), pltpu.VMEM((nb, 1, D), F32),
                        pltpu.VMEM((nb, D), F32), pltpu.SemaphoreType.DMA((2,))],
    )
    return pl.pallas_call(
        _moe_kernel,
        grid_spec=grid_spec,
        out_shape=jax.ShapeDtypeStruct((nblk * nb, 1, D), F32),
        compiler_params=pltpu.CompilerParams(
            dimension_semantics=("arbitrary",), vmem_limit_bytes=VMEM_LIMIT),
        name="moe",
    )(block_expert, n_used, row_tok, row_tok, u2, wup, bup, wdn, bdn)


def _final_kernel(dest_ref, dest_next_ref, rows_hbm, x1_ref, gate_ref, g2_ref, lng_ref,
                  lnb_ref, o_ref, buf0, buf1, flat_ref, sems):
    tm = x1_ref.shape[0]
    _pipelined_gather(pl.program_id(0), pl.num_programs(0), dest_ref, dest_next_ref,
                      rows_hbm, (buf0, buf1), sems, flat_ref)
    lane = lax.broadcasted_iota(jnp.int32, (tm, LANES), 1)
    gates = gate_ref[...]
    ffn = jnp.zeros((tm, D), F32)
    for j in range(TOP_K):
        ffn = ffn + _lane_col(gates, j, lane) * flat_ref[tm * j:tm * j + tm, :]
    z = ALPHA * x1_ref[...] + (1.0 + g2_ref[0]) * ffn
    o_ref[...] = _ln(z) * lng_ref[...] + lnb_ref[...]


def _final(dest, rows, x1, gates, g2, lng, lnb, tiles_per_seq):
    T = x1.shape[0]
    tm = TM_FINAL
    nt = T // tm
    tok = lambda i: (i, 0)
    c2 = lambda i: (0, 0)
    n = TOP_K * tm
    return pl.pallas_call(
        _final_kernel,
        grid=(nt,),
        in_specs=[pl.BlockSpec((1, 1, n), lambda i: (i, 0, 0), memory_space=pltpu.SMEM),
                  pl.BlockSpec((1, 1, n), lambda i: (jnp.minimum(i + 1, nt - 1), 0, 0),
                               memory_space=pltpu.SMEM),
                  pl.BlockSpec(memory_space=pl.ANY),
                  pl.BlockSpec((tm, D), tok),
                  pl.BlockSpec((tm, LANES), tok),
                  pl.BlockSpec((1, 1, D), lambda i: (i // tiles_per_seq, 0, 0)),
                  pl.BlockSpec((1, D), c2),
                  pl.BlockSpec((1, D), c2)],
        out_specs=pl.BlockSpec((tm, D), tok),
        out_shape=jax.ShapeDtypeStruct((T, D), F32),
        scratch_shapes=[pltpu.VMEM((n, 1, D), F32), pltpu.VMEM((n, 1, D), F32),
                        pltpu.VMEM((n, D), F32), pltpu.SemaphoreType.DMA((2,))],
        compiler_params=pltpu.CompilerParams(
            dimension_semantics=("arbitrary",), vmem_limit_bytes=VMEM_LIMIT),
        name="final",
    )(dest, dest, rows, x1, gates, g2, lng, lnb)


def _routing(top_idx):
    T = top_idx.shape[0]
    A = T * TOP_K
    nb = MOE_BLOCK
    nblk = A // nb + N_EXP
    e_flat = top_idx.reshape(A)
    onehot = (e_flat[:, None] == jnp.arange(N_EXP, dtype=jnp.int32)[None, :]).astype(jnp.int32)
    csum = jnp.cumsum(onehot, axis=0)
    counts = csum[-1]
    rank = jnp.sum(csum * onehot, axis=1) - 1
    padded = (counts + nb - 1) // nb * nb
    padded_end = jnp.cumsum(padded)
    padded_start = padded_end - padded
    dest = padded_start[e_flat] + rank
    row_tok = jnp.zeros((nblk * nb,), jnp.int32).at[dest].set(
        jnp.arange(A, dtype=jnp.int32) // TOP_K)
    blk_row0 = jnp.arange(nblk, dtype=jnp.int32) * nb
    block_expert = jnp.minimum(
        jnp.sum((padded_end[None, :] <= blk_row0[:, None]).astype(jnp.int32), axis=1),
        N_EXP - 1)
    n_used = (padded_end[-1] // nb).astype(jnp.int32).reshape(1)
    last_e = block_expert[jnp.maximum(n_used[0] - 1, 0)]
    block_expert = jnp.where(jnp.arange(nblk) < n_used[0], block_expert, last_e)
    return block_expert, n_used, row_tok.reshape(nblk, 1, nb), dest.reshape(T, TOP_K)


def kernel(x, c, w_ada, b_ada, w_in, fox_f_bias, w_gla_gate, b_gla_gate, gla_norm_g,
           w_branch_a, w_branch_b, w_out, ln1_g, ln1_b, w_router, b_router,
           w_up, b_up, w_down, b_down, ln2_g, ln2_b):
    B, S, _ = x.shape
    T = B * S
    l = 0

    c_pad = jnp.zeros((8, D), F32).at[:B].set(c)
    mod = _ada(c_pad, w_ada[l], b_ada[l][None, :])[:B]
    mod6 = mod.reshape(B, 6, D)
    mod8 = jnp.concatenate([mod6, jnp.zeros((B, 2, D), F32)], axis=1)
    sh1, sc1 = mod6[:, 0:1], mod6[:, 1:2]
    g2 = mod6[:, 5:6]

    w = w_in[l]
    o = 0
    parts = []
    for width in (FOX_W, FOX_W, FOX_W, FOX_H, GLA_KW, GLA_KW, GLA_VW, GLA_VW, GLA_RANK, D, D):
        parts.append(w[:, o:o + width])
        o += width
    wq, wk, wv, wff, wgq, wgk, wgv, wgr, wglr, wga, wgb = parts

    def head_pad(m):
        m = m.reshape(D, FOX_H, FOX_DH)
        return jnp.concatenate([m, jnp.zeros_like(m)], axis=2).reshape(D, FOX_H * LANES)

    wfox = jnp.concatenate([head_pad(wq * (FOX_DH ** -0.5 * LOG2E)), head_pad(wk), head_pad(wv)],
                           axis=1).astype(BF16)
    wgla = jnp.concatenate([wgq * GLA_DK ** -0.5, wgk, wgv, wgr], axis=1).astype(BF16)
    wsm = jnp.zeros((D, 2 * LANES), F32).at[:, :FOX_H].set(wff)
    wsm = wsm.at[:, LANES:LANES + GLA_RANK].set(wglr).astype(BF16)
    fb = jnp.zeros((1, LANES), F32).at[0, :FOX_H].set(fox_f_bias[l])
    tri = jnp.asarray(np.tril(np.ones((TM_IN, TM_IN), np.float32)), dtype=BF16)

    qa, ka, va, gq, gk, gv, gr, glr = _inproj(x, sh1, sc1, wfox, wgla, wsm, fb, tri)

    ya = _fox(qa, ka, va)

    cm3, masks = _gla_tables()
    yb = _gla(gq, gk, gv, gr, glr, w_gla_gate[l], b_gla_gate[l][None, :],
              gla_norm_g[l][None, :], jnp.asarray(cm3, dtype=BF16), jnp.asarray(masks))

    wgate = jnp.concatenate([wga, wgb], axis=1).astype(BF16)
    wr = jnp.zeros((D, LANES), F32).at[:, :N_EXP].set(w_router[l])
    wr_hi = wr.astype(BF16)
    wr = jnp.stack([wr_hi, (wr - wr_hi.astype(F32)).astype(BF16)])
    br = jnp.full((1, LANES), NEG, F32).at[0, :N_EXP].set(b_router[l])
    x1, u2, topv, topi = _merge(
        x, ya, yb, mod8, wgate, w_branch_a[l].astype(BF16), w_branch_b[l].astype(BF16),
        w_out[l].astype(BF16), ln1_g[l][None, :], ln1_b[l][None, :], wr, br)

    block_expert, n_used, row_tok, dest = _routing(topi.reshape(T, LANES)[:, :TOP_K])
    rows = _moe(block_expert, n_used, row_tok, u2,
                w_up[l].astype(BF16), b_up[l][:, None, :],
                w_down[l].astype(BF16), b_down[l][:, None, :])

    nt = T // TM_FINAL
    dest_t = dest.reshape(nt, TM_FINAL, TOP_K).transpose(0, 2, 1).reshape(nt, 1, TOP_K * TM_FINAL)
    out = _final(dest_t, rows, x1.reshape(T, D), topv.reshape(T, LANES), g2,
                 ln2_g[l][None, :], ln2_b[l][None, :], S // TM_FINAL)
    return out.reshape(B, S, D)
```

```python
import functools

import numpy as np
import jax
import jax.numpy as jnp
from jax import lax
from jax.experimental import pallas as pl
from jax.experimental.pallas import tpu as pltpu

F32 = jnp.float32
BF16 = jnp.bfloat16
HIGHEST = lax.Precision.HIGHEST

D = 1024
FOX_H = 8
FOX_DH = 64
FOX_W = FOX_H * FOX_DH
GLA_H = 4
GLA_DK = 128
GLA_DV = 256
GLA_KW = GLA_H * GLA_DK
GLA_VW = GLA_H * GLA_DV
GLA_RANK = 16
GLA_TAU = 16.0
N_EXP = 32
TOP_K = 4
SWIGLU_LIMIT = 7.0
SWIGLU_ALPHA = 1.702
EPS = 1e-5
DEPTH = 1
ALPHA = (2 * DEPTH) ** 0.25
LANES = 128

GLA_CHUNK = 64
GLA_BLOCK = 512
GLA_UNROLL = 2
MOE_BLOCK = 512
TM_IN = 256
TM_MERGE = 512
TM_FINAL = 256
TQ = 512
VMEM_LIMIT = 56 * 1024 * 1024
VMEM_LIMIT_MOE = 60 * 1024 * 1024

NEG = -1e30
LOG2E = 1.4426950408889634


def _ln(x):
    mu = jnp.mean(x, axis=-1, keepdims=True)
    xc = x - mu
    var = jnp.mean(xc * xc, axis=-1, keepdims=True)
    return xc * lax.rsqrt(var + EPS)


def _sigmoid(x):
    return 1.0 / (1.0 + jnp.exp(-x))


def _log_sigmoid(x):
    return jnp.minimum(x, 0.0) - jnp.log(1.0 + jnp.exp(-jnp.abs(x)))


def _split3(x):
    hi = x.astype(BF16)
    r = x - hi.astype(F32)
    mid = r.astype(BF16)
    lo = (r - mid.astype(F32)).astype(BF16)
    return hi, mid, lo


def _lane_col(x, idx, lane):
    return jnp.sum(jnp.where(lane == idx, x, 0.0), axis=1, keepdims=True)


def _dot(a, b):
    return jnp.dot(a, b, preferred_element_type=F32)


def _dot_nt(a, b):
    return lax.dot_general(a, b, (((1,), (1,)), ((), ())), preferred_element_type=F32)


def _dot_tn(a, b):
    return lax.dot_general(a, b, (((0,), (0,)), ((), ())), preferred_element_type=F32)


def _ada_kernel(c_ref, w_ref, b_ref, o_ref):
    c = c_ref[...]
    ca = c * _sigmoid(c)
    o_ref[...] = jnp.dot(ca, w_ref[...], precision=HIGHEST,
                         preferred_element_type=F32) + b_ref[...]


def _ada(c_pad, w, b):
    n = w.shape[1]
    tn = 1536
    return pl.pallas_call(
        _ada_kernel,
        grid=(n // tn,),
        in_specs=[pl.BlockSpec((8, D), lambda j: (0, 0)),
                  pl.BlockSpec((D, tn), lambda j: (0, j)),
                  pl.BlockSpec((1, tn), lambda j: (0, j))],
        out_specs=pl.BlockSpec((8, tn), lambda j: (0, j)),
        out_shape=jax.ShapeDtypeStruct((8, n), F32),
        compiler_params=pltpu.CompilerParams(
            dimension_semantics=("arbitrary",), vmem_limit_bytes=VMEM_LIMIT),
        name="ada",
    )(c_pad, w, b)


def _inproj_kernel(x_ref, sh_ref, sc_ref, wfox_ref, wgla_ref, wsm_ref, fb_ref, tri_ref,
                   qa_ref, ka_ref, vt_ref, gq_ref, gk_ref, gv_ref, gr_ref, glr_ref,
                   carry_ref):
    tm = x_ref.shape[1]

    @pl.when(pl.program_id(1) == 0)
    def _():
        carry_ref[...] = jnp.zeros_like(carry_ref)

    u = _ln(x_ref[0]) * (1.0 + sc_ref[0]) + sh_ref[0]
    ub = u.astype(BF16)

    sm = _dot(ub, wsm_ref[...])
    glr_ref[0] = sm[:, LANES:LANES + GLA_RANK]
    lane = lax.broadcasted_iota(jnp.int32, (tm, LANES), 1)
    lf = jnp.where(lane < FOX_H, _log_sigmoid(sm[:, :LANES] + fb_ref[...]), 0.0)
    hi, mid, lo = _split3(lf)
    tri = tri_ref[...]
    cum = _dot(tri, hi) + _dot(tri, mid) + _dot(tri, lo) + carry_ref[...]
    carry_ref[...] = cum[tm - 1:tm, :]
    chi, cmid, clo = _split3(cum * LOG2E)
    chi, cmid, clo = chi.astype(F32), cmid.astype(F32), clo.astype(F32)

    ex_v = jnp.where(lane == FOX_DH, 1.0, 0.0)
    is_q1 = (lane >= FOX_DH + 3) & (lane < FOX_DH + 6)
    is_k1 = (lane >= FOX_DH) & (lane < FOX_DH + 3)
    for hp in range(FOX_H // 2):
        qp = _dot(ub, wfox_ref[:, 256 * hp:256 * hp + 256])
        kp = _dot(ub, wfox_ref[:, 1024 + 256 * hp:1024 + 256 * hp + 256])
        vp = _dot(ub, wfox_ref[:, 2048 + 256 * hp:2048 + 256 * hp + 256])
        for hh in range(2):
            h = 2 * hp + hh
            c0 = _lane_col(chi, h, lane)
            c1 = _lane_col(cmid, h, lane)
            c2 = _lane_col(clo, h, lane)
            ex_q = jnp.where(lane == FOX_DH, c0,
                             jnp.where(lane == FOX_DH + 1, c1,
                                       jnp.where(lane == FOX_DH + 2, c2,
                                                 jnp.where(is_q1, 1.0, 0.0))))
            ex_k = jnp.where(lane == FOX_DH + 3, -c0,
                             jnp.where(lane == FOX_DH + 4, -c1,
                                       jnp.where(lane == FOX_DH + 5, -c2,
                                                 jnp.where(is_k1, 1.0, 0.0))))
            sl = slice(LANES * hh, LANES * hh + LANES)
            qa_ref[0, h] = (qp[:, sl] + ex_q).astype(BF16)
            ka_ref[0, h] = (kp[:, sl] + ex_k).astype(BF16)
            vt_ref[0, h, 0] = (vp[:, sl] + ex_v).T.astype(BF16)

    for j in range(GLA_KW // 256):
        gq_ref[0, :, 256 * j:256 * j + 256] = _dot(
            ub, wgla_ref[:, 256 * j:256 * j + 256]).astype(BF16)
        gk_ref[0, :, 256 * j:256 * j + 256] = _dot(
            ub, wgla_ref[:, GLA_KW + 256 * j:GLA_KW + 256 * j + 256]).astype(BF16)
    for j in range(GLA_VW // 256):
        o = 2 * GLA_KW + 256 * j
        gv_ref[0, :, 256 * j:256 * j + 256] = _dot(ub, wgla_ref[:, o:o + 256]).astype(BF16)
        o = 2 * GLA_KW + GLA_VW + 256 * j
        gr_ref[0, :, 256 * j:256 * j + 256] = _dot(ub, wgla_ref[:, o:o + 256]).astype(BF16)


def _inproj(x, sh1, sc1, wfox, wgla, wsm, fb, tri):
    B, S, _ = x.shape
    tm = TM_IN
    const = lambda b, s: (0, 0)
    tok = lambda b, s: (b, s, 0)
    head = lambda b, s: (b, 0, s, 0)
    vec = lambda b, s: (b, 0, 0)
    hs = jax.ShapeDtypeStruct((B, FOX_H, S, LANES), BF16)
    per_q = TQ // tm
    return pl.pallas_call(
        _inproj_kernel,
        grid=(B, S // tm),
        in_specs=[pl.BlockSpec((1, tm, D), tok),
                  pl.BlockSpec((1, 1, D), vec),
                  pl.BlockSpec((1, 1, D), vec),
                  pl.BlockSpec(wfox.shape, const),
                  pl.BlockSpec(wgla.shape, const),
                  pl.BlockSpec(wsm.shape, const),
                  pl.BlockSpec((1, LANES), const),
                  pl.BlockSpec((tm, tm), const)],
        out_specs=[pl.BlockSpec((1, FOX_H, tm, LANES), head),
                   pl.BlockSpec((1, FOX_H, tm, LANES), head),
                   pl.BlockSpec((1, FOX_H, 1, LANES, tm),
                                lambda b, s: (b, 0, s // per_q, 0, s % per_q)),
                   pl.BlockSpec((1, tm, GLA_KW), tok),
                   pl.BlockSpec((1, tm, GLA_KW), tok),
                   pl.BlockSpec((1, tm, GLA_VW), tok),
                   pl.BlockSpec((1, tm, GLA_VW), tok),
                   pl.BlockSpec((1, tm, GLA_RANK), tok)],
        out_shape=[hs, hs, jax.ShapeDtypeStruct((B, FOX_H, S // TQ, LANES, TQ), BF16),
                   jax.ShapeDtypeStruct((B, S, GLA_KW), BF16),
                   jax.ShapeDtypeStruct((B, S, GLA_KW), BF16),
                   jax.ShapeDtypeStruct((B, S, GLA_VW), BF16),
                   jax.ShapeDtypeStruct((B, S, GLA_VW), BF16),
                   jax.ShapeDtypeStruct((B, S, GLA_RANK), F32)],
        scratch_shapes=[pltpu.VMEM((1, LANES), F32)],
        compiler_params=pltpu.CompilerParams(
            dimension_semantics=("parallel", "arbitrary"), vmem_limit_bytes=VMEM_LIMIT),
        name="inproj",
    )(x, sh1, sc1, wfox, wgla, wsm, fb, tri)


def _fox_kernel(q_ref, k_ref, vt_ref, o_ref, m_ref, acc_ref, sa_ref, sb_ref):
    qi = pl.program_id(2)
    tq = q_ref.shape[2]
    m_ref[...] = jnp.full(m_ref.shape, -jnp.inf, F32)
    acc_ref[...] = jnp.zeros_like(acc_ref)

    def scores(j, s_ref):
        k0 = pl.multiple_of(j * tq, tq)
        for hh in range(2):
            s_ref[hh] = _dot_nt(k_ref[0, hh, pl.ds(k0, tq), :], q_ref[0, hh])

    def update(j, s_ref, masked):
        for hh in range(2):
            s_t = s_ref[hh]
            if masked:
                key = lax.broadcasted_iota(jnp.int32, (tq, tq), 0)
                qry = lax.broadcasted_iota(jnp.int32, (tq, tq), 1)
                s_t = jnp.where(key <= qry, s_t, -jnp.inf)
            m_old = m_ref[hh]
            m_new = jnp.maximum(m_old, jnp.max(s_t, axis=0, keepdims=True))
            p_t = jnp.exp2(s_t - m_new).astype(BF16)
            acc_ref[hh] = (jnp.exp2(m_old - m_new) * acc_ref[hh]
                           + _dot(vt_ref[0, hh, j], p_t))
            m_ref[hh] = m_new

    scores(0, sa_ref)

    def body(t, carry):
        j = 2 * t
        scores(j + 1, sb_ref)
        update(j, sa_ref, False)
        scores(j + 2, sa_ref)
        update(j + 1, sb_ref, False)
        return carry

    lax.fori_loop(0, qi // 2, body, 0)

    @pl.when(qi % 2 == 0)
    def _():
        update(qi, sa_ref, True)

    @pl.when(qi % 2 == 1)
    def _():
        scores(qi, sb_ref)
        update(qi - 1, sa_ref, False)
        update(qi, sb_ref, True)

    outs = []
    for hh in range(2):
        acc = acc_ref[hh]
        outs.append((acc / acc[FOX_DH:FOX_DH + 1, :])[:FOX_DH])
    o_ref[0] = jnp.concatenate(outs, axis=0).T.astype(BF16)


def _fox(qa, ka, vt):
    B, H, S, _ = qa.shape
    tq = TQ
    return pl.pallas_call(
        _fox_kernel,
        grid=(B, H // 2, S // tq),
        in_specs=[pl.BlockSpec((1, 2, tq, LANES), lambda b, h, q: (b, h, q, 0)),
                  pl.BlockSpec((1, 2, S, LANES), lambda b, h, q: (b, h, 0, 0)),
                  pl.BlockSpec((1, 2, S // tq, LANES, tq), lambda b, h, q: (b, h, 0, 0, 0))],
        out_specs=pl.BlockSpec((1, tq, LANES), lambda b, h, q: (b, q, h)),
        out_shape=jax.ShapeDtypeStruct((B, S, FOX_W), BF16),
        scratch_shapes=[pltpu.VMEM((2, 1, tq), F32), pltpu.VMEM((2, LANES, tq), F32),
                        pltpu.VMEM((2, tq, tq), F32), pltpu.VMEM((2, tq, tq), F32)],
        compiler_params=pltpu.CompilerParams(
            dimension_semantics=("parallel", "parallel", "arbitrary"),
            vmem_limit_bytes=VMEM_LIMIT),
        name="fox",
    )(qa, ka, vt)


def _gla_tables():
    C = GLA_CHUNK
    t = np.arange(C)[:, None]
    j = np.arange(C)[None, :]
    slabs = [(j <= t), (j > t)]
    masks = [np.eye(C, dtype=bool)]
    m = C // 2
    while m >= 1:
        g0 = (t // (2 * m)) * (2 * m)
        piv = g0 + m - 1
        upper = (t - g0) >= m
        slabs.append(np.where(upper, (j > piv) & (j <= t), (j > t) & (j <= piv)))
        s = np.arange(C)[None, :]
        masks.append(upper & ((s // (2 * m)) == (t // (2 * m))) & ((s % (2 * m)) < m))
        m //= 2
    cm = np.concatenate(slabs, axis=0).astype(np.float32)
    cm3 = np.concatenate([cm, cm, cm], axis=1)
    return cm3, np.stack(masks).astype(np.float32)


def _gla_kernel(gq_ref, gk_ref, gv_ref, gr_ref, glr_ref, wg_ref, bg_ref, ng_ref,
                cm_ref, mask_ref, o_ref, st_ref, la3_ref):
    C = GLA_CHUNK
    L = gq_ref.shape[1]
    n_lvl = mask_ref.shape[0] - 1

    @pl.when(pl.program_id(1) == 0)
    def _():
        st_ref[...] = jnp.zeros_like(st_ref)

    xg = jnp.dot(glr_ref[0], wg_ref[...], precision=HIGHEST,
                 preferred_element_type=F32) + bg_ref[...]
    la = _log_sigmoid(xg) * (1.0 / GLA_TAU)
    hi, mid, lo = _split3(la)
    for c in range(L // C):
        la3_ref[3 * C * c:3 * C * c + C, :] = hi[C * c:C * c + C]
        la3_ref[3 * C * c + C:3 * C * c + 2 * C, :] = mid[C * c:C * c + C]
        la3_ref[3 * C * c + 2 * C:3 * C * c + 3 * C, :] = lo[C * c:C * c + C]

    def chunk_group(gi, carry):
        pairs = [(u, h) for u in range(GLA_UNROLL) for h in range(GLA_H)]
        r0 = [pl.multiple_of((gi * GLA_UNROLL + u) * C, C) for u in range(GLA_UNROLL)]
        ks = [slice(GLA_DK * h, GLA_DK * h + GLA_DK) for h in range(GLA_H)]
        vs = [slice(GLA_DV * h, GLA_DV * h + GLA_DV) for h in range(GLA_H)]

        w, q, k = [], [], []
        for u in range(GLA_UNROLL):
            a0 = pl.multiple_of((gi * GLA_UNROLL + u) * 3 * C, 3 * C)
            w.append(jnp.exp(_dot(cm_ref[...], la3_ref[pl.ds(a0, 3 * C), :])))
            q.append(gq_ref[0, pl.ds(r0[u], C), :].astype(F32))
            k.append(gk_ref[0, pl.ds(r0[u], C), :].astype(F32))

        q_in, k_out, dec, q_lv, k_lv, v = {}, {}, {}, {}, {}, {}
        for u, h in pairs:
            wh, qh, kh = w[u][:, ks[h]], q[u][:, ks[h]], k[u][:, ks[h]]
            q_in[u, h] = (qh * wh[0:C]).astype(BF16)
            k_out[u, h] = (kh * wh[C:2 * C]).astype(BF16)
            dec[u, h] = wh[C - 1:C, :]
            q_lv[u, h] = [qh.astype(BF16)] + [(qh * wh[(2 + lv) * C:(3 + lv) * C]).astype(BF16)
                                              for lv in range(n_lvl)]
            k_lv[u, h] = [kh.astype(BF16)] + [(kh * wh[(2 + lv) * C:(3 + lv) * C]).astype(BF16)
                                              for lv in range(n_lvl)]
            v[u, h] = gv_ref[0, pl.ds(r0[u], C), vs[h]]

        upd = {p: _dot_tn(v[p], k_out[p]) for p in pairs}
        sc_parts = {p: [_dot_nt(a, b) for a, b in zip(q_lv[p], k_lv[p])] for p in pairs}

        inter = {}
        for h in range(GLA_H):
            st = st_ref[h]
            for u in range(GLA_UNROLL):
                inter[u, h] = _dot_nt(q_in[u, h], st.astype(BF16))
                st = st * dec[u, h] + upd[u, h]
            st_ref[h] = st

        sc = {}
        for p in pairs:
            acc = mask_ref[0] * sc_parts[p][0]
            for lv in range(n_lvl):
                acc = acc + mask_ref[1 + lv] * sc_parts[p][1 + lv]
            sc[p] = acc.astype(BF16)
        intra = {p: _dot(sc[p], v[p]) for p in pairs}
        for u, h in pairs:
            o = inter[u, h] + intra[u, h]
            y = o * lax.rsqrt(jnp.mean(o * o, axis=1, keepdims=True) + EPS)
            g = gr_ref[0, pl.ds(r0[u], C), vs[h]].astype(F32)
            o_ref[0, pl.ds(r0[u], C), vs[h]] = (
                y * ng_ref[:, vs[h]] * (g * _sigmoid(g))).astype(BF16)
        return carry

    lax.fori_loop(0, L // (C * GLA_UNROLL), chunk_group, 0)


def _gla(gq, gk, gv, gr, glr, wg, bg, ng, cm3, masks):
    B, S, _ = gq.shape
    L = GLA_BLOCK
    tok = lambda b, s: (b, s, 0)
    c2 = lambda b, s: (0, 0)
    return pl.pallas_call(
        _gla_kernel,
        grid=(B, S // L),
        in_specs=[pl.BlockSpec((1, L, GLA_KW), tok),
                  pl.BlockSpec((1, L, GLA_KW), tok),
                  pl.BlockSpec((1, L, GLA_VW), tok),
                  pl.BlockSpec((1, L, GLA_VW), tok),
                  pl.BlockSpec((1, L, GLA_RANK), tok),
                  pl.BlockSpec(wg.shape, c2),
                  pl.BlockSpec(bg.shape, c2),
                  pl.BlockSpec(ng.shape, c2),
                  pl.BlockSpec(cm3.shape, c2),
                  pl.BlockSpec(masks.shape, lambda b, s: (0, 0, 0))],
        out_specs=pl.BlockSpec((1, L, GLA_VW), tok),
        out_shape=jax.ShapeDtypeStruct((B, S, GLA_VW), BF16),
        scratch_shapes=[pltpu.VMEM((GLA_H, GLA_DV, GLA_DK), F32),
                        pltpu.VMEM((3 * L, GLA_KW), BF16)],
        compiler_params=pltpu.CompilerParams(
            dimension_semantics=("parallel", "arbitrary"), vmem_limit_bytes=VMEM_LIMIT),
        name="gla",
    )(gq, gk, gv, gr, glr, wg, bg, ng, cm3, masks)


def _merge_kernel(x_ref, ya_ref, yb_ref, mod_ref, wgate_ref, wa_ref, wb_ref, wo_ref,
                  lng_ref, lnb_ref, wr_ref, br_ref,
                  x1_ref, u2_ref, topv_ref, topi_ref):
    tm = x_ref.shape[1]
    x = x_ref[0]
    mod = mod_ref[0]
    sh1, sc1, g1 = mod[0:1], mod[1:2], mod[2:3]
    sh2, sc2 = mod[3:4], mod[4:5]
    ub = (_ln(x) * (1.0 + sc1) + sh1).astype(BF16)
    br_a = _dot(ya_ref[0], wa_ref[...])
    br_b = _dot(yb_ref[0], wb_ref[...])
    merged = (_sigmoid(_dot(ub, wgate_ref[:, :D])) * br_a
              + _sigmoid(_dot(ub, wgate_ref[:, D:])) * br_b)
    mix = _dot(merged.astype(BF16), wo_ref[...])
    x1 = _ln(ALPHA * x + (1.0 + g1) * mix) * lng_ref[...] + lnb_ref[...]
    x1_ref[0] = x1
    u2 = _ln(x1) * (1.0 + sc2) + sh2
    u2_ref[...] = u2.reshape(u2_ref.shape)

    u_hi = u2.astype(BF16)
    u_lo = (u2 - u_hi.astype(F32)).astype(BF16)
    logits = (_dot(u_hi, wr_ref[0]) + _dot(u_lo, wr_ref[0]) + _dot(u_hi, wr_ref[1])
              + br_ref[...])
    lane = lax.broadcasted_iota(jnp.int32, (tm, LANES), 1)
    vals = jnp.zeros((tm, LANES), F32)
    idxs = jnp.zeros((tm, LANES), jnp.int32)
    cur = logits
    for k in range(TOP_K):
        mx = jnp.max(cur, axis=1, keepdims=True)
        ix = jnp.min(jnp.where(cur == mx, lane, LANES), axis=1, keepdims=True)
        vals = jnp.where(lane == k, mx, vals)
        idxs = jnp.where(lane == k, ix, idxs)
        cur = jnp.where(lane == ix, -jnp.inf, cur)
    v0 = jnp.max(jnp.where(lane < TOP_K, vals, -jnp.inf), axis=1, keepdims=True)
    e = jnp.where(lane < TOP_K, jnp.exp(vals - v0), 0.0)
    topv_ref[0] = e / jnp.sum(e, axis=1, keepdims=True)
    topi_ref[0] = idxs


def _merge(x, ya, yb, mod, wgate, wa, wb, wo, lng, lnb, wr, br):
    B, S, _ = x.shape
    tm = TM_MERGE
    tok = lambda b, s: (b, s, 0)
    c2 = lambda b, s: (0, 0)
    nst = S // tm
    return pl.pallas_call(
        _merge_kernel,
        grid=(B, nst),
        in_specs=[pl.BlockSpec((1, tm, D), tok),
                  pl.BlockSpec((1, tm, FOX_W), tok),
                  pl.BlockSpec((1, tm, GLA_VW), tok),
                  pl.BlockSpec((1, 8, D), lambda b, s: (b, 0, 0)),
                  pl.BlockSpec(wgate.shape, c2, pipeline_mode=pl.Buffered(1)),
                  pl.BlockSpec(wa.shape, c2, pipeline_mode=pl.Buffered(1)),
                  pl.BlockSpec(wb.shape, c2, pipeline_mode=pl.Buffered(1)),
                  pl.BlockSpec(wo.shape, c2, pipeline_mode=pl.Buffered(1)),
                  pl.BlockSpec((1, D), c2),
                  pl.BlockSpec((1, D), c2),
                  pl.BlockSpec(wr.shape, lambda b, s: (0, 0, 0)),
                  pl.BlockSpec((1, LANES), c2)],
        out_specs=[pl.BlockSpec((1, tm, D), tok),
                   pl.BlockSpec((tm, 1, D), lambda b, s: (b * nst + s, 0, 0)),
                   pl.BlockSpec((1, tm, LANES), tok),
                   pl.BlockSpec((1, tm, LANES), tok)],
        out_shape=[jax.ShapeDtypeStruct((B, S, D), F32),
                   jax.ShapeDtypeStruct((B * S, 1, D), F32),
                   jax.ShapeDtypeStruct((B, S, LANES), F32),
                   jax.ShapeDtypeStruct((B, S, LANES), jnp.int32)],
        compiler_params=pltpu.CompilerParams(
            dimension_semantics=("parallel", "arbitrary"), vmem_limit_bytes=VMEM_LIMIT),
        name="merge",
    )(x, ya, yb, mod, wgate, wa, wb, wo, lng, lnb, wr, br)


GATHER_UNROLL = 8


def _issue_rows(idx_ref, src_hbm, buf, sem):
    def issue(g, carry):
        for u in range(GATHER_UNROLL):
            r = g * GATHER_UNROLL + u
            pltpu.make_async_copy(src_hbm.at[idx_ref[0, 0, r]], buf.at[r], sem).start()
        return carry
    lax.fori_loop(0, buf.shape[0] // GATHER_UNROLL, issue, 0)


def _wait_rows(src_hbm, buf, sem):
    pltpu.make_async_copy(src_hbm.at[pl.ds(0, buf.shape[0])], buf, sem).wait()


def _gather_step(step, n_steps, idx_ref, idx_next_ref, src_hbm, bufs, sems, flat_ref, compute):
    @pl.when(step == 0)
    def _():
        _issue_rows(idx_ref, src_hbm, bufs[0], sems.at[0])

    for par in range(2):
        @pl.when((step < n_steps) & (step % 2 == par))
        def _():
            nxt, cur = bufs[1 - par], bufs[par]
            for r in range(nxt.shape[0]):
                pltpu.make_async_copy(src_hbm.at[idx_next_ref[0, 0, r]], nxt.at[r],
                                      sems.at[1 - par]).start()
            _wait_rows(src_hbm, cur, sems.at[par])
            flat_ref[...] = cur[...].reshape(flat_ref.shape)
            compute()

            @pl.when(step == n_steps - 1)
            def _():
                _wait_rows(src_hbm, nxt, sems.at[1 - par])


def _moe_kernel(be_ref, nu_ref, tok_ref, tok_next_ref, u2_hbm, wup_ref, bup_ref, wdn_ref,
                bdn_ref, o_ref, buf0, buf1, xs_ref, wup_bf, wdn_bf, sems):
    i = pl.program_id(0)
    n_used = nu_ref[0]

    @pl.when((i < n_used) & ((i == 0) | (be_ref[i] != be_ref[jnp.maximum(i - 1, 0)])))
    def _():
        wup_bf[...] = wup_ref[0].astype(BF16)
        wdn_bf[...] = wdn_ref[0].astype(BF16)

    def compute():
        xb = xs_ref[...].astype(BF16)
        h = _dot(xb, wup_bf[...]) + bup_ref[0]
        h_glu = jnp.minimum(h[:, :D], SWIGLU_LIMIT)
        h_lin = jnp.clip(h[:, D:], -SWIGLU_LIMIT, SWIGLU_LIMIT)
        act = h_glu * _sigmoid(SWIGLU_ALPHA * h_glu) * (h_lin + 1.0)
        out = _dot(act.astype(BF16), wdn_bf[...]) + bdn_ref[0]
        o_ref[...] = out.reshape(o_ref.shape)

    _gather_step(i, n_used, tok_ref, tok_next_ref, u2_hbm, (buf0, buf1), sems, xs_ref, compute)

    @pl.when(i >= n_used)
    def _():
        o_ref[...] = jnp.zeros_like(o_ref)


def _moe(block_expert, n_used, row_tok, u2, wup, bup, wdn, bdn):
    nblk = block_expert.shape[0]
    nb = MOE_BLOCK
    ex = lambda i, be, nu: (be[i], 0, 0)
    grid_spec = pltpu.PrefetchScalarGridSpec(
        num_scalar_prefetch=2,
        grid=(nblk,),
        in_specs=[pl.BlockSpec((1, 1, nb), lambda i, be, nu: (i, 0, 0),
                               memory_space=pltpu.SMEM),
                  pl.BlockSpec((1, 1, nb),
                               lambda i, be, nu: (jnp.maximum(jnp.minimum(i + 1, nu[0] - 1), 0), 0, 0),
                               memory_space=pltpu.SMEM),
                  pl.BlockSpec(memory_space=pl.ANY),
                  pl.BlockSpec((1, D, 2 * D), ex),
                  pl.BlockSpec((1, 1, 2 * D), ex),
                  pl.BlockSpec((1, D, D), ex),
                  pl.BlockSpec((1, 1, D), ex)],
        out_specs=pl.BlockSpec((nb, 1, D), lambda i, be, nu: (i, 0, 0)),
        scratch_shapes=[pltpu.VMEM((nb, 1, D), F32), pltpu.VMEM((nb, 1, D), F32),
                        pltpu.VMEM((nb, D), F32),
                        pltpu.VMEM((D, 2 * D), BF16), pltpu.VMEM((D, D), BF16),
                        pltpu.SemaphoreType.DMA((2,))],
    )
    return pl.pallas_call(
        _moe_kernel,
        grid_spec=grid_spec,
        out_shape=jax.ShapeDtypeStruct((nblk * nb, 1, D), F32),
        compiler_params=pltpu.CompilerParams(
            dimension_semantics=("arbitrary",), vmem_limit_bytes=VMEM_LIMIT_MOE),
        name="moe",
    )(block_expert, n_used, row_tok, row_tok, u2, wup, bup, wdn, bdn)


def _final_kernel(dest_ref, dest_next_ref, rows_hbm, x1_ref, gate_ref, g2_ref, lng_ref,
                  lnb_ref, o_ref, buf0, buf1, flat_ref, sems):
    tm = x1_ref.shape[0]

    def compute():
        lane = lax.broadcasted_iota(jnp.int32, (tm, LANES), 1)
        gates = gate_ref[...]
        ffn = jnp.zeros((tm, D), F32)
        for j in range(TOP_K):
            ffn = ffn + _lane_col(gates, j, lane) * flat_ref[tm * j:tm * j + tm, :]
        z = ALPHA * x1_ref[...] + (1.0 + g2_ref[0]) * ffn
        o_ref[...] = _ln(z) * lng_ref[...] + lnb_ref[...]

    _gather_step(pl.program_id(0), pl.num_programs(0), dest_ref, dest_next_ref,
                 rows_hbm, (buf0, buf1), sems, flat_ref, compute)


def _final(dest, rows, x1, gates, g2, lng, lnb, tiles_per_seq):
    T = x1.shape[0]
    tm = TM_FINAL
    nt = T // tm
    tok = lambda i: (i, 0)
    c2 = lambda i: (0, 0)
    n = TOP_K * tm
    return pl.pallas_call(
        _final_kernel,
        grid=(nt,),
        in_specs=[pl.BlockSpec((1, 1, n), lambda i: (i, 0, 0), memory_space=pltpu.SMEM),
                  pl.BlockSpec((1, 1, n), lambda i: (jnp.minimum(i + 1, nt - 1), 0, 0),
                               memory_space=pltpu.SMEM),
                  pl.BlockSpec(memory_space=pl.ANY),
                  pl.BlockSpec((tm, D), tok),
                  pl.BlockSpec((tm, LANES), tok),
                  pl.BlockSpec((1, 1, D), lambda i: (i // tiles_per_seq, 0, 0)),
                  pl.BlockSpec((1, D), c2),
                  pl.BlockSpec((1, D), c2)],
        out_specs=pl.BlockSpec((tm, D), tok),
        out_shape=jax.ShapeDtypeStruct((T, D), F32),
        scratch_shapes=[pltpu.VMEM((n, 1, D), F32), pltpu.VMEM((n, 1, D), F32),
                        pltpu.VMEM((n, D), F32), pltpu.SemaphoreType.DMA((2,))],
        compiler_params=pltpu.CompilerParams(
            dimension_semantics=("arbitrary",), vmem_limit_bytes=VMEM_LIMIT),
        name="final",
    )(dest, dest, rows, x1, gates, g2, lng, lnb)


def _routing(top_idx):
    T = top_idx.shape[0]
    A = T * TOP_K
    nb = MOE_BLOCK
    nblk = A // nb + N_EXP
    e_flat = top_idx.reshape(A)
    onehot = (e_flat[:, None] == jnp.arange(N_EXP, dtype=jnp.int32)[None, :]).astype(jnp.int32)
    csum = jnp.cumsum(onehot, axis=0)
    counts = csum[-1]
    rank = jnp.sum(csum * onehot, axis=1) - 1
    padded = (counts + nb - 1) // nb * nb
    padded_end = jnp.cumsum(padded)
    padded_start = padded_end - padded
    dest = padded_start[e_flat] + rank
    row_tok = jnp.zeros((nblk * nb,), jnp.int32).at[dest].set(
        jnp.arange(A, dtype=jnp.int32) // TOP_K)
    blk_row0 = jnp.arange(nblk, dtype=jnp.int32) * nb
    block_expert = jnp.minimum(
        jnp.sum((padded_end[None, :] <= blk_row0[:, None]).astype(jnp.int32), axis=1),
        N_EXP - 1)
    n_used = (padded_end[-1] // nb).astype(jnp.int32).reshape(1)
    last_e = block_expert[jnp.maximum(n_used[0] - 1, 0)]
    block_expert = jnp.where(jnp.arange(nblk) < n_used[0], block_expert, last_e)
    return block_expert, n_used, row_tok.reshape(nblk, 1, nb), dest.reshape(T, TOP_K)


def kernel(x, c, w_ada, b_ada, w_in, fox_f_bias, w_gla_gate, b_gla_gate, gla_norm_g,
           w_branch_a, w_branch_b, w_out, ln1_g, ln1_b, w_router, b_router,
           w_up, b_up, w_down, b_down, ln2_g, ln2_b):
    B, S, _ = x.shape
    T = B * S
    l = 0

    c_pad = jnp.zeros((8, D), F32).at[:B].set(c)
    mod = _ada(c_pad, w_ada[l], b_ada[l][None, :])[:B]
    mod6 = mod.reshape(B, 6, D)
    mod8 = jnp.concatenate([mod6, jnp.zeros((B, 2, D), F32)], axis=1)
    sh1, sc1 = mod6[:, 0:1], mod6[:, 1:2]
    g2 = mod6[:, 5:6]

    w = w_in[l]
    o = 0
    parts = []
    for width in (FOX_W, FOX_W, FOX_W, FOX_H, GLA_KW, GLA_KW, GLA_VW, GLA_VW, GLA_RANK, D, D):
        parts.append(w[:, o:o + width])
        o += width
    wq, wk, wv, wff, wgq, wgk, wgv, wgr, wglr, wga, wgb = parts

    def head_pad(m):
        m = m.reshape(D, FOX_H, FOX_DH)
        return jnp.concatenate([m, jnp.zeros_like(m)], axis=2).reshape(D, FOX_H * LANES)

    wfox = jnp.concatenate([head_pad(wq * (FOX_DH ** -0.5 * LOG2E)), head_pad(wk), head_pad(wv)],
                           axis=1).astype(BF16)
    wgla = jnp.concatenate([wgq * GLA_DK ** -0.5, wgk, wgv, wgr], axis=1).astype(BF16)
    wsm = jnp.zeros((D, 2 * LANES), F32).at[:, :FOX_H].set(wff)
    wsm = wsm.at[:, LANES:LANES + GLA_RANK].set(wglr).astype(BF16)
    fb = jnp.zeros((1, LANES), F32).at[0, :FOX_H].set(fox_f_bias[l])
    tri = jnp.asarray(np.tril(np.ones((TM_IN, TM_IN), np.float32)), dtype=BF16)

    qa, ka, va, gq, gk, gv, gr, glr = _inproj(x, sh1, sc1, wfox, wgla, wsm, fb, tri)

    ya = _fox(qa, ka, va)

    cm3, masks = _gla_tables()
    yb = _gla(gq, gk, gv, gr, glr, w_gla_gate[l], b_gla_gate[l][None, :],
              gla_norm_g[l][None, :], jnp.asarray(cm3, dtype=BF16), jnp.asarray(masks))

    wgate = jnp.concatenate([wga, wgb], axis=1).astype(BF16)
    wr = jnp.zeros((D, LANES), F32).at[:, :N_EXP].set(w_router[l])
    wr_hi = wr.astype(BF16)
    wr = jnp.stack([wr_hi, (wr - wr_hi.astype(F32)).astype(BF16)])
    br = jnp.full((1, LANES), NEG, F32).at[0, :N_EXP].set(b_router[l])
    x1, u2, topv, topi = _merge(
        x, ya, yb, mod8, wgate, w_branch_a[l].astype(BF16), w_branch_b[l].astype(BF16),
        w_out[l].astype(BF16), ln1_g[l][None, :], ln1_b[l][None, :], wr, br)

    block_expert, n_used, row_tok, dest = _routing(topi.reshape(T, LANES)[:, :TOP_K])
    rows = _moe(block_expert, n_used, row_tok, u2,
                w_up[l], b_up[l][:, None, :], w_down[l], b_down[l][:, None, :])

    nt = T // TM_FINAL
    dest_t = dest.reshape(nt, TM_FINAL, TOP_K).transpose(0, 2, 1).reshape(nt, 1, TOP_K * TM_FINAL)
    out = _final(dest_t, rows, x1.reshape(T, D), topv.reshape(T, LANES), g2,
                 ln2_g[l][None, :], ln2_b[l][None, :], S // TM_FINAL)
    return out.reshape(B, S, D)
```

```python
import functools

import numpy as np
import jax
import jax.numpy as jnp
from jax import lax
from jax.experimental import pallas as pl
from jax.experimental.pallas import tpu as pltpu

F32 = jnp.float32
BF16 = jnp.bfloat16
HIGHEST = lax.Precision.HIGHEST

D = 1024
FOX_H = 8
FOX_DH = 64
FOX_W = FOX_H * FOX_DH
GLA_H = 4
GLA_DK = 128
GLA_DV = 256
GLA_KW = GLA_H * GLA_DK
GLA_VW = GLA_H * GLA_DV
GLA_RANK = 16
GLA_TAU = 16.0
N_EXP = 32
TOP_K = 4
SWIGLU_LIMIT = 7.0
SWIGLU_ALPHA = 1.702
EPS = 1e-5
DEPTH = 1
ALPHA = (2 * DEPTH) ** 0.25
LANES = 128
ROW_TILE = 8

GLA_CHUNK = 64
GLA_BLOCK = 512
GLA_UNROLL = 2
MOE_BLOCK = 512
TM_IN = 256
TM_MERGE = 512
TM_FINAL = 256
TQ = 512
VMEM_LIMIT = 56 * 1024 * 1024
VMEM_LIMIT_MOE = 60 * 1024 * 1024

NEG = -1e30
LOG2E = 1.4426950408889634


def _ln(x):
    mu = jnp.mean(x, axis=-1, keepdims=True)
    xc = x - mu
    var = jnp.mean(xc * xc, axis=-1, keepdims=True)
    return xc * lax.rsqrt(var + EPS)


def _sigmoid(x):
    return 1.0 / (1.0 + jnp.exp(-x))


def _log_sigmoid(x):
    return jnp.minimum(x, 0.0) - jnp.log(1.0 + jnp.exp(-jnp.abs(x)))


def _split3(x):
    hi = x.astype(BF16)
    r = x - hi.astype(F32)
    mid = r.astype(BF16)
    lo = (r - mid.astype(F32)).astype(BF16)
    return hi, mid, lo


def _lane_col(x, idx, lane):
    return jnp.sum(jnp.where(lane == idx, x, 0.0), axis=1, keepdims=True)


def _dot(a, b):
    return jnp.dot(a, b, preferred_element_type=F32)


def _dot_nt(a, b):
    return lax.dot_general(a, b, (((1,), (1,)), ((), ())), preferred_element_type=F32)


def _dot_tn(a, b):
    return lax.dot_general(a, b, (((0,), (0,)), ((), ())), preferred_element_type=F32)


def _ada_kernel(c_ref, w_ref, b_ref, o_ref):
    c = c_ref[...]
    ca = c * _sigmoid(c)
    o_ref[...] = jnp.dot(ca, w_ref[...], precision=HIGHEST,
                         preferred_element_type=F32) + b_ref[...]


def _ada(c_pad, w, b):
    n = w.shape[1]
    tn = 1536
    return pl.pallas_call(
        _ada_kernel,
        grid=(n // tn,),
        in_specs=[pl.BlockSpec((8, D), lambda j: (0, 0)),
                  pl.BlockSpec((D, tn), lambda j: (0, j)),
                  pl.BlockSpec((1, tn), lambda j: (0, j))],
        out_specs=pl.BlockSpec((8, tn), lambda j: (0, j)),
        out_shape=jax.ShapeDtypeStruct((8, n), F32),
        compiler_params=pltpu.CompilerParams(
            dimension_semantics=("arbitrary",), vmem_limit_bytes=VMEM_LIMIT),
        name="ada",
    )(c_pad, w, b)


def _inproj_kernel(x_ref, sh_ref, sc_ref, wfox_ref, wgla_ref, wsm_ref, fb_ref, tri_ref,
                   qa_ref, ka_ref, vt_ref, gq_ref, gk_ref, gv_ref, gr_ref, glr_ref,
                   carry_ref):
    tm = x_ref.shape[1]

    @pl.when(pl.program_id(1) == 0)
    def _():
        carry_ref[...] = jnp.zeros_like(carry_ref)

    u = _ln(x_ref[0]) * (1.0 + sc_ref[0]) + sh_ref[0]
    ub = u.astype(BF16)

    sm = _dot(ub, wsm_ref[...])
    glr_ref[0] = sm[:, LANES:LANES + GLA_RANK]
    lane = lax.broadcasted_iota(jnp.int32, (tm, LANES), 1)
    lf = jnp.where(lane < FOX_H, _log_sigmoid(sm[:, :LANES] + fb_ref[...]), 0.0)
    hi, mid, lo = _split3(lf)
    tri = tri_ref[...]
    cum = _dot(tri, hi) + _dot(tri, mid) + _dot(tri, lo) + carry_ref[...]
    carry_ref[...] = cum[tm - 1:tm, :]
    chi, cmid, clo = _split3(cum * LOG2E)
    chi, cmid, clo = chi.astype(F32), cmid.astype(F32), clo.astype(F32)

    ex_v = jnp.where(lane == FOX_DH, 1.0, 0.0)
    is_q1 = (lane >= FOX_DH + 3) & (lane < FOX_DH + 6)
    is_k1 = (lane >= FOX_DH) & (lane < FOX_DH + 3)
    for hp in range(FOX_H // 2):
        qp = _dot(ub, wfox_ref[:, 256 * hp:256 * hp + 256])
        kp = _dot(ub, wfox_ref[:, 1024 + 256 * hp:1024 + 256 * hp + 256])
        vp = _dot(ub, wfox_ref[:, 2048 + 256 * hp:2048 + 256 * hp + 256])
        for hh in range(2):
            h = 2 * hp + hh
            c0 = _lane_col(chi, h, lane)
            c1 = _lane_col(cmid, h, lane)
            c2 = _lane_col(clo, h, lane)
            ex_q = jnp.where(lane == FOX_DH, c0,
                             jnp.where(lane == FOX_DH + 1, c1,
                                       jnp.where(lane == FOX_DH + 2, c2,
                                                 jnp.where(is_q1, 1.0, 0.0))))
            ex_k = jnp.where(lane == FOX_DH + 3, -c0,
                             jnp.where(lane == FOX_DH + 4, -c1,
                                       jnp.where(lane == FOX_DH + 5, -c2,
                                                 jnp.where(is_k1, 1.0, 0.0))))
            sl = slice(LANES * hh, LANES * hh + LANES)
            qa_ref[0, h] = (qp[:, sl] + ex_q).astype(BF16)
            ka_ref[0, h] = (kp[:, sl] + ex_k).astype(BF16)
            vt_ref[0, h, 0] = (vp[:, sl] + ex_v).T.astype(BF16)

    for j in range(GLA_KW // 256):
        gq_ref[0, :, 256 * j:256 * j + 256] = _dot(
            ub, wgla_ref[:, 256 * j:256 * j + 256]).astype(BF16)
        gk_ref[0, :, 256 * j:256 * j + 256] = _dot(
            ub, wgla_ref[:, GLA_KW + 256 * j:GLA_KW + 256 * j + 256]).astype(BF16)
    for j in range(GLA_VW // 256):
        o = 2 * GLA_KW + 256 * j
        gv_ref[0, :, 256 * j:256 * j + 256] = _dot(ub, wgla_ref[:, o:o + 256]).astype(BF16)
        o = 2 * GLA_KW + GLA_VW + 256 * j
        gr_ref[0, :, 256 * j:256 * j + 256] = _dot(ub, wgla_ref[:, o:o + 256]).astype(BF16)


def _inproj(x, sh1, sc1, wfox, wgla, wsm, fb, tri):
    B, S, _ = x.shape
    tm = TM_IN
    const = lambda b, s: (0, 0)
    tok = lambda b, s: (b, s, 0)
    head = lambda b, s: (b, 0, s, 0)
    vec = lambda b, s: (b, 0, 0)
    hs = jax.ShapeDtypeStruct((B, FOX_H, S, LANES), BF16)
    per_q = TQ // tm
    return pl.pallas_call(
        _inproj_kernel,
        grid=(B, S // tm),
        in_specs=[pl.BlockSpec((1, tm, D), tok),
                  pl.BlockSpec((1, 1, D), vec),
                  pl.BlockSpec((1, 1, D), vec),
                  pl.BlockSpec(wfox.shape, const),
                  pl.BlockSpec(wgla.shape, const),
                  pl.BlockSpec(wsm.shape, const),
                  pl.BlockSpec((1, LANES), const),
                  pl.BlockSpec((tm, tm), const)],
        out_specs=[pl.BlockSpec((1, FOX_H, tm, LANES), head),
                   pl.BlockSpec((1, FOX_H, tm, LANES), head),
                   pl.BlockSpec((1, FOX_H, 1, LANES, tm),
                                lambda b, s: (b, 0, s // per_q, 0, s % per_q)),
                   pl.BlockSpec((1, tm, GLA_KW), tok),
                   pl.BlockSpec((1, tm, GLA_KW), tok),
                   pl.BlockSpec((1, tm, GLA_VW), tok),
                   pl.BlockSpec((1, tm, GLA_VW), tok),
                   pl.BlockSpec((1, tm, GLA_RANK), tok)],
        out_shape=[hs, hs, jax.ShapeDtypeStruct((B, FOX_H, S // TQ, LANES, TQ), BF16),
                   jax.ShapeDtypeStruct((B, S, GLA_KW), BF16),
                   jax.ShapeDtypeStruct((B, S, GLA_KW), BF16),
                   jax.ShapeDtypeStruct((B, S, GLA_VW), BF16),
                   jax.ShapeDtypeStruct((B, S, GLA_VW), BF16),
                   jax.ShapeDtypeStruct((B, S, GLA_RANK), F32)],
        scratch_shapes=[pltpu.VMEM((1, LANES), F32)],
        compiler_params=pltpu.CompilerParams(
            dimension_semantics=("parallel", "arbitrary"), vmem_limit_bytes=VMEM_LIMIT),
        name="inproj",
    )(x, sh1, sc1, wfox, wgla, wsm, fb, tri)


def _fox_kernel(q_ref, k_ref, vt_ref, o_ref, m_ref, acc_ref, sa_ref, sb_ref):
    qi = pl.program_id(2)
    tq = q_ref.shape[2]
    m_ref[...] = jnp.full(m_ref.shape, -jnp.inf, F32)
    acc_ref[...] = jnp.zeros_like(acc_ref)

    def scores(j, s_ref):
        k0 = pl.multiple_of(j * tq, tq)
        for hh in range(2):
            s_ref[hh] = _dot_nt(k_ref[0, hh, pl.ds(k0, tq), :], q_ref[0, hh])

    def update(j, s_ref, masked):
        for hh in range(2):
            s_t = s_ref[hh]
            if masked:
                key = lax.broadcasted_iota(jnp.int32, (tq, tq), 0)
                qry = lax.broadcasted_iota(jnp.int32, (tq, tq), 1)
                s_t = jnp.where(key <= qry, s_t, -jnp.inf)
            m_old = m_ref[hh]
            m_new = jnp.maximum(m_old, jnp.max(s_t, axis=0, keepdims=True))
            p_t = jnp.exp2(s_t - m_new).astype(BF16)
            acc_ref[hh] = (jnp.exp2(m_old - m_new) * acc_ref[hh]
                           + _dot(vt_ref[0, hh, j], p_t))
            m_ref[hh] = m_new

    scores(0, sa_ref)

    def body(t, carry):
        j = 2 * t
        scores(j + 1, sb_ref)
        update(j, sa_ref, False)
        scores(j + 2, sa_ref)
        update(j + 1, sb_ref, False)
        return carry

    lax.fori_loop(0, qi // 2, body, 0)

    @pl.when(qi % 2 == 0)
    def _():
        update(qi, sa_ref, True)

    @pl.when(qi % 2 == 1)
    def _():
        scores(qi, sb_ref)
        update(qi - 1, sa_ref, False)
        update(qi, sb_ref, True)

    outs = []
    for hh in range(2):
        acc = acc_ref[hh]
        outs.append((acc / acc[FOX_DH:FOX_DH + 1, :])[:FOX_DH])
    o_ref[0] = jnp.concatenate(outs, axis=0).T.astype(BF16)


def _fox(qa, ka, vt):
    B, H, S, _ = qa.shape
    tq = TQ
    return pl.pallas_call(
        _fox_kernel,
        grid=(B, H // 2, S // tq),
        in_specs=[pl.BlockSpec((1, 2, tq, LANES), lambda b, h, q: (b, h, q, 0)),
                  pl.BlockSpec((1, 2, S, LANES), lambda b, h, q: (b, h, 0, 0)),
                  pl.BlockSpec((1, 2, S // tq, LANES, tq), lambda b, h, q: (b, h, 0, 0, 0))],
        out_specs=pl.BlockSpec((1, tq, LANES), lambda b, h, q: (b, q, h)),
        out_shape=jax.ShapeDtypeStruct((B, S, FOX_W), BF16),
        scratch_shapes=[pltpu.VMEM((2, 1, tq), F32), pltpu.VMEM((2, LANES, tq), F32),
                        pltpu.VMEM((2, tq, tq), F32), pltpu.VMEM((2, tq, tq), F32)],
        compiler_params=pltpu.CompilerParams(
            dimension_semantics=("parallel", "parallel", "arbitrary"),
            vmem_limit_bytes=VMEM_LIMIT),
        name="fox",
    )(qa, ka, vt)


def _gla_tables():
    C = GLA_CHUNK
    t = np.arange(C)[:, None]
    j = np.arange(C)[None, :]
    slabs = [(j <= t), (j > t)]
    masks = [np.eye(C, dtype=bool)]
    m = C // 2
    while m >= 1:
        g0 = (t // (2 * m)) * (2 * m)
        piv = g0 + m - 1
        upper = (t - g0) >= m
        slabs.append(np.where(upper, (j > piv) & (j <= t), (j > t) & (j <= piv)))
        s = np.arange(C)[None, :]
        masks.append(upper & ((s // (2 * m)) == (t // (2 * m))) & ((s % (2 * m)) < m))
        m //= 2
    cm = np.concatenate(slabs, axis=0).astype(np.float32)
    cm3 = np.concatenate([cm, cm, cm], axis=1)
    return cm3, np.stack(masks).astype(np.float32)


def _gla_kernel(gq_ref, gk_ref, gv_ref, gr_ref, glr_ref, wg_ref, bg_ref, ng_ref,
                cm_ref, mask_ref, o_ref, st_ref, la3_ref):
    C = GLA_CHUNK
    L = gq_ref.shape[1]
    n_lvl = mask_ref.shape[0] - 1

    @pl.when(pl.program_id(1) == 0)
    def _():
        st_ref[...] = jnp.zeros_like(st_ref)

    xg = jnp.dot(glr_ref[0], wg_ref[...], precision=HIGHEST,
                 preferred_element_type=F32) + bg_ref[...]
    la = _log_sigmoid(xg) * (1.0 / GLA_TAU)
    hi, mid, lo = _split3(la)
    for c in range(L // C):
        la3_ref[3 * C * c:3 * C * c + C, :] = hi[C * c:C * c + C]
        la3_ref[3 * C * c + C:3 * C * c + 2 * C, :] = mid[C * c:C * c + C]
        la3_ref[3 * C * c + 2 * C:3 * C * c + 3 * C, :] = lo[C * c:C * c + C]

    def chunk_group(gi, carry):
        pairs = [(u, h) for u in range(GLA_UNROLL) for h in range(GLA_H)]
        r0 = [pl.multiple_of((gi * GLA_UNROLL + u) * C, C) for u in range(GLA_UNROLL)]
        ks = [slice(GLA_DK * h, GLA_DK * h + GLA_DK) for h in range(GLA_H)]
        vs = [slice(GLA_DV * h, GLA_DV * h + GLA_DV) for h in range(GLA_H)]

        w, q, k = [], [], []
        for u in range(GLA_UNROLL):
            a0 = pl.multiple_of((gi * GLA_UNROLL + u) * 3 * C, 3 * C)
            w.append(jnp.exp(_dot(cm_ref[...], la3_ref[pl.ds(a0, 3 * C), :])))
            q.append(gq_ref[0, pl.ds(r0[u], C), :].astype(F32))
            k.append(gk_ref[0, pl.ds(r0[u], C), :].astype(F32))

        q_in, k_out, dec, q_lv, k_lv, v = {}, {}, {}, {}, {}, {}
        for u, h in pairs:
            wh, qh, kh = w[u][:, ks[h]], q[u][:, ks[h]], k[u][:, ks[h]]
            q_in[u, h] = (qh * wh[0:C]).astype(BF16)
            k_out[u, h] = (kh * wh[C:2 * C]).astype(BF16)
            dec[u, h] = wh[C - 1:C, :]
            q_lv[u, h] = [qh.astype(BF16)] + [(qh * wh[(2 + lv) * C:(3 + lv) * C]).astype(BF16)
                                              for lv in range(n_lvl)]
            k_lv[u, h] = [kh.astype(BF16)] + [(kh * wh[(2 + lv) * C:(3 + lv) * C]).astype(BF16)
                                              for lv in range(n_lvl)]
            v[u, h] = gv_ref[0, pl.ds(r0[u], C), vs[h]]

        upd = {p: _dot_tn(v[p], k_out[p]) for p in pairs}
        sc_parts = {p: [_dot_nt(a, b) for a, b in zip(q_lv[p], k_lv[p])] for p in pairs}

        inter = {}
        for h in range(GLA_H):
            st = st_ref[h]
            for u in range(GLA_UNROLL):
                inter[u, h] = _dot_nt(q_in[u, h], st.astype(BF16))
                st = st * dec[u, h] + upd[u, h]
            st_ref[h] = st

        sc = {}
        for p in pairs:
            acc = mask_ref[0] * sc_parts[p][0]
            for lv in range(n_lvl):
                acc = acc + mask_ref[1 + lv] * sc_parts[p][1 + lv]
            sc[p] = acc.astype(BF16)
        intra = {p: _dot(sc[p], v[p]) for p in pairs}
        for u, h in pairs:
            o = inter[u, h] + intra[u, h]
            y = o * lax.rsqrt(jnp.mean(o * o, axis=1, keepdims=True) + EPS)
            g = gr_ref[0, pl.ds(r0[u], C), vs[h]].astype(F32)
            o_ref[0, pl.ds(r0[u], C), vs[h]] = (
                y * ng_ref[:, vs[h]] * (g * _sigmoid(g))).astype(BF16)
        return carry

    lax.fori_loop(0, L // (C * GLA_UNROLL), chunk_group, 0)


def _gla(gq, gk, gv, gr, glr, wg, bg, ng, cm3, masks):
    B, S, _ = gq.shape
    L = GLA_BLOCK
    tok = lambda b, s: (b, s, 0)
    c2 = lambda b, s: (0, 0)
    return pl.pallas_call(
        _gla_kernel,
        grid=(B, S // L),
        in_specs=[pl.BlockSpec((1, L, GLA_KW), tok),
                  pl.BlockSpec((1, L, GLA_KW), tok),
                  pl.BlockSpec((1, L, GLA_VW), tok),
                  pl.BlockSpec((1, L, GLA_VW), tok),
                  pl.BlockSpec((1, L, GLA_RANK), tok),
                  pl.BlockSpec(wg.shape, c2),
                  pl.BlockSpec(bg.shape, c2),
                  pl.BlockSpec(ng.shape, c2),
                  pl.BlockSpec(cm3.shape, c2),
                  pl.BlockSpec(masks.shape, lambda b, s: (0, 0, 0))],
        out_specs=pl.BlockSpec((1, L, GLA_VW), tok),
        out_shape=jax.ShapeDtypeStruct((B, S, GLA_VW), BF16),
        scratch_shapes=[pltpu.VMEM((GLA_H, GLA_DV, GLA_DK), F32),
                        pltpu.VMEM((3 * L, GLA_KW), BF16)],
        compiler_params=pltpu.CompilerParams(
            dimension_semantics=("parallel", "arbitrary"), vmem_limit_bytes=VMEM_LIMIT),
        name="gla",
    )(gq, gk, gv, gr, glr, wg, bg, ng, cm3, masks)


def _merge_kernel(x_ref, ya_ref, yb_ref, mod_ref, wgate_ref, wa_ref, wb_ref, wo_ref,
                  lng_ref, lnb_ref, wr_ref, br_ref,
                  x1_ref, u2_ref, topv_ref, topi_ref):
    tm = x_ref.shape[1]
    x = x_ref[0]
    mod = mod_ref[0]
    sh1, sc1, g1 = mod[0:1], mod[1:2], mod[2:3]
    sh2, sc2 = mod[3:4], mod[4:5]
    ub = (_ln(x) * (1.0 + sc1) + sh1).astype(BF16)
    br_a = _dot(ya_ref[0], wa_ref[...])
    br_b = _dot(yb_ref[0], wb_ref[...])
    merged = (_sigmoid(_dot(ub, wgate_ref[:, :D])) * br_a
              + _sigmoid(_dot(ub, wgate_ref[:, D:])) * br_b)
    mix = _dot(merged.astype(BF16), wo_ref[...])
    x1 = _ln(ALPHA * x + (1.0 + g1) * mix) * lng_ref[...] + lnb_ref[...]
    x1_ref[0] = x1
    u2 = _ln(x1) * (1.0 + sc2) + sh2
    _store_token_tiles(u2_ref, u2)

    u_hi = u2.astype(BF16)
    u_lo = (u2 - u_hi.astype(F32)).astype(BF16)
    logits = (_dot(u_hi, wr_ref[0]) + _dot(u_lo, wr_ref[0]) + _dot(u_hi, wr_ref[1])
              + br_ref[...])
    lane = lax.broadcasted_iota(jnp.int32, (tm, LANES), 1)
    vals = jnp.zeros((tm, LANES), F32)
    idxs = jnp.zeros((tm, LANES), jnp.int32)
    cur = logits
    for k in range(TOP_K):
        mx = jnp.max(cur, axis=1, keepdims=True)
        ix = jnp.min(jnp.where(cur == mx, lane, LANES), axis=1, keepdims=True)
        vals = jnp.where(lane == k, mx, vals)
        idxs = jnp.where(lane == k, ix, idxs)
        cur = jnp.where(lane == ix, -jnp.inf, cur)
    v0 = jnp.max(jnp.where(lane < TOP_K, vals, -jnp.inf), axis=1, keepdims=True)
    e = jnp.where(lane < TOP_K, jnp.exp(vals - v0), 0.0)
    topv_ref[0] = e / jnp.sum(e, axis=1, keepdims=True)
    topi_ref[0] = idxs


def _merge(x, ya, yb, mod, wgate, wa, wb, wo, lng, lnb, wr, br):
    B, S, _ = x.shape
    tm = TM_MERGE
    tok = lambda b, s: (b, s, 0)
    c2 = lambda b, s: (0, 0)
    nst = S // tm
    return pl.pallas_call(
        _merge_kernel,
        grid=(B, nst),
        in_specs=[pl.BlockSpec((1, tm, D), tok),
                  pl.BlockSpec((1, tm, FOX_W), tok),
                  pl.BlockSpec((1, tm, GLA_VW), tok),
                  pl.BlockSpec((1, 8, D), lambda b, s: (b, 0, 0)),
                  pl.BlockSpec(wgate.shape, c2, pipeline_mode=pl.Buffered(1)),
                  pl.BlockSpec(wa.shape, c2, pipeline_mode=pl.Buffered(1)),
                  pl.BlockSpec(wb.shape, c2, pipeline_mode=pl.Buffered(1)),
                  pl.BlockSpec(wo.shape, c2, pipeline_mode=pl.Buffered(1)),
                  pl.BlockSpec((1, D), c2),
                  pl.BlockSpec((1, D), c2),
                  pl.BlockSpec(wr.shape, lambda b, s: (0, 0, 0)),
                  pl.BlockSpec((1, LANES), c2)],
        out_specs=[pl.BlockSpec((1, tm, D), tok),
                   pl.BlockSpec((tm * ROW_TILE, LANES), lambda b, s: (b * nst + s, 0)),
                   pl.BlockSpec((1, tm, LANES), tok),
                   pl.BlockSpec((1, tm, LANES), tok)],
        out_shape=[jax.ShapeDtypeStruct((B, S, D), F32),
                   jax.ShapeDtypeStruct((B * S * ROW_TILE, LANES), F32),
                   jax.ShapeDtypeStruct((B, S, LANES), F32),
                   jax.ShapeDtypeStruct((B, S, LANES), jnp.int32)],
        compiler_params=pltpu.CompilerParams(
            dimension_semantics=("parallel", "arbitrary"), vmem_limit_bytes=VMEM_LIMIT),
        name="merge",
    )(x, ya, yb, mod, wgate, wa, wb, wo, lng, lnb, wr, br)


GATHER_UNROLL = 8


def _store_token_tiles(ref, x):
    n = x.shape[0]
    for c in range(ROW_TILE):
        ref[pl.ds(c, n, stride=ROW_TILE), :] = x[:, LANES * c:LANES * c + LANES]


def _token_tile_cols(ref, row0, n, c):
    return ref[pl.ds(row0 * ROW_TILE + c, n, stride=ROW_TILE), :]


def _row_copy(src_hbm, row, buf, r, sem):
    src = src_hbm.at[pl.ds(pl.multiple_of(row * ROW_TILE, ROW_TILE), ROW_TILE), :]
    return pltpu.make_async_copy(src, buf.at[pl.ds(r * ROW_TILE, ROW_TILE), :], sem)


def _issue_rows(idx_ref, src_hbm, buf, sem):
    def issue(g, carry):
        for u in range(GATHER_UNROLL):
            r = g * GATHER_UNROLL + u
            src = src_hbm.at[pl.ds(pl.multiple_of(idx_ref[0, 0, r] * ROW_TILE, ROW_TILE),
                                   ROW_TILE), :]
            dst = buf.at[pl.ds(pl.multiple_of(r * ROW_TILE, ROW_TILE), ROW_TILE), :]
            pltpu.make_async_copy(src, dst, sem).start()
        return carry
    lax.fori_loop(0, buf.shape[0] // (ROW_TILE * GATHER_UNROLL), issue, 0)


def _wait_rows(src_hbm, buf, sem):
    pltpu.make_async_copy(src_hbm.at[pl.ds(0, buf.shape[0]), :], buf, sem).wait()


def _gather_step(step, n_steps, idx_ref, idx_next_ref, src_hbm, bufs, sems, compute):
    @pl.when(step == 0)
    def _():
        _issue_rows(idx_ref, src_hbm, bufs[0], sems.at[0])

    for par in range(2):
        @pl.when((step < n_steps) & (step % 2 == par))
        def _():
            nxt, cur = bufs[1 - par], bufs[par]
            for r in range(nxt.shape[0] // ROW_TILE):
                _row_copy(src_hbm, idx_next_ref[0, 0, r], nxt, r,
                          sems.at[1 - par]).start(priority=r % 2)
            _wait_rows(src_hbm, cur, sems.at[par])
            compute(cur)

            @pl.when(step == n_steps - 1)
            def _():
                _wait_rows(src_hbm, nxt, sems.at[1 - par])


def _moe_kernel(be_ref, nu_ref, tok_ref, tok_next_ref, u2_hbm, wup_ref, bup_ref, wdn_ref,
                bdn_ref, o_ref, buf0, buf1, wup_bf, wdn_bf, sems):
    i = pl.program_id(0)
    n_used = nu_ref[0]
    nb = o_ref.shape[0] // ROW_TILE

    @pl.when((i < n_used) & ((i == 0) | (be_ref[i] != be_ref[jnp.maximum(i - 1, 0)])))
    def _():
        wup_bf[...] = wup_ref[0].astype(BF16)
        wdn_bf[...] = wdn_ref[0].astype(BF16)

    def compute(buf):
        xb = jnp.concatenate([_token_tile_cols(buf, 0, nb, c).astype(BF16)
                              for c in range(ROW_TILE)], axis=1)
        h = _dot(xb, wup_bf[...]) + bup_ref[0]
        h_glu = jnp.minimum(h[:, :D], SWIGLU_LIMIT)
        h_lin = jnp.clip(h[:, D:], -SWIGLU_LIMIT, SWIGLU_LIMIT)
        act = h_glu * _sigmoid(SWIGLU_ALPHA * h_glu) * (h_lin + 1.0)
        _store_token_tiles(o_ref, _dot(act.astype(BF16), wdn_bf[...]) + bdn_ref[0])

    _gather_step(i, n_used, tok_ref, tok_next_ref, u2_hbm, (buf0, buf1), sems, compute)

    @pl.when(i >= n_used)
    def _():
        o_ref[...] = jnp.zeros_like(o_ref)


def _moe(block_expert, n_used, row_tok, u2, wup, bup, wdn, bdn):
    nblk = block_expert.shape[0]
    nb = MOE_BLOCK
    ex = lambda i, be, nu: (be[i], 0, 0)
    grid_spec = pltpu.PrefetchScalarGridSpec(
        num_scalar_prefetch=2,
        grid=(nblk,),
        in_specs=[pl.BlockSpec((1, 1, nb), lambda i, be, nu: (i, 0, 0),
                               memory_space=pltpu.SMEM),
                  pl.BlockSpec((1, 1, nb),
                               lambda i, be, nu: (jnp.maximum(jnp.minimum(i + 1, nu[0] - 1), 0), 0, 0),
                               memory_space=pltpu.SMEM),
                  pl.BlockSpec(memory_space=pl.ANY),
                  pl.BlockSpec((1, D, 2 * D), ex),
                  pl.BlockSpec((1, 1, 2 * D), ex),
                  pl.BlockSpec((1, D, D), ex),
                  pl.BlockSpec((1, 1, D), ex)],
        out_specs=pl.BlockSpec((nb * ROW_TILE, LANES), lambda i, be, nu: (i, 0)),
        scratch_shapes=[pltpu.VMEM((nb * ROW_TILE, LANES), F32),
                        pltpu.VMEM((nb * ROW_TILE, LANES), F32),
                        pltpu.VMEM((D, 2 * D), BF16), pltpu.VMEM((D, D), BF16),
                        pltpu.SemaphoreType.DMA((2,))],
    )
    return pl.pallas_call(
        _moe_kernel,
        grid_spec=grid_spec,
        out_shape=jax.ShapeDtypeStruct((nblk * nb * ROW_TILE, LANES), F32),
        compiler_params=pltpu.CompilerParams(
            dimension_semantics=("arbitrary",), vmem_limit_bytes=VMEM_LIMIT_MOE),
        name="moe",
    )(block_expert, n_used, row_tok, row_tok, u2, wup, bup, wdn, bdn)


def _final_kernel(dest_ref, dest_next_ref, rows_hbm, x1_ref, gate_ref, g2_ref, lng_ref,
                  lnb_ref, o_ref, buf0, buf1, sems):
    tm = x1_ref.shape[0]

    def compute(buf):
        lane = lax.broadcasted_iota(jnp.int32, (tm, LANES), 1)
        gates = gate_ref[...]
        g = [_lane_col(gates, j, lane) for j in range(TOP_K)]
        cols = []
        for c in range(ROW_TILE):
            acc = g[0] * _token_tile_cols(buf, 0, tm, c)
            for j in range(1, TOP_K):
                acc = acc + g[j] * _token_tile_cols(buf, tm * j, tm, c)
            cols.append(acc)
        ffn = jnp.concatenate(cols, axis=1)
        z = ALPHA * x1_ref[...] + (1.0 + g2_ref[0]) * ffn
        o_ref[...] = _ln(z) * lng_ref[...] + lnb_ref[...]

    _gather_step(pl.program_id(0), pl.num_programs(0), dest_ref, dest_next_ref,
                 rows_hbm, (buf0, buf1), sems, compute)


def _final(dest, rows, x1, gates, g2, lng, lnb, tiles_per_seq):
    T = x1.shape[0]
    tm = TM_FINAL
    nt = T // tm
    tok = lambda i: (i, 0)
    c2 = lambda i: (0, 0)
    n = TOP_K * tm
    return pl.pallas_call(
        _final_kernel,
        grid=(nt,),
        in_specs=[pl.BlockSpec((1, 1, n), lambda i: (i, 0, 0), memory_space=pltpu.SMEM),
                  pl.BlockSpec((1, 1, n), lambda i: (jnp.minimum(i + 1, nt - 1), 0, 0),
                               memory_space=pltpu.SMEM),
                  pl.BlockSpec(memory_space=pl.ANY),
                  pl.BlockSpec((tm, D), tok),
                  pl.BlockSpec((tm, LANES), tok),
                  pl.BlockSpec((1, 1, D), lambda i: (i // tiles_per_seq, 0, 0)),
                  pl.BlockSpec((1, D), c2),
                  pl.BlockSpec((1, D), c2)],
        out_specs=pl.BlockSpec((tm, D), tok),
        out_shape=jax.ShapeDtypeStruct((T, D), F32),
        scratch_shapes=[pltpu.VMEM((n * ROW_TILE, LANES), F32),
                        pltpu.VMEM((n * ROW_TILE, LANES), F32),
                        pltpu.SemaphoreType.DMA((2,))],
        compiler_params=pltpu.CompilerParams(
            dimension_semantics=("arbitrary",), vmem_limit_bytes=VMEM_LIMIT),
        name="final",
    )(dest, dest, rows, x1, gates, g2, lng, lnb)


def _routing(top_idx):
    T = top_idx.shape[0]
    A = T * TOP_K
    nb = MOE_BLOCK
    nblk = A // nb + N_EXP
    e_flat = top_idx.reshape(A)
    onehot = (e_flat[:, None] == jnp.arange(N_EXP, dtype=jnp.int32)[None, :]).astype(jnp.int32)
    csum = jnp.cumsum(onehot, axis=0)
    counts = csum[-1]
    rank = jnp.sum(csum * onehot, axis=1) - 1
    padded = (counts + nb - 1) // nb * nb
    padded_end = jnp.cumsum(padded)
    padded_start = padded_end - padded
    dest = padded_start[e_flat] + rank
    blk_row0 = jnp.arange(nblk, dtype=jnp.int32) * nb
    block_expert = jnp.minimum(
        jnp.sum((padded_end[None, :] <= blk_row0[:, None]).astype(jnp.int32), axis=1),
        N_EXP - 1)
    order = jnp.argsort(e_flat, stable=True).astype(jnp.int32)
    starts = jnp.cumsum(counts) - counts
    local = (blk_row0 - padded_start[block_expert])[:, None] + jnp.arange(nb, dtype=jnp.int32)
    valid = local < counts[block_expert][:, None]
    src = jnp.clip(starts[block_expert][:, None] + local, 0, A - 1)
    row_tok = jnp.where(valid, order[src] // TOP_K, 0)
    n_used = (padded_end[-1] // nb).astype(jnp.int32).reshape(1)
    last_e = block_expert[jnp.maximum(n_used[0] - 1, 0)]
    block_expert = jnp.where(jnp.arange(nblk) < n_used[0], block_expert, last_e)
    return (block_expert, n_used, row_tok.astype(jnp.int32).reshape(nblk, 1, nb),
            dest.reshape(T, TOP_K))


def kernel(x, c, w_ada, b_ada, w_in, fox_f_bias, w_gla_gate, b_gla_gate, gla_norm_g,
           w_branch_a, w_branch_b, w_out, ln1_g, ln1_b, w_router, b_router,
           w_up, b_up, w_down, b_down, ln2_g, ln2_b):
    B, S, _ = x.shape
    T = B * S
    l = 0

    c_pad = jnp.zeros((8, D), F32).at[:B].set(c)
    mod = _ada(c_pad, w_ada[l], b_ada[l][None, :])[:B]
    mod6 = mod.reshape(B, 6, D)
    mod8 = jnp.concatenate([mod6, jnp.zeros((B, 2, D), F32)], axis=1)
    sh1, sc1 = mod6[:, 0:1], mod6[:, 1:2]
    g2 = mod6[:, 5:6]

    w = w_in[l]
    o = 0
    parts = []
    for width in (FOX_W, FOX_W, FOX_W, FOX_H, GLA_KW, GLA_KW, GLA_VW, GLA_VW, GLA_RANK, D, D):
        parts.append(w[:, o:o + width])
        o += width
    wq, wk, wv, wff, wgq, wgk, wgv, wgr, wglr, wga, wgb = parts

    def head_pad(m):
        m = m.reshape(D, FOX_H, FOX_DH)
        return jnp.concatenate([m, jnp.zeros_like(m)], axis=2).reshape(D, FOX_H * LANES)

    wfox = jnp.concatenate([head_pad(wq * (FOX_DH ** -0.5 * LOG2E)), head_pad(wk), head_pad(wv)],
                           axis=1).astype(BF16)
    wgla = jnp.concatenate([wgq * GLA_DK ** -0.5, wgk, wgv, wgr], axis=1).astype(BF16)
    wsm = jnp.zeros((D, 2 * LANES), F32).at[:, :FOX_H].set(wff)
    wsm = wsm.at[:, LANES:LANES + GLA_RANK].set(wglr).astype(BF16)
    fb = jnp.zeros((1, LANES), F32).at[0, :FOX_H].set(fox_f_bias[l])
    tri = jnp.asarray(np.tril(np.ones((TM_IN, TM_IN), np.float32)), dtype=BF16)

    qa, ka, va, gq, gk, gv, gr, glr = _inproj(x, sh1, sc1, wfox, wgla, wsm, fb, tri)

    ya = _fox(qa, ka, va)

    cm3, masks = _gla_tables()
    yb = _gla(gq, gk, gv, gr, glr, w_gla_gate[l], b_gla_gate[l][None, :],
              gla_norm_g[l][None, :], jnp.asarray(cm3, dtype=BF16), jnp.asarray(masks))

    wgate = jnp.concatenate([wga, wgb], axis=1).astype(BF16)
    wr = jnp.zeros((D, LANES), F32).at[:, :N_EXP].set(w_router[l])
    wr_hi = wr.astype(BF16)
    wr = jnp.stack([wr_hi, (wr - wr_hi.astype(F32)).astype(BF16)])
    br = jnp.full((1, LANES), NEG, F32).at[0, :N_EXP].set(b_router[l])
    x1, u2, topv, topi = _merge(
        x, ya, yb, mod8, wgate, w_branch_a[l].astype(BF16), w_branch_b[l].astype(BF16),
        w_out[l].astype(BF16), ln1_g[l][None, :], ln1_b[l][None, :], wr, br)

    block_expert, n_used, row_tok, dest = _routing(topi.reshape(T, LANES)[:, :TOP_K])
    rows = _moe(block_expert, n_used, row_tok, u2,
                w_up[l], b_up[l][:, None, :], w_down[l], b_down[l][:, None, :])

    nt = T // TM_FINAL
    dest_t = dest.reshape(nt, TM_FINAL, TOP_K).transpose(0, 2, 1).reshape(nt, 1, TOP_K * TM_FINAL)
    out = _final(dest_t, rows, x1.reshape(T, D), topv.reshape(T, LANES), g2,
                 ln2_g[l][None, :], ln2_b[l][None, :], S // TM_FINAL)
    return out.reshape(B, S, D)
```

```python
import functools

import numpy as np
import jax
import jax.numpy as jnp
from jax import lax
from jax.experimental import pallas as pl
from jax.experimental.pallas import tpu as pltpu

F32 = jnp.float32
BF16 = jnp.bfloat16
HIGHEST = lax.Precision.HIGHEST

D = 1024
FOX_H = 8
FOX_DH = 64
FOX_W = FOX_H * FOX_DH
GLA_H = 4
GLA_DK = 128
GLA_DV = 256
GLA_KW = GLA_H * GLA_DK
GLA_VW = GLA_H * GLA_DV
GLA_RANK = 16
GLA_TAU = 16.0
N_EXP = 32
TOP_K = 4
SWIGLU_LIMIT = 7.0
SWIGLU_ALPHA = 1.702
EPS = 1e-5
DEPTH = 1
ALPHA = (2 * DEPTH) ** 0.25
LANES = 128
ROW_TILE = 8

GLA_CHUNK = 64
GLA_BLOCK = 512
GLA_UNROLL = 2
MOE_BLOCK = 512
TM_IN = 256
TM_MERGE = 512
TM_FINAL = 256
TQ = 512
VMEM_LIMIT = 56 * 1024 * 1024
VMEM_LIMIT_MOE = 60 * 1024 * 1024

NEG = -1e30
LOG2E = 1.4426950408889634


def _ln(x):
    mu = jnp.mean(x, axis=-1, keepdims=True)
    xc = x - mu
    var = jnp.mean(xc * xc, axis=-1, keepdims=True)
    return xc * lax.rsqrt(var + EPS)


def _sigmoid(x):
    return 1.0 / (1.0 + jnp.exp(-x))


def _log_sigmoid(x):
    return jnp.minimum(x, 0.0) - jnp.log(1.0 + jnp.exp(-jnp.abs(x)))


def _split3(x):
    hi = x.astype(BF16)
    r = x - hi.astype(F32)
    mid = r.astype(BF16)
    lo = (r - mid.astype(F32)).astype(BF16)
    return hi, mid, lo


def _lane_col(x, idx, lane):
    return jnp.sum(jnp.where(lane == idx, x, 0.0), axis=1, keepdims=True)


def _dot(a, b):
    return jnp.dot(a, b, preferred_element_type=F32)


def _dot_nt(a, b):
    return lax.dot_general(a, b, (((1,), (1,)), ((), ())), preferred_element_type=F32)


def _dot_tn(a, b):
    return lax.dot_general(a, b, (((0,), (0,)), ((), ())), preferred_element_type=F32)


def _ada_kernel(c_ref, w_ref, b_ref, o_ref):
    c = c_ref[...]
    ca = c * _sigmoid(c)
    o_ref[...] = jnp.dot(ca, w_ref[...], precision=HIGHEST,
                         preferred_element_type=F32) + b_ref[...]


def _ada(c_pad, w, b):
    n = w.shape[1]
    tn = 1536
    return pl.pallas_call(
        _ada_kernel,
        grid=(n // tn,),
        in_specs=[pl.BlockSpec((8, D), lambda j: (0, 0)),
                  pl.BlockSpec((D, tn), lambda j: (0, j)),
                  pl.BlockSpec((1, tn), lambda j: (0, j))],
        out_specs=pl.BlockSpec((8, tn), lambda j: (0, j)),
        out_shape=jax.ShapeDtypeStruct((8, n), F32),
        compiler_params=pltpu.CompilerParams(
            dimension_semantics=("arbitrary",), vmem_limit_bytes=VMEM_LIMIT),
        name="ada",
    )(c_pad, w, b)


def _inproj_kernel(x_ref, sh_ref, sc_ref, wfox_ref, wgla_ref, wsm_ref, fb_ref, tri_ref,
                   qa_ref, ka_ref, vt_ref, gq_ref, gk_ref, gv_ref, gr_ref, glr_ref,
                   carry_ref):
    tm = x_ref.shape[1]

    @pl.when(pl.program_id(1) == 0)
    def _():
        carry_ref[...] = jnp.zeros_like(carry_ref)

    u = _ln(x_ref[0]) * (1.0 + sc_ref[0]) + sh_ref[0]
    ub = u.astype(BF16)

    sm = _dot(ub, wsm_ref[...])
    glr_ref[0] = sm[:, LANES:LANES + GLA_RANK]
    lane = lax.broadcasted_iota(jnp.int32, (tm, LANES), 1)
    lf = jnp.where(lane < FOX_H, _log_sigmoid(sm[:, :LANES] + fb_ref[...]), 0.0)
    hi, mid, lo = _split3(lf)
    tri = tri_ref[...]
    cum = _dot(tri, hi) + _dot(tri, mid) + _dot(tri, lo) + carry_ref[...]
    carry_ref[...] = cum[tm - 1:tm, :]
    chi, cmid, clo = _split3(cum * LOG2E)
    chi, cmid, clo = chi.astype(F32), cmid.astype(F32), clo.astype(F32)

    ex_v = jnp.where(lane == FOX_DH, 1.0, 0.0)
    is_q1 = (lane >= FOX_DH + 3) & (lane < FOX_DH + 6)
    is_k1 = (lane >= FOX_DH) & (lane < FOX_DH + 3)
    low = lane < FOX_DH
    q_all = _dot(ub, wfox_ref[:, 0:FOX_W])
    k_all = _dot(ub, wfox_ref[:, FOX_W:2 * FOX_W])
    v_all = _dot(ub, wfox_ref[:, 2 * FOX_W:3 * FOX_W])
    for hp in range(FOX_H // 2):
        pair = slice(LANES * hp, LANES * hp + LANES)
        for hh in range(2):
            h = 2 * hp + hh
            c0 = _lane_col(chi, h, lane)
            c1 = _lane_col(cmid, h, lane)
            c2 = _lane_col(clo, h, lane)
            ex_q = jnp.where(lane == FOX_DH, c0,
                             jnp.where(lane == FOX_DH + 1, c1,
                                       jnp.where(lane == FOX_DH + 2, c2,
                                                 jnp.where(is_q1, 1.0, 0.0))))
            ex_k = jnp.where(lane == FOX_DH + 3, -c0,
                             jnp.where(lane == FOX_DH + 4, -c1,
                                       jnp.where(lane == FOX_DH + 5, -c2,
                                                 jnp.where(is_k1, 1.0, 0.0))))
            qh, kh, vh = q_all[:, pair], k_all[:, pair], v_all[:, pair]
            if hh == 1:
                qh, kh, vh = (pltpu.roll(t, FOX_DH, 1) for t in (qh, kh, vh))
            qa_ref[0, h] = jnp.where(low, qh, ex_q).astype(BF16)
            ka_ref[0, h] = jnp.where(low, kh, ex_k).astype(BF16)
            vt_ref[0, h, 0] = jnp.where(low, vh, ex_v).T.astype(BF16)

    for j in range(GLA_KW // 256):
        gq_ref[0, :, 256 * j:256 * j + 256] = _dot(
            ub, wgla_ref[:, 256 * j:256 * j + 256]).astype(BF16)
        gk_ref[0, :, 256 * j:256 * j + 256] = _dot(
            ub, wgla_ref[:, GLA_KW + 256 * j:GLA_KW + 256 * j + 256]).astype(BF16)
    for j in range(GLA_VW // 256):
        o = 2 * GLA_KW + 256 * j
        gv_ref[0, :, 256 * j:256 * j + 256] = _dot(ub, wgla_ref[:, o:o + 256]).astype(BF16)
        o = 2 * GLA_KW + GLA_VW + 256 * j
        gr_ref[0, :, 256 * j:256 * j + 256] = _dot(ub, wgla_ref[:, o:o + 256]).astype(BF16)


def _inproj(x, sh1, sc1, wfox, wgla, wsm, fb, tri):
    B, S, _ = x.shape
    tm = TM_IN
    const = lambda b, s: (0, 0)
    tok = lambda b, s: (b, s, 0)
    head = lambda b, s: (b, 0, s, 0)
    vec = lambda b, s: (b, 0, 0)
    hs = jax.ShapeDtypeStruct((B, FOX_H, S, LANES), BF16)
    per_q = TQ // tm
    return pl.pallas_call(
        _inproj_kernel,
        grid=(B, S // tm),
        in_specs=[pl.BlockSpec((1, tm, D), tok),
                  pl.BlockSpec((1, 1, D), vec),
                  pl.BlockSpec((1, 1, D), vec),
                  pl.BlockSpec(wfox.shape, const),
                  pl.BlockSpec(wgla.shape, const),
                  pl.BlockSpec(wsm.shape, const),
                  pl.BlockSpec((1, LANES), const),
                  pl.BlockSpec((tm, tm), const)],
        out_specs=[pl.BlockSpec((1, FOX_H, tm, LANES), head),
                   pl.BlockSpec((1, FOX_H, tm, LANES), head),
                   pl.BlockSpec((1, FOX_H, 1, LANES, tm),
                                lambda b, s: (b, 0, s // per_q, 0, s % per_q)),
                   pl.BlockSpec((1, tm, GLA_KW), tok),
                   pl.BlockSpec((1, tm, GLA_KW), tok),
                   pl.BlockSpec((1, tm, GLA_VW), tok),
                   pl.BlockSpec((1, tm, GLA_VW), tok),
                   pl.BlockSpec((1, tm, GLA_RANK), tok)],
        out_shape=[hs, hs, jax.ShapeDtypeStruct((B, FOX_H, S // TQ, LANES, TQ), BF16),
                   jax.ShapeDtypeStruct((B, S, GLA_KW), BF16),
                   jax.ShapeDtypeStruct((B, S, GLA_KW), BF16),
                   jax.ShapeDtypeStruct((B, S, GLA_VW), BF16),
                   jax.ShapeDtypeStruct((B, S, GLA_VW), BF16),
                   jax.ShapeDtypeStruct((B, S, GLA_RANK), F32)],
        scratch_shapes=[pltpu.VMEM((1, LANES), F32)],
        compiler_params=pltpu.CompilerParams(
            dimension_semantics=("parallel", "arbitrary"), vmem_limit_bytes=VMEM_LIMIT),
        name="inproj",
    )(x, sh1, sc1, wfox, wgla, wsm, fb, tri)


def _fox_kernel(q_ref, k_ref, vt_ref, o_ref, m_ref, acc_ref, sa_ref, sb_ref):
    qi = pl.program_id(2)
    tq = q_ref.shape[2]
    m_ref[...] = jnp.full(m_ref.shape, -jnp.inf, F32)
    acc_ref[...] = jnp.zeros_like(acc_ref)

    def scores(j, s_ref):
        k0 = pl.multiple_of(j * tq, tq)
        for hh in range(2):
            s_ref[hh] = _dot_nt(k_ref[0, hh, pl.ds(k0, tq), :], q_ref[0, hh])

    def update(j, s_ref, masked):
        for hh in range(2):
            s_t = s_ref[hh]
            if masked:
                key = lax.broadcasted_iota(jnp.int32, (tq, tq), 0)
                qry = lax.broadcasted_iota(jnp.int32, (tq, tq), 1)
                s_t = jnp.where(key <= qry, s_t, -jnp.inf)
            m_old = m_ref[hh]
            m_new = jnp.maximum(m_old, jnp.max(s_t, axis=0, keepdims=True))
            p_t = jnp.exp2(s_t - m_new).astype(BF16)
            acc_ref[hh] = (jnp.exp2(m_old - m_new) * acc_ref[hh]
                           + _dot(vt_ref[0, hh, j], p_t))
            m_ref[hh] = m_new

    scores(0, sa_ref)

    def body(t, carry):
        j = 2 * t
        scores(j + 1, sb_ref)
        update(j, sa_ref, False)
        scores(j + 2, sa_ref)
        update(j + 1, sb_ref, False)
        return carry

    lax.fori_loop(0, qi // 2, body, 0)

    @pl.when(qi % 2 == 0)
    def _():
        update(qi, sa_ref, True)

    @pl.when(qi % 2 == 1)
    def _():
        scores(qi, sb_ref)
        update(qi - 1, sa_ref, False)
        update(qi, sb_ref, True)

    outs = []
    for hh in range(2):
        acc = acc_ref[hh]
        outs.append((acc / acc[FOX_DH:FOX_DH + 1, :])[:FOX_DH])
    o_ref[0] = jnp.concatenate(outs, axis=0).T.astype(BF16)


def _fox(qa, ka, vt):
    B, H, S, _ = qa.shape
    tq = TQ
    return pl.pallas_call(
        _fox_kernel,
        grid=(B, H // 2, S // tq),
        in_specs=[pl.BlockSpec((1, 2, tq, LANES), lambda b, h, q: (b, h, q, 0)),
                  pl.BlockSpec((1, 2, S, LANES), lambda b, h, q: (b, h, 0, 0)),
                  pl.BlockSpec((1, 2, S // tq, LANES, tq), lambda b, h, q: (b, h, 0, 0, 0))],
        out_specs=pl.BlockSpec((1, tq, LANES), lambda b, h, q: (b, q, h)),
        out_shape=jax.ShapeDtypeStruct((B, S, FOX_W), BF16),
        scratch_shapes=[pltpu.VMEM((2, 1, tq), F32), pltpu.VMEM((2, LANES, tq), F32),
                        pltpu.VMEM((2, tq, tq), F32), pltpu.VMEM((2, tq, tq), F32)],
        compiler_params=pltpu.CompilerParams(
            dimension_semantics=("parallel", "parallel", "arbitrary"),
            vmem_limit_bytes=VMEM_LIMIT),
        name="fox",
    )(qa, ka, vt)


def _gla_tables():
    C = GLA_CHUNK
    t = np.arange(C)[:, None]
    j = np.arange(C)[None, :]
    slabs = [(j <= t), (j > t)]
    masks = [np.eye(C, dtype=bool)]
    m = C // 2
    while m >= 1:
        g0 = (t // (2 * m)) * (2 * m)
        piv = g0 + m - 1
        upper = (t - g0) >= m
        slabs.append(np.where(upper, (j > piv) & (j <= t), (j > t) & (j <= piv)))
        s = np.arange(C)[None, :]
        masks.append(upper & ((s // (2 * m)) == (t // (2 * m))) & ((s % (2 * m)) < m))
        m //= 2
    cm = np.concatenate(slabs, axis=0).astype(np.float32)
    cm3 = np.concatenate([cm, cm, cm], axis=1)
    return cm3, np.stack(masks).astype(np.float32)


def _gla_kernel(gq_ref, gk_ref, gv_ref, gr_ref, glr_ref, wg_ref, bg_ref, ng_ref,
                cm_ref, mask_ref, o_ref, st_ref, la3_ref):
    C = GLA_CHUNK
    L = gq_ref.shape[1]
    n_lvl = mask_ref.shape[0] - 1

    @pl.when(pl.program_id(1) == 0)
    def _():
        st_ref[...] = jnp.zeros_like(st_ref)

    xg = jnp.dot(glr_ref[0], wg_ref[...], precision=HIGHEST,
                 preferred_element_type=F32) + bg_ref[...]
    la = _log_sigmoid(xg) * (1.0 / GLA_TAU)
    hi, mid, lo = _split3(la)
    for c in range(L // C):
        la3_ref[3 * C * c:3 * C * c + C, :] = hi[C * c:C * c + C]
        la3_ref[3 * C * c + C:3 * C * c + 2 * C, :] = mid[C * c:C * c + C]
        la3_ref[3 * C * c + 2 * C:3 * C * c + 3 * C, :] = lo[C * c:C * c + C]

    def chunk_group(gi, carry):
        pairs = [(u, h) for u in range(GLA_UNROLL) for h in range(GLA_H)]
        r0 = [pl.multiple_of((gi * GLA_UNROLL + u) * C, C) for u in range(GLA_UNROLL)]
        ks = [slice(GLA_DK * h, GLA_DK * h + GLA_DK) for h in range(GLA_H)]
        vs = [slice(GLA_DV * h, GLA_DV * h + GLA_DV) for h in range(GLA_H)]

        w, q, k = [], [], []
        for u in range(GLA_UNROLL):
            a0 = pl.multiple_of((gi * GLA_UNROLL + u) * 3 * C, 3 * C)
            w.append(jnp.exp(_dot(cm_ref[...], la3_ref[pl.ds(a0, 3 * C), :])))
            q.append(gq_ref[0, pl.ds(r0[u], C), :].astype(F32))
            k.append(gk_ref[0, pl.ds(r0[u], C), :].astype(F32))

        q_in, k_out, dec, q_lv, k_lv, v = {}, {}, {}, {}, {}, {}
        for u, h in pairs:
            wh, qh, kh = w[u][:, ks[h]], q[u][:, ks[h]], k[u][:, ks[h]]
            q_in[u, h] = (qh * wh[0:C]).astype(BF16)
            k_out[u, h] = (kh * wh[C:2 * C]).astype(BF16)
            dec[u, h] = wh[C - 1:C, :]
            q_lv[u, h] = [qh.astype(BF16)] + [(qh * wh[(2 + lv) * C:(3 + lv) * C]).astype(BF16)
                                              for lv in range(n_lvl)]
            k_lv[u, h] = [kh.astype(BF16)] + [(kh * wh[(2 + lv) * C:(3 + lv) * C]).astype(BF16)
                                              for lv in range(n_lvl)]
            v[u, h] = gv_ref[0, pl.ds(r0[u], C), vs[h]]

        upd = {p: _dot_tn(v[p], k_out[p]) for p in pairs}
        sc_parts = {p: [_dot_nt(a, b) for a, b in zip(q_lv[p], k_lv[p])] for p in pairs}

        inter = {}
        for h in range(GLA_H):
            st = st_ref[h]
            for u in range(GLA_UNROLL):
                inter[u, h] = _dot_nt(q_in[u, h], st.astype(BF16))
                st = st * dec[u, h] + upd[u, h]
            st_ref[h] = st

        sc = {}
        for p in pairs:
            acc = mask_ref[0] * sc_parts[p][0]
            for lv in range(n_lvl):
                acc = acc + mask_ref[1 + lv] * sc_parts[p][1 + lv]
            sc[p] = acc.astype(BF16)
        intra = {p: _dot(sc[p], v[p]) for p in pairs}
        for u, h in pairs:
            o = inter[u, h] + intra[u, h]
            y = o * lax.rsqrt(jnp.mean(o * o, axis=1, keepdims=True) + EPS)
            g = gr_ref[0, pl.ds(r0[u], C), vs[h]].astype(F32)
            o_ref[0, pl.ds(r0[u], C), vs[h]] = (
                y * ng_ref[:, vs[h]] * (g * _sigmoid(g))).astype(BF16)
        return carry

    lax.fori_loop(0, L // (C * GLA_UNROLL), chunk_group, 0)


def _gla(gq, gk, gv, gr, glr, wg, bg, ng, cm3, masks):
    B, S, _ = gq.shape
    L = GLA_BLOCK
    tok = lambda b, s: (b, s, 0)
    c2 = lambda b, s: (0, 0)
    return pl.pallas_call(
        _gla_kernel,
        grid=(B, S // L),
        in_specs=[pl.BlockSpec((1, L, GLA_KW), tok),
                  pl.BlockSpec((1, L, GLA_KW), tok),
                  pl.BlockSpec((1, L, GLA_VW), tok),
                  pl.BlockSpec((1, L, GLA_VW), tok),
                  pl.BlockSpec((1, L, GLA_RANK), tok),
                  pl.BlockSpec(wg.shape, c2),
                  pl.BlockSpec(bg.shape, c2),
                  pl.BlockSpec(ng.shape, c2),
                  pl.BlockSpec(cm3.shape, c2),
                  pl.BlockSpec(masks.shape, lambda b, s: (0, 0, 0))],
        out_specs=pl.BlockSpec((1, L, GLA_VW), tok),
        out_shape=jax.ShapeDtypeStruct((B, S, GLA_VW), BF16),
        scratch_shapes=[pltpu.VMEM((GLA_H, GLA_DV, GLA_DK), F32),
                        pltpu.VMEM((3 * L, GLA_KW), BF16)],
        compiler_params=pltpu.CompilerParams(
            dimension_semantics=("parallel", "arbitrary"), vmem_limit_bytes=VMEM_LIMIT),
        name="gla",
    )(gq, gk, gv, gr, glr, wg, bg, ng, cm3, masks)


def _merge_kernel(x_ref, ya_ref, yb_ref, mod_ref, wgate_ref, wa_ref, wb_ref, wo_ref,
                  lng_ref, lnb_ref, wr_ref, br_ref, tri_ref,
                  x1_ref, u2_ref, topv_ref, topi_ref, rank_ref, cnt_ref, carry_ref):
    tm = x_ref.shape[1]
    x = x_ref[0]
    mod = mod_ref[0]
    sh1, sc1, g1 = mod[0:1], mod[1:2], mod[2:3]
    sh2, sc2 = mod[3:4], mod[4:5]
    ub = (_ln(x) * (1.0 + sc1) + sh1).astype(BF16)
    br_a = _dot(ya_ref[0], wa_ref[...])
    br_b = _dot(yb_ref[0], wb_ref[...])
    merged = (_sigmoid(_dot(ub, wgate_ref[:, :D])) * br_a
              + _sigmoid(_dot(ub, wgate_ref[:, D:])) * br_b)
    mix = _dot(merged.astype(BF16), wo_ref[...])
    x1 = _ln(ALPHA * x + (1.0 + g1) * mix) * lng_ref[...] + lnb_ref[...]
    x1_ref[0] = x1
    u2 = _ln(x1) * (1.0 + sc2) + sh2
    _store_token_tiles(u2_ref, u2)

    u_hi = u2.astype(BF16)
    u_lo = (u2 - u_hi.astype(F32)).astype(BF16)
    logits = (_dot(u_hi, wr_ref[0]) + _dot(u_lo, wr_ref[0]) + _dot(u_hi, wr_ref[1])
              + br_ref[...])
    lane = lax.broadcasted_iota(jnp.int32, (tm, LANES), 1)
    vals = jnp.zeros((tm, LANES), F32)
    idxs = jnp.zeros((tm, LANES), jnp.int32)
    picked = jnp.zeros((tm, LANES), F32)
    cur = logits
    ixs = []
    for k in range(TOP_K):
        mx = jnp.max(cur, axis=1, keepdims=True)
        ix = jnp.min(jnp.where(cur == mx, lane, LANES), axis=1, keepdims=True)
        vals = jnp.where(lane == k, mx, vals)
        idxs = jnp.where(lane == k, ix, idxs)
        picked = jnp.where(lane == ix, 1.0, picked)
        cur = jnp.where(lane == ix, -jnp.inf, cur)
        ixs.append(ix)
    v0 = jnp.max(jnp.where(lane < TOP_K, vals, -jnp.inf), axis=1, keepdims=True)
    e = jnp.where(lane < TOP_K, jnp.exp(vals - v0), 0.0)
    topv_ref[0] = e / jnp.sum(e, axis=1, keepdims=True)
    topi_ref[0] = idxs

    @pl.when((pl.program_id(0) == 0) & (pl.program_id(1) == 0))
    def _():
        carry_ref[...] = jnp.zeros_like(carry_ref)

    before = _dot(tri_ref[...], picked.astype(BF16)) + carry_ref[...]
    ranks = jnp.zeros((tm, LANES), F32)
    for k in range(TOP_K):
        rk = jnp.sum(jnp.where(lane == ixs[k], before, 0.0), axis=1, keepdims=True)
        ranks = jnp.where(lane == k, rk, ranks)
    rank_ref[0] = ranks.astype(jnp.int32)
    total = carry_ref[...] + jnp.sum(picked, axis=0, keepdims=True)
    carry_ref[...] = total
    cnt_ref[...] = total


def _merge(x, ya, yb, mod, wgate, wa, wb, wo, lng, lnb, wr, br):
    B, S, _ = x.shape
    tm = TM_MERGE
    tri = jnp.asarray(np.tril(np.ones((tm, tm), np.float32), -1), dtype=BF16)
    tok = lambda b, s: (b, s, 0)
    c2 = lambda b, s: (0, 0)
    nst = S // tm
    return pl.pallas_call(
        _merge_kernel,
        grid=(B, nst),
        in_specs=[pl.BlockSpec((1, tm, D), tok),
                  pl.BlockSpec((1, tm, FOX_W), tok),
                  pl.BlockSpec((1, tm, GLA_VW), tok),
                  pl.BlockSpec((1, 8, D), lambda b, s: (b, 0, 0)),
                  pl.BlockSpec(wgate.shape, c2, pipeline_mode=pl.Buffered(1)),
                  pl.BlockSpec(wa.shape, c2, pipeline_mode=pl.Buffered(1)),
                  pl.BlockSpec(wb.shape, c2, pipeline_mode=pl.Buffered(1)),
                  pl.BlockSpec(wo.shape, c2, pipeline_mode=pl.Buffered(1)),
                  pl.BlockSpec((1, D), c2),
                  pl.BlockSpec((1, D), c2),
                  pl.BlockSpec(wr.shape, lambda b, s: (0, 0, 0)),
                  pl.BlockSpec((1, LANES), c2),
                  pl.BlockSpec((tm, tm), c2)],
        out_specs=[pl.BlockSpec((1, tm, D), tok),
                   pl.BlockSpec((tm * ROW_TILE, LANES), lambda b, s: (b * nst + s, 0)),
                   pl.BlockSpec((1, tm, LANES), tok),
                   pl.BlockSpec((1, tm, LANES), tok),
                   pl.BlockSpec((1, tm, LANES), tok),
                   pl.BlockSpec((1, LANES), c2)],
        out_shape=[jax.ShapeDtypeStruct((B, S, D), F32),
                   jax.ShapeDtypeStruct((B * S * ROW_TILE, LANES), F32),
                   jax.ShapeDtypeStruct((B, S, LANES), F32),
                   jax.ShapeDtypeStruct((B, S, LANES), jnp.int32),
                   jax.ShapeDtypeStruct((B, S, LANES), jnp.int32),
                   jax.ShapeDtypeStruct((1, LANES), F32)],
        scratch_shapes=[pltpu.VMEM((1, LANES), F32)],
        compiler_params=pltpu.CompilerParams(
            dimension_semantics=("arbitrary", "arbitrary"), vmem_limit_bytes=VMEM_LIMIT),
        name="merge",
    )(x, ya, yb, mod, wgate, wa, wb, wo, lng, lnb, wr, br, tri)


GATHER_UNROLL = 8


def _store_token_tiles(ref, x):
    n = x.shape[0]
    for c in range(ROW_TILE):
        ref[pl.ds(c, n, stride=ROW_TILE), :] = x[:, LANES * c:LANES * c + LANES]


def _token_tile_cols(ref, row0, n, c):
    return ref[pl.ds(row0 * ROW_TILE + c, n, stride=ROW_TILE), :]


def _row_copy(src_hbm, row, buf, r, sem):
    src = src_hbm.at[pl.ds(pl.multiple_of(row * ROW_TILE, ROW_TILE), ROW_TILE), :]
    return pltpu.make_async_copy(src, buf.at[pl.ds(r * ROW_TILE, ROW_TILE), :], sem)


def _issue_rows(idx_ref, src_hbm, buf, sem):
    def issue(g, carry):
        for u in range(GATHER_UNROLL):
            r = g * GATHER_UNROLL + u
            src = src_hbm.at[pl.ds(pl.multiple_of(idx_ref[0, 0, r] * ROW_TILE, ROW_TILE),
                                   ROW_TILE), :]
            dst = buf.at[pl.ds(pl.multiple_of(r * ROW_TILE, ROW_TILE), ROW_TILE), :]
            pltpu.make_async_copy(src, dst, sem).start()
        return carry
    lax.fori_loop(0, buf.shape[0] // (ROW_TILE * GATHER_UNROLL), issue, 0)


def _wait_rows(src_hbm, buf, sem):
    pltpu.make_async_copy(src_hbm.at[pl.ds(0, buf.shape[0]), :], buf, sem).wait()


def _gather_step(step, n_steps, idx_ref, idx_next_ref, src_hbm, bufs, sems, compute):
    @pl.when(step == 0)
    def _():
        _issue_rows(idx_ref, src_hbm, bufs[0], sems.at[0])

    for par in range(2):
        @pl.when((step < n_steps) & (step % 2 == par))
        def _():
            nxt, cur = bufs[1 - par], bufs[par]
            _wait_rows(src_hbm, cur, sems.at[par])
            for r in range(nxt.shape[0] // ROW_TILE):
                _row_copy(src_hbm, idx_next_ref[0, 0, r], nxt, r,
                          sems.at[1 - par]).start(priority=r % 2)
            compute(cur)

            @pl.when(step == n_steps - 1)
            def _():
                _wait_rows(src_hbm, nxt, sems.at[1 - par])


def _moe_kernel(be_ref, nu_ref, tok_ref, tok_next_ref, u2_hbm, wup_ref, bup_ref, wdn_ref,
                bdn_ref, o_ref, buf0, buf1, wup_bf, wdn_bf, sems):
    i = pl.program_id(0)
    n_used = nu_ref[0]
    nb = o_ref.shape[0] // ROW_TILE

    @pl.when((i < n_used) & ((i == 0) | (be_ref[i] != be_ref[jnp.maximum(i - 1, 0)])))
    def _():
        wup_bf[...] = wup_ref[0].astype(BF16)
        wdn_bf[...] = wdn_ref[0].astype(BF16)

    def compute(buf):
        xb = jnp.concatenate([_token_tile_cols(buf, 0, nb, c).astype(BF16)
                              for c in range(ROW_TILE)], axis=1)
        h = _dot(xb, wup_bf[...]) + bup_ref[0]
        h_glu = jnp.minimum(h[:, :D], SWIGLU_LIMIT)
        h_lin = jnp.clip(h[:, D:], -SWIGLU_LIMIT, SWIGLU_LIMIT)
        act = h_glu * _sigmoid(SWIGLU_ALPHA * h_glu) * (h_lin + 1.0)
        _store_token_tiles(o_ref, _dot(act.astype(BF16), wdn_bf[...]) + bdn_ref[0])

    _gather_step(i, n_used, tok_ref, tok_next_ref, u2_hbm, (buf0, buf1), sems, compute)

    @pl.when(i >= n_used)
    def _():
        o_ref[...] = jnp.zeros_like(o_ref)


def _moe(block_expert, n_used, row_tok, u2, wup, bup, wdn, bdn):
    nblk = block_expert.shape[0]
    nb = MOE_BLOCK
    ex = lambda i, be, nu: (be[i], 0, 0)
    grid_spec = pltpu.PrefetchScalarGridSpec(
        num_scalar_prefetch=2,
        grid=(nblk,),
        in_specs=[pl.BlockSpec((1, 1, nb), lambda i, be, nu: (i, 0, 0),
                               memory_space=pltpu.SMEM),
                  pl.BlockSpec((1, 1, nb),
                               lambda i, be, nu: (jnp.maximum(jnp.minimum(i + 1, nu[0] - 1), 0), 0, 0),
                               memory_space=pltpu.SMEM),
                  pl.BlockSpec(memory_space=pl.ANY),
                  pl.BlockSpec((1, D, 2 * D), ex),
                  pl.BlockSpec((1, 1, 2 * D), ex),
                  pl.BlockSpec((1, D, D), ex),
                  pl.BlockSpec((1, 1, D), ex)],
        out_specs=pl.BlockSpec((nb * ROW_TILE, LANES), lambda i, be, nu: (i, 0)),
        scratch_shapes=[pltpu.VMEM((nb * ROW_TILE, LANES), F32),
                        pltpu.VMEM((nb * ROW_TILE, LANES), F32),
                        pltpu.VMEM((D, 2 * D), BF16), pltpu.VMEM((D, D), BF16),
                        pltpu.SemaphoreType.DMA((2,))],
    )
    return pl.pallas_call(
        _moe_kernel,
        grid_spec=grid_spec,
        out_shape=jax.ShapeDtypeStruct((nblk * nb * ROW_TILE, LANES), F32),
        compiler_params=pltpu.CompilerParams(
            dimension_semantics=("arbitrary",), vmem_limit_bytes=VMEM_LIMIT_MOE),
        name="moe",
    )(block_expert, n_used, row_tok, row_tok, u2, wup, bup, wdn, bdn)


def _final_kernel(dest_ref, dest_next_ref, rows_hbm, x1_ref, gate_ref, g2_ref, lng_ref,
                  lnb_ref, o_ref, buf0, buf1, sems):
    tm = x1_ref.shape[0]

    def compute(buf):
        lane = lax.broadcasted_iota(jnp.int32, (tm, LANES), 1)
        gates = gate_ref[...]
        g = [_lane_col(gates, j, lane) for j in range(TOP_K)]
        cols = []
        for c in range(ROW_TILE):
            acc = g[0] * _token_tile_cols(buf, 0, tm, c)
            for j in range(1, TOP_K):
                acc = acc + g[j] * _token_tile_cols(buf, tm * j, tm, c)
            cols.append(acc)
        ffn = jnp.concatenate(cols, axis=1)
        z = ALPHA * x1_ref[...] + (1.0 + g2_ref[0]) * ffn
        o_ref[...] = _ln(z) * lng_ref[...] + lnb_ref[...]

    _gather_step(pl.program_id(0), pl.num_programs(0), dest_ref, dest_next_ref,
                 rows_hbm, (buf0, buf1), sems, compute)


def _final(dest, rows, x1, gates, g2, lng, lnb, tiles_per_seq):
    T = x1.shape[0]
    tm = TM_FINAL
    nt = T // tm
    tok = lambda i: (i, 0)
    c2 = lambda i: (0, 0)
    n = TOP_K * tm
    return pl.pallas_call(
        _final_kernel,
        grid=(nt,),
        in_specs=[pl.BlockSpec((1, 1, n), lambda i: (i, 0, 0), memory_space=pltpu.SMEM),
                  pl.BlockSpec((1, 1, n), lambda i: (jnp.minimum(i + 1, nt - 1), 0, 0),
                               memory_space=pltpu.SMEM),
                  pl.BlockSpec(memory_space=pl.ANY),
                  pl.BlockSpec((tm, D), tok),
                  pl.BlockSpec((tm, LANES), tok),
                  pl.BlockSpec((1, 1, D), lambda i: (i // tiles_per_seq, 0, 0)),
                  pl.BlockSpec((1, D), c2),
                  pl.BlockSpec((1, D), c2)],
        out_specs=pl.BlockSpec((tm, D), tok),
        out_shape=jax.ShapeDtypeStruct((T, D), F32),
        scratch_shapes=[pltpu.VMEM((n * ROW_TILE, LANES), F32),
                        pltpu.VMEM((n * ROW_TILE, LANES), F32),
                        pltpu.SemaphoreType.DMA((2,))],
        compiler_params=pltpu.CompilerParams(
            dimension_semantics=("arbitrary",), vmem_limit_bytes=VMEM_LIMIT),
        name="final",
    )(dest, dest, rows, x1, gates, g2, lng, lnb)


def _routing(top_idx, rank, counts):
    T = top_idx.shape[0]
    A = T * TOP_K
    nb = MOE_BLOCK
    nblk = A // nb + N_EXP
    e_flat = top_idx.reshape(A)
    rank = rank.reshape(A)
    padded = (counts + nb - 1) // nb * nb
    padded_end = jnp.cumsum(padded)
    padded_start = padded_end - padded
    dest = padded_start[e_flat] + rank
    blk_row0 = jnp.arange(nblk, dtype=jnp.int32) * nb
    block_expert = jnp.minimum(
        jnp.sum((padded_end[None, :] <= blk_row0[:, None]).astype(jnp.int32), axis=1),
        N_EXP - 1)
    order = jnp.argsort(e_flat, stable=True).astype(jnp.int32)
    starts = jnp.cumsum(counts) - counts
    local = (blk_row0 - padded_start[block_expert])[:, None] + jnp.arange(nb, dtype=jnp.int32)
    valid = local < counts[block_expert][:, None]
    src = jnp.clip(starts[block_expert][:, None] + local, 0, A - 1)
    row_tok = jnp.where(valid, order[src] // TOP_K, 0)
    n_used = (padded_end[-1] // nb).astype(jnp.int32).reshape(1)
    last_e = block_expert[jnp.maximum(n_used[0] - 1, 0)]
    block_expert = jnp.where(jnp.arange(nblk) < n_used[0], block_expert, last_e)
    return (block_expert, n_used, row_tok.astype(jnp.int32).reshape(nblk, 1, nb),
            dest.reshape(T, TOP_K))


def kernel(x, c, w_ada, b_ada, w_in, fox_f_bias, w_gla_gate, b_gla_gate, gla_norm_g,
           w_branch_a, w_branch_b, w_out, ln1_g, ln1_b, w_router, b_router,
           w_up, b_up, w_down, b_down, ln2_g, ln2_b):
    B, S, _ = x.shape
    T = B * S
    l = 0

    c_pad = jnp.zeros((8, D), F32).at[:B].set(c)
    mod = _ada(c_pad, w_ada[l], b_ada[l][None, :])[:B]
    mod6 = mod.reshape(B, 6, D)
    mod8 = jnp.concatenate([mod6, jnp.zeros((B, 2, D), F32)], axis=1)
    sh1, sc1 = mod6[:, 0:1], mod6[:, 1:2]
    g2 = mod6[:, 5:6]

    w = w_in[l]
    o = 0
    parts = []
    for width in (FOX_W, FOX_W, FOX_W, FOX_H, GLA_KW, GLA_KW, GLA_VW, GLA_VW, GLA_RANK, D, D):
        parts.append(w[:, o:o + width])
        o += width
    wq, wk, wv, wff, wgq, wgk, wgv, wgr, wglr, wga, wgb = parts

    wfox = jnp.concatenate([wq * (FOX_DH ** -0.5 * LOG2E), wk, wv], axis=1).astype(BF16)
    wgla = jnp.concatenate([wgq * GLA_DK ** -0.5, wgk, wgv, wgr], axis=1).astype(BF16)
    wsm = jnp.zeros((D, 2 * LANES), F32).at[:, :FOX_H].set(wff)
    wsm = wsm.at[:, LANES:LANES + GLA_RANK].set(wglr).astype(BF16)
    fb = jnp.zeros((1, LANES), F32).at[0, :FOX_H].set(fox_f_bias[l])
    tri = jnp.asarray(np.tril(np.ones((TM_IN, TM_IN), np.float32)), dtype=BF16)

    qa, ka, va, gq, gk, gv, gr, glr = _inproj(x, sh1, sc1, wfox, wgla, wsm, fb, tri)

    ya = _fox(qa, ka, va)

    cm3, masks = _gla_tables()
    yb = _gla(gq, gk, gv, gr, glr, w_gla_gate[l], b_gla_gate[l][None, :],
              gla_norm_g[l][None, :], jnp.asarray(cm3, dtype=BF16), jnp.asarray(masks))

    wgate = jnp.concatenate([wga, wgb], axis=1).astype(BF16)
    wr = jnp.zeros((D, LANES), F32).at[:, :N_EXP].set(w_router[l])
    wr_hi = wr.astype(BF16)
    wr = jnp.stack([wr_hi, (wr - wr_hi.astype(F32)).astype(BF16)])
    br = jnp.full((1, LANES), NEG, F32).at[0, :N_EXP].set(b_router[l])
    x1, u2, topv, topi, rank, counts = _merge(
        x, ya, yb, mod8, wgate, w_branch_a[l].astype(BF16), w_branch_b[l].astype(BF16),
        w_out[l].astype(BF16), ln1_g[l][None, :], ln1_b[l][None, :], wr, br)

    block_expert, n_used, row_tok, dest = _routing(
        topi.reshape(T, LANES)[:, :TOP_K], rank.reshape(T, LANES)[:, :TOP_K],
        counts[0, :N_EXP].astype(jnp.int32))
    rows = _moe(block_expert, n_used, row_tok, u2,
                w_up[l], b_up[l][:, None, :], w_down[l], b_down[l][:, None, :])

    nt = T // TM_FINAL
    dest_t = dest.reshape(nt, TM_FINAL, TOP_K).transpose(0, 2, 1).reshape(nt, 1, TOP_K * TM_FINAL)
    out = _final(dest_t, rows, x1.reshape(T, D), topv.reshape(T, LANES), g2,
                 ln2_g[l][None, :], ln2_b[l][None, :], S // TM_FINAL)
    return out.reshape(B, S, D)
```

```python
import functools

import numpy as np
import jax
import jax.numpy as jnp
from jax import lax
from jax.experimental import pallas as pl
from jax.experimental.pallas import tpu as pltpu

F32 = jnp.float32
BF16 = jnp.bfloat16
HIGHEST = lax.Precision.HIGHEST

D = 1024
FOX_H = 8
FOX_DH = 64
FOX_W = FOX_H * FOX_DH
GLA_H = 4
GLA_DK = 128
GLA_DV = 256
GLA_KW = GLA_H * GLA_DK
GLA_VW = GLA_H * GLA_DV
GLA_RANK = 16
GLA_TAU = 16.0
N_EXP = 32
TOP_K = 4
SWIGLU_LIMIT = 7.0
SWIGLU_ALPHA = 1.702
EPS = 1e-5
DEPTH = 1
ALPHA = (2 * DEPTH) ** 0.25
LANES = 128
ROW_TILE = 8
ROUTE_LANES = 8

GLA_CHUNK = 64
GLA_BLOCK = 512
GLA_UNROLL = 2
MOE_BLOCK = 512
TM_IN = 256
TM_MERGE = 512
TM_FINAL = 256
TQ = 512
VMEM_LIMIT = 56 * 1024 * 1024
VMEM_LIMIT_MOE = 60 * 1024 * 1024

NEG = -1e30
LOG2E = 1.4426950408889634


def _ln(x):
    mu = jnp.mean(x, axis=-1, keepdims=True)
    xc = x - mu
    var = jnp.mean(xc * xc, axis=-1, keepdims=True)
    return xc * lax.rsqrt(var + EPS)


def _sigmoid(x):
    return 1.0 / (1.0 + jnp.exp(-x))


def _log_sigmoid(x):
    return jnp.minimum(x, 0.0) - jnp.log(1.0 + jnp.exp(-jnp.abs(x)))


def _split3(x):
    hi = x.astype(BF16)
    r = x - hi.astype(F32)
    mid = r.astype(BF16)
    lo = (r - mid.astype(F32)).astype(BF16)
    return hi, mid, lo


def _lane_col(x, idx, lane):
    return jnp.sum(jnp.where(lane == idx, x, 0.0), axis=1, keepdims=True)


def _dot(a, b):
    return jnp.dot(a, b, preferred_element_type=F32)


def _dot_nt(a, b):
    return lax.dot_general(a, b, (((1,), (1,)), ((), ())), preferred_element_type=F32)


def _dot_tn(a, b):
    return lax.dot_general(a, b, (((0,), (0,)), ((), ())), preferred_element_type=F32)


def _ada_kernel(c_ref, w_ref, b_ref, o_ref):
    c = c_ref[...]
    ca = c * _sigmoid(c)
    o_ref[...] = jnp.dot(ca, w_ref[...], precision=HIGHEST,
                         preferred_element_type=F32) + b_ref[...]


def _ada(c_pad, w, b):
    n = w.shape[1]
    tn = 1536
    return pl.pallas_call(
        _ada_kernel,
        grid=(n // tn,),
        in_specs=[pl.BlockSpec((8, D), lambda j: (0, 0)),
                  pl.BlockSpec((D, tn), lambda j: (0, j)),
                  pl.BlockSpec((1, tn), lambda j: (0, j))],
        out_specs=pl.BlockSpec((8, tn), lambda j: (0, j)),
        out_shape=jax.ShapeDtypeStruct((8, n), F32),
        compiler_params=pltpu.CompilerParams(
            dimension_semantics=("arbitrary",), vmem_limit_bytes=VMEM_LIMIT),
        name="ada",
    )(c_pad, w, b)


def _inproj_kernel(x_ref, sh_ref, sc_ref, wfox_ref, wgla_ref, wsm_ref, fb_ref, tri_ref,
                   qa_ref, ka_ref, vt_ref, gq_ref, gk_ref, gv_ref, gr_ref, glr_ref,
                   carry_ref):
    tm = x_ref.shape[1]

    @pl.when(pl.program_id(1) == 0)
    def _():
        carry_ref[...] = jnp.zeros_like(carry_ref)

    u = _ln(x_ref[0]) * (1.0 + sc_ref[0]) + sh_ref[0]
    ub = u.astype(BF16)

    sm = _dot(ub, wsm_ref[...])
    glr_ref[0] = sm[:, LANES:LANES + GLA_RANK]
    lane = lax.broadcasted_iota(jnp.int32, (tm, LANES), 1)
    lf = jnp.where(lane < FOX_H, _log_sigmoid(sm[:, :LANES] + fb_ref[...]), 0.0)
    hi, mid, lo = _split3(lf)
    tri = tri_ref[...]
    cum = _dot(tri, hi) + _dot(tri, mid) + _dot(tri, lo) + carry_ref[...]
    carry_ref[...] = cum[tm - 1:tm, :]
    chi, cmid, clo = _split3(cum * LOG2E)
    chi, cmid, clo = chi.astype(F32), cmid.astype(F32), clo.astype(F32)

    ex_v = jnp.where(lane == FOX_DH, 1.0, 0.0)
    is_q1 = (lane >= FOX_DH + 3) & (lane < FOX_DH + 6)
    is_k1 = (lane >= FOX_DH) & (lane < FOX_DH + 3)
    low = lane < FOX_DH
    q_all = _dot(ub, wfox_ref[:, 0:FOX_W])
    k_all = _dot(ub, wfox_ref[:, FOX_W:2 * FOX_W])
    v_all = _dot(ub, wfox_ref[:, 2 * FOX_W:3 * FOX_W])
    for hp in range(FOX_H // 2):
        pair = slice(LANES * hp, LANES * hp + LANES)
        for hh in range(2):
            h = 2 * hp + hh
            c0 = _lane_col(chi, h, lane)
            c1 = _lane_col(cmid, h, lane)
            c2 = _lane_col(clo, h, lane)
            ex_q = jnp.where(lane == FOX_DH, c0,
                             jnp.where(lane == FOX_DH + 1, c1,
                                       jnp.where(lane == FOX_DH + 2, c2,
                                                 jnp.where(is_q1, 1.0, 0.0))))
            ex_k = jnp.where(lane == FOX_DH + 3, -c0,
                             jnp.where(lane == FOX_DH + 4, -c1,
                                       jnp.where(lane == FOX_DH + 5, -c2,
                                                 jnp.where(is_k1, 1.0, 0.0))))
            qh, kh, vh = q_all[:, pair], k_all[:, pair], v_all[:, pair]
            if hh == 1:
                qh, kh, vh = (pltpu.roll(t, FOX_DH, 1) for t in (qh, kh, vh))
            qa_ref[0, h] = jnp.where(low, qh, ex_q).astype(BF16)
            ka_ref[0, h] = jnp.where(low, kh, ex_k).astype(BF16)
            vt_ref[0, h, 0] = jnp.where(low, vh, ex_v).T.astype(BF16)

    for j in range(GLA_KW // 256):
        gq_ref[0, :, 256 * j:256 * j + 256] = _dot(
            ub, wgla_ref[:, 256 * j:256 * j + 256]).astype(BF16)
        gk_ref[0, :, 256 * j:256 * j + 256] = _dot(
            ub, wgla_ref[:, GLA_KW + 256 * j:GLA_KW + 256 * j + 256]).astype(BF16)
    for j in range(GLA_VW // 256):
        o = 2 * GLA_KW + 256 * j
        gv_ref[0, :, 256 * j:256 * j + 256] = _dot(ub, wgla_ref[:, o:o + 256]).astype(BF16)
        o = 2 * GLA_KW + GLA_VW + 256 * j
        gr_ref[0, :, 256 * j:256 * j + 256] = _dot(ub, wgla_ref[:, o:o + 256]).astype(BF16)


def _inproj(x, sh1, sc1, wfox, wgla, wsm, fb, tri):
    B, S, _ = x.shape
    tm = TM_IN
    const = lambda b, s: (0, 0)
    tok = lambda b, s: (b, s, 0)
    head = lambda b, s: (b, 0, s, 0)
    vec = lambda b, s: (b, 0, 0)
    hs = jax.ShapeDtypeStruct((B, FOX_H, S, LANES), BF16)
    per_q = TQ // tm
    return pl.pallas_call(
        _inproj_kernel,
        grid=(B, S // tm),
        in_specs=[pl.BlockSpec((1, tm, D), tok),
                  pl.BlockSpec((1, 1, D), vec),
                  pl.BlockSpec((1, 1, D), vec),
                  pl.BlockSpec(wfox.shape, const),
                  pl.BlockSpec(wgla.shape, const),
                  pl.BlockSpec(wsm.shape, const),
                  pl.BlockSpec((1, LANES), const),
                  pl.BlockSpec((tm, tm), const)],
        out_specs=[pl.BlockSpec((1, FOX_H, tm, LANES), head),
                   pl.BlockSpec((1, FOX_H, tm, LANES), head),
                   pl.BlockSpec((1, FOX_H, 1, LANES, tm),
                                lambda b, s: (b, 0, s // per_q, 0, s % per_q)),
                   pl.BlockSpec((1, tm, GLA_KW), tok),
                   pl.BlockSpec((1, tm, GLA_KW), tok),
                   pl.BlockSpec((1, tm, GLA_VW), tok),
                   pl.BlockSpec((1, tm, GLA_VW), tok),
                   pl.BlockSpec((1, tm, GLA_RANK), tok)],
        out_shape=[hs, hs, jax.ShapeDtypeStruct((B, FOX_H, S // TQ, LANES, TQ), BF16),
                   jax.ShapeDtypeStruct((B, S, GLA_KW), BF16),
                   jax.ShapeDtypeStruct((B, S, GLA_KW), BF16),
                   jax.ShapeDtypeStruct((B, S, GLA_VW), BF16),
                   jax.ShapeDtypeStruct((B, S, GLA_VW), BF16),
                   jax.ShapeDtypeStruct((B, S, GLA_RANK), F32)],
        scratch_shapes=[pltpu.VMEM((1, LANES), F32)],
        compiler_params=pltpu.CompilerParams(
            dimension_semantics=("parallel", "arbitrary"), vmem_limit_bytes=VMEM_LIMIT),
        name="inproj",
    )(x, sh1, sc1, wfox, wgla, wsm, fb, tri)


def _fox_kernel(q_ref, k_ref, vt_ref, o_ref, m_ref, acc_ref, sa_ref, sb_ref):
    qi = pl.program_id(2)
    tq = q_ref.shape[2]
    m_ref[...] = jnp.full(m_ref.shape, -jnp.inf, F32)
    acc_ref[...] = jnp.zeros_like(acc_ref)

    def scores(j, s_ref):
        k0 = pl.multiple_of(j * tq, tq)
        for hh in range(2):
            s_ref[hh] = _dot_nt(k_ref[0, hh, pl.ds(k0, tq), :], q_ref[0, hh])

    def update(j, s_ref, masked):
        for hh in range(2):
            s_t = s_ref[hh]
            if masked:
                key = lax.broadcasted_iota(jnp.int32, (tq, tq), 0)
                qry = lax.broadcasted_iota(jnp.int32, (tq, tq), 1)
                s_t = jnp.where(key <= qry, s_t, -jnp.inf)
            m_old = m_ref[hh]
            m_new = jnp.maximum(m_old, jnp.max(s_t, axis=0, keepdims=True))
            p_t = jnp.exp2(s_t - m_new).astype(BF16)
            acc_ref[hh] = (jnp.exp2(m_old - m_new) * acc_ref[hh]
                           + _dot(vt_ref[0, hh, j], p_t))
            m_ref[hh] = m_new

    scores(0, sa_ref)

    def body(t, carry):
        j = 2 * t
        scores(j + 1, sb_ref)
        update(j, sa_ref, False)
        scores(j + 2, sa_ref)
        update(j + 1, sb_ref, False)
        return carry

    lax.fori_loop(0, qi // 2, body, 0)

    @pl.when(qi % 2 == 0)
    def _():
        update(qi, sa_ref, True)

    @pl.when(qi % 2 == 1)
    def _():
        scores(qi, sb_ref)
        update(qi - 1, sa_ref, False)
        update(qi, sb_ref, True)

    outs = []
    for hh in range(2):
        acc = acc_ref[hh]
        outs.append((acc / acc[FOX_DH:FOX_DH + 1, :])[:FOX_DH])
    o_ref[0] = jnp.concatenate(outs, axis=0).T.astype(BF16)


def _fox(qa, ka, vt):
    B, H, S, _ = qa.shape
    tq = TQ
    return pl.pallas_call(
        _fox_kernel,
        grid=(B, H // 2, S // tq),
        in_specs=[pl.BlockSpec((1, 2, tq, LANES), lambda b, h, q: (b, h, q, 0)),
                  pl.BlockSpec((1, 2, S, LANES), lambda b, h, q: (b, h, 0, 0)),
                  pl.BlockSpec((1, 2, S // tq, LANES, tq), lambda b, h, q: (b, h, 0, 0, 0))],
        out_specs=pl.BlockSpec((1, tq, LANES), lambda b, h, q: (b, q, h)),
        out_shape=jax.ShapeDtypeStruct((B, S, FOX_W), BF16),
        scratch_shapes=[pltpu.VMEM((2, 1, tq), F32), pltpu.VMEM((2, LANES, tq), F32),
                        pltpu.VMEM((2, tq, tq), F32), pltpu.VMEM((2, tq, tq), F32)],
        compiler_params=pltpu.CompilerParams(
            dimension_semantics=("parallel", "parallel", "arbitrary"),
            vmem_limit_bytes=VMEM_LIMIT),
        name="fox",
    )(qa, ka, vt)


def _gla_tables():
    C = GLA_CHUNK
    t = np.arange(C)[:, None]
    j = np.arange(C)[None, :]
    slabs = [(j <= t), (j > t)]
    masks = [np.eye(C, dtype=bool)]
    m = C // 2
    while m >= 1:
        g0 = (t // (2 * m)) * (2 * m)
        piv = g0 + m - 1
        upper = (t - g0) >= m
        slabs.append(np.where(upper, (j > piv) & (j <= t), (j > t) & (j <= piv)))
        s = np.arange(C)[None, :]
        masks.append(upper & ((s // (2 * m)) == (t // (2 * m))) & ((s % (2 * m)) < m))
        m //= 2
    cm = np.concatenate(slabs, axis=0).astype(np.float32)
    cm3 = np.concatenate([cm, cm, cm], axis=1)
    return cm3, np.stack(masks).astype(np.float32)


def _gla_kernel(gq_ref, gk_ref, gv_ref, gr_ref, glr_ref, wg_ref, bg_ref, ng_ref,
                cm_ref, mask_ref, o_ref, st_ref, la3_ref):
    C = GLA_CHUNK
    L = gq_ref.shape[1]
    n_lvl = mask_ref.shape[0] - 1

    @pl.when(pl.program_id(1) == 0)
    def _():
        st_ref[...] = jnp.zeros_like(st_ref)

    xg = jnp.dot(glr_ref[0], wg_ref[...], precision=HIGHEST,
                 preferred_element_type=F32) + bg_ref[...]
    la = _log_sigmoid(xg) * (1.0 / GLA_TAU)
    hi, mid, lo = _split3(la)
    for c in range(L // C):
        la3_ref[3 * C * c:3 * C * c + C, :] = hi[C * c:C * c + C]
        la3_ref[3 * C * c + C:3 * C * c + 2 * C, :] = mid[C * c:C * c + C]
        la3_ref[3 * C * c + 2 * C:3 * C * c + 3 * C, :] = lo[C * c:C * c + C]

    def chunk_group(gi, carry):
        pairs = [(u, h) for u in range(GLA_UNROLL) for h in range(GLA_H)]
        r0 = [pl.multiple_of((gi * GLA_UNROLL + u) * C, C) for u in range(GLA_UNROLL)]
        ks = [slice(GLA_DK * h, GLA_DK * h + GLA_DK) for h in range(GLA_H)]
        vs = [slice(GLA_DV * h, GLA_DV * h + GLA_DV) for h in range(GLA_H)]

        w, q, k = [], [], []
        for u in range(GLA_UNROLL):
            a0 = pl.multiple_of((gi * GLA_UNROLL + u) * 3 * C, 3 * C)
            w.append(jnp.exp(_dot(cm_ref[...], la3_ref[pl.ds(a0, 3 * C), :])))
            q.append(gq_ref[0, pl.ds(r0[u], C), :].astype(F32))
            k.append(gk_ref[0, pl.ds(r0[u], C), :].astype(F32))

        q_in, k_out, dec, q_lv, k_lv, v = {}, {}, {}, {}, {}, {}
        for u, h in pairs:
            wh, qh, kh = w[u][:, ks[h]], q[u][:, ks[h]], k[u][:, ks[h]]
            q_in[u, h] = (qh * wh[0:C]).astype(BF16)
            k_out[u, h] = (kh * wh[C:2 * C]).astype(BF16)
            dec[u, h] = wh[C - 1:C, :]
            q_lv[u, h] = [qh.astype(BF16)] + [(qh * wh[(2 + lv) * C:(3 + lv) * C]).astype(BF16)
                                              for lv in range(n_lvl)]
            k_lv[u, h] = [kh.astype(BF16)] + [(kh * wh[(2 + lv) * C:(3 + lv) * C]).astype(BF16)
                                              for lv in range(n_lvl)]
            v[u, h] = gv_ref[0, pl.ds(r0[u], C), vs[h]]

        upd = {p: _dot_tn(v[p], k_out[p]) for p in pairs}
        sc_parts = {p: [_dot_nt(a, b) for a, b in zip(q_lv[p], k_lv[p])] for p in pairs}

        inter = {}
        for h in range(GLA_H):
            st = st_ref[h]
            for u in range(GLA_UNROLL):
                inter[u, h] = _dot_nt(q_in[u, h], st.astype(BF16))
                st = st * dec[u, h] + upd[u, h]
            st_ref[h] = st

        sc = {}
        for p in pairs:
            acc = mask_ref[0] * sc_parts[p][0]
            for lv in range(n_lvl):
                acc = acc + mask_ref[1 + lv] * sc_parts[p][1 + lv]
            sc[p] = acc.astype(BF16)
        intra = {p: _dot(sc[p], v[p]) for p in pairs}
        for u, h in pairs:
            o = inter[u, h] + intra[u, h]
            y = o * lax.rsqrt(jnp.mean(o * o, axis=1, keepdims=True) + EPS)
            g = gr_ref[0, pl.ds(r0[u], C), vs[h]].astype(F32)
            o_ref[0, pl.ds(r0[u], C), vs[h]] = (
                y * ng_ref[:, vs[h]] * (g * _sigmoid(g))).astype(BF16)
        return carry

    lax.fori_loop(0, L // (C * GLA_UNROLL), chunk_group, 0)


def _gla(gq, gk, gv, gr, glr, wg, bg, ng, cm3, masks):
    B, S, _ = gq.shape
    L = GLA_BLOCK
    tok = lambda b, s: (b, s, 0)
    c2 = lambda b, s: (0, 0)
    return pl.pallas_call(
        _gla_kernel,
        grid=(B, S // L),
        in_specs=[pl.BlockSpec((1, L, GLA_KW), tok),
                  pl.BlockSpec((1, L, GLA_KW), tok),
                  pl.BlockSpec((1, L, GLA_VW), tok),
                  pl.BlockSpec((1, L, GLA_VW), tok),
                  pl.BlockSpec((1, L, GLA_RANK), tok),
                  pl.BlockSpec(wg.shape, c2),
                  pl.BlockSpec(bg.shape, c2),
                  pl.BlockSpec(ng.shape, c2),
                  pl.BlockSpec(cm3.shape, c2),
                  pl.BlockSpec(masks.shape, lambda b, s: (0, 0, 0))],
        out_specs=pl.BlockSpec((1, L, GLA_VW), tok),
        out_shape=jax.ShapeDtypeStruct((B, S, GLA_VW), BF16),
        scratch_shapes=[pltpu.VMEM((GLA_H, GLA_DV, GLA_DK), F32),
                        pltpu.VMEM((3 * L, GLA_KW), BF16)],
        compiler_params=pltpu.CompilerParams(
            dimension_semantics=("parallel", "arbitrary"), vmem_limit_bytes=VMEM_LIMIT),
        name="gla",
    )(gq, gk, gv, gr, glr, wg, bg, ng, cm3, masks)


def _merge_kernel(x_ref, ya_ref, yb_ref, mod_ref, wgate_ref, wa_ref, wb_ref, wo_ref,
                  lng_ref, lnb_ref, wr_ref, br_ref, tri_ref,
                  x1_ref, u2_ref, topv_ref, topi_ref, rank_ref, cnt_ref, carry_ref):
    tm = x_ref.shape[1]
    x = x_ref[0]
    mod = mod_ref[0]
    sh1, sc1, g1 = mod[0:1], mod[1:2], mod[2:3]
    sh2, sc2 = mod[3:4], mod[4:5]
    ub = (_ln(x) * (1.0 + sc1) + sh1).astype(BF16)
    br_a = _dot(ya_ref[0], wa_ref[...])
    br_b = _dot(yb_ref[0], wb_ref[...])
    merged = (_sigmoid(_dot(ub, wgate_ref[:, :D])) * br_a
              + _sigmoid(_dot(ub, wgate_ref[:, D:])) * br_b)
    mix = _dot(merged.astype(BF16), wo_ref[...])
    x1 = _ln(ALPHA * x + (1.0 + g1) * mix) * lng_ref[...] + lnb_ref[...]
    x1_ref[0] = x1
    u2 = _ln(x1) * (1.0 + sc2) + sh2
    _store_token_tiles(u2_ref, u2)

    u_hi = u2.astype(BF16)
    u_lo = (u2 - u_hi.astype(F32)).astype(BF16)
    logits = (_dot(u_hi, wr_ref[0]) + _dot(u_lo, wr_ref[0]) + _dot(u_hi, wr_ref[1])
              + br_ref[...])
    lane = lax.broadcasted_iota(jnp.int32, (tm, LANES), 1)
    vals = jnp.zeros((tm, LANES), F32)
    idxs = jnp.zeros((tm, LANES), jnp.int32)
    picked = jnp.zeros((tm, LANES), F32)
    cur = logits
    ixs = []
    for k in range(TOP_K):
        mx = jnp.max(cur, axis=1, keepdims=True)
        ix = jnp.min(jnp.where(cur == mx, lane, LANES), axis=1, keepdims=True)
        vals = jnp.where(lane == k, mx, vals)
        idxs = jnp.where(lane == k, ix, idxs)
        picked = jnp.where(lane == ix, 1.0, picked)
        cur = jnp.where(lane == ix, -jnp.inf, cur)
        ixs.append(ix)
    v0 = jnp.max(jnp.where(lane < TOP_K, vals, -jnp.inf), axis=1, keepdims=True)
    e = jnp.where(lane < TOP_K, jnp.exp(vals - v0), 0.0)
    topv_ref[0] = (e / jnp.sum(e, axis=1, keepdims=True))[:, :ROUTE_LANES]
    topi_ref[0] = idxs[:, :ROUTE_LANES]

    @pl.when((pl.program_id(0) == 0) & (pl.program_id(1) == 0))
    def _():
        carry_ref[...] = jnp.zeros_like(carry_ref)

    before = _dot(tri_ref[...], picked.astype(BF16)) + carry_ref[...]
    ranks = jnp.zeros((tm, LANES), F32)
    for k in range(TOP_K):
        rk = jnp.sum(jnp.where(lane == ixs[k], before, 0.0), axis=1, keepdims=True)
        ranks = jnp.where(lane == k, rk, ranks)
    rank_ref[0] = ranks.astype(jnp.int32)[:, :ROUTE_LANES]
    total = carry_ref[...] + jnp.sum(picked, axis=0, keepdims=True)
    carry_ref[...] = total
    cnt_ref[...] = total


def _merge(x, ya, yb, mod, wgate, wa, wb, wo, lng, lnb, wr, br):
    B, S, _ = x.shape
    tm = TM_MERGE
    tri = jnp.asarray(np.tril(np.ones((tm, tm), np.float32), -1), dtype=BF16)
    tok = lambda b, s: (b, s, 0)
    c2 = lambda b, s: (0, 0)
    nst = S // tm
    return pl.pallas_call(
        _merge_kernel,
        grid=(B, nst),
        in_specs=[pl.BlockSpec((1, tm, D), tok),
                  pl.BlockSpec((1, tm, FOX_W), tok),
                  pl.BlockSpec((1, tm, GLA_VW), tok),
                  pl.BlockSpec((1, 8, D), lambda b, s: (b, 0, 0)),
                  pl.BlockSpec(wgate.shape, c2, pipeline_mode=pl.Buffered(1)),
                  pl.BlockSpec(wa.shape, c2, pipeline_mode=pl.Buffered(1)),
                  pl.BlockSpec(wb.shape, c2, pipeline_mode=pl.Buffered(1)),
                  pl.BlockSpec(wo.shape, c2, pipeline_mode=pl.Buffered(1)),
                  pl.BlockSpec((1, D), c2),
                  pl.BlockSpec((1, D), c2),
                  pl.BlockSpec(wr.shape, lambda b, s: (0, 0, 0)),
                  pl.BlockSpec((1, LANES), c2),
                  pl.BlockSpec((tm, tm), c2)],
        out_specs=[pl.BlockSpec((1, tm, D), tok),
                   pl.BlockSpec((tm * ROW_TILE, LANES), lambda b, s: (b * nst + s, 0)),
                   pl.BlockSpec((1, tm, ROUTE_LANES), tok),
                   pl.BlockSpec((1, tm, ROUTE_LANES), tok),
                   pl.BlockSpec((1, tm, ROUTE_LANES), tok),
                   pl.BlockSpec((1, LANES), c2)],
        out_shape=[jax.ShapeDtypeStruct((B, S, D), F32),
                   jax.ShapeDtypeStruct((B * S * ROW_TILE, LANES), F32),
                   jax.ShapeDtypeStruct((B, S, ROUTE_LANES), F32),
                   jax.ShapeDtypeStruct((B, S, ROUTE_LANES), jnp.int32),
                   jax.ShapeDtypeStruct((B, S, ROUTE_LANES), jnp.int32),
                   jax.ShapeDtypeStruct((1, LANES), F32)],
        scratch_shapes=[pltpu.VMEM((1, LANES), F32)],
        compiler_params=pltpu.CompilerParams(
            dimension_semantics=("arbitrary", "arbitrary"), vmem_limit_bytes=VMEM_LIMIT),
        name="merge",
    )(x, ya, yb, mod, wgate, wa, wb, wo, lng, lnb, wr, br, tri)


GATHER_UNROLL = 8


def _store_token_tiles(ref, x):
    n = x.shape[0]
    for c in range(ROW_TILE):
        ref[pl.ds(c, n, stride=ROW_TILE), :] = x[:, LANES * c:LANES * c + LANES]


def _token_tile_cols(ref, row0, n, c):
    return ref[pl.ds(row0 * ROW_TILE + c, n, stride=ROW_TILE), :]


def _row_copy(src_hbm, row, buf, r, sem):
    src = src_hbm.at[pl.ds(pl.multiple_of(row * ROW_TILE, ROW_TILE), ROW_TILE), :]
    return pltpu.make_async_copy(src, buf.at[pl.ds(r * ROW_TILE, ROW_TILE), :], sem)


def _issue_rows(idx_ref, src_hbm, buf, sem):
    def issue(g, carry):
        for u in range(GATHER_UNROLL):
            r = g * GATHER_UNROLL + u
            src = src_hbm.at[pl.ds(pl.multiple_of(idx_ref[0, 0, r] * ROW_TILE, ROW_TILE),
                                   ROW_TILE), :]
            dst = buf.at[pl.ds(pl.multiple_of(r * ROW_TILE, ROW_TILE), ROW_TILE), :]
            pltpu.make_async_copy(src, dst, sem).start()
        return carry
    lax.fori_loop(0, buf.shape[0] // (ROW_TILE * GATHER_UNROLL), issue, 0)


def _wait_rows(src_hbm, buf, sem):
    pltpu.make_async_copy(src_hbm.at[pl.ds(0, buf.shape[0]), :], buf, sem).wait()


def _gather_step(step, n_steps, idx_ref, idx_next_ref, src_hbm, bufs, sems, compute,
                 row_priority):
    @pl.when(step == 0)
    def _():
        _issue_rows(idx_ref, src_hbm, bufs[0], sems.at[0])

    for par in range(2):
        @pl.when((step < n_steps) & (step % 2 == par))
        def _():
            nxt, cur = bufs[1 - par], bufs[par]
            _wait_rows(src_hbm, cur, sems.at[par])
            for r in range(nxt.shape[0] // ROW_TILE):
                _row_copy(src_hbm, idx_next_ref[0, 0, r], nxt, r,
                          sems.at[1 - par]).start(priority=row_priority(r))
            compute(cur)

            @pl.when(step == n_steps - 1)
            def _():
                _wait_rows(src_hbm, nxt, sems.at[1 - par])


def _moe_kernel(be_ref, nu_ref, tok_ref, tok_next_ref, u2_hbm, wup_ref, bup_ref, wdn_ref,
                bdn_ref, o_ref, buf0, buf1, wup_bf, wdn_bf, sems):
    i = pl.program_id(0)
    n_used = nu_ref[0]
    nb = o_ref.shape[0] // ROW_TILE

    @pl.when((i < n_used) & ((i == 0) | (be_ref[i] != be_ref[jnp.maximum(i - 1, 0)])))
    def _():
        wup_bf[...] = wup_ref[0].astype(BF16)
        wdn_bf[...] = wdn_ref[0].astype(BF16)

    def compute(buf):
        xb = jnp.concatenate([_token_tile_cols(buf, 0, nb, c).astype(BF16)
                              for c in range(ROW_TILE)], axis=1)
        h = _dot(xb, wup_bf[...]) + bup_ref[0]
        h_glu = jnp.minimum(h[:, :D], SWIGLU_LIMIT)
        h_lin = jnp.clip(h[:, D:], -SWIGLU_LIMIT, SWIGLU_LIMIT)
        act = h_glu * _sigmoid(SWIGLU_ALPHA * h_glu) * (h_lin + 1.0)
        _store_token_tiles(o_ref, _dot(act.astype(BF16), wdn_bf[...]) + bdn_ref[0])

    _gather_step(i, n_used, tok_ref, tok_next_ref, u2_hbm, (buf0, buf1), sems, compute,
                 lambda r: 1)

    @pl.when(i >= n_used)
    def _():
        o_ref[...] = jnp.zeros_like(o_ref)


def _moe(block_expert, n_used, row_tok, u2, wup, bup, wdn, bdn):
    nblk = block_expert.shape[0]
    nb = MOE_BLOCK
    ex = lambda i, be, nu: (be[i], 0, 0)
    grid_spec = pltpu.PrefetchScalarGridSpec(
        num_scalar_prefetch=2,
        grid=(nblk,),
        in_specs=[pl.BlockSpec((1, 1, nb), lambda i, be, nu: (i, 0, 0),
                               memory_space=pltpu.SMEM),
                  pl.BlockSpec((1, 1, nb),
                               lambda i, be, nu: (jnp.maximum(jnp.minimum(i + 1, nu[0] - 1), 0), 0, 0),
                               memory_space=pltpu.SMEM),
                  pl.BlockSpec(memory_space=pl.ANY),
                  pl.BlockSpec((1, D, 2 * D), ex),
                  pl.BlockSpec((1, 1, 2 * D), ex),
                  pl.BlockSpec((1, D, D), ex),
                  pl.BlockSpec((1, 1, D), ex)],
        out_specs=pl.BlockSpec((nb * ROW_TILE, LANES), lambda i, be, nu: (i, 0)),
        scratch_shapes=[pltpu.VMEM((nb * ROW_TILE, LANES), F32),
                        pltpu.VMEM((nb * ROW_TILE, LANES), F32),
                        pltpu.VMEM((D, 2 * D), BF16), pltpu.VMEM((D, D), BF16),
                        pltpu.SemaphoreType.DMA((2,))],
    )
    return pl.pallas_call(
        _moe_kernel,
        grid_spec=grid_spec,
        out_shape=jax.ShapeDtypeStruct((nblk * nb * ROW_TILE, LANES), F32),
        compiler_params=pltpu.CompilerParams(
            dimension_semantics=("arbitrary",), vmem_limit_bytes=VMEM_LIMIT_MOE),
        name="moe",
    )(block_expert, n_used, row_tok, row_tok, u2, wup, bup, wdn, bdn)


def _final_kernel(dest_ref, dest_next_ref, rows_hbm, x1_ref, gate_ref, g2_ref, lng_ref,
                  lnb_ref, o_ref, buf0, buf1, sems):
    tm = x1_ref.shape[0]

    def compute(buf):
        gates = gate_ref[...]
        lane = lax.broadcasted_iota(jnp.int32, gates.shape, 1)
        g = [_lane_col(gates, j, lane) for j in range(TOP_K)]
        cols = []
        for c in range(ROW_TILE):
            acc = g[0] * _token_tile_cols(buf, 0, tm, c)
            for j in range(1, TOP_K):
                acc = acc + g[j] * _token_tile_cols(buf, tm * j, tm, c)
            cols.append(acc)
        ffn = jnp.concatenate(cols, axis=1)
        z = ALPHA * x1_ref[...] + (1.0 + g2_ref[0]) * ffn
        o_ref[...] = _ln(z) * lng_ref[...] + lnb_ref[...]

    _gather_step(pl.program_id(0), pl.num_programs(0), dest_ref, dest_next_ref,
                 rows_hbm, (buf0, buf1), sems, compute, lambda r: r % 2)


def _final(dest, rows, x1, gates, g2, lng, lnb, tiles_per_seq):
    T = x1.shape[0]
    tm = TM_FINAL
    nt = T // tm
    tok = lambda i: (i, 0)
    c2 = lambda i: (0, 0)
    n = TOP_K * tm
    return pl.pallas_call(
        _final_kernel,
        grid=(nt,),
        in_specs=[pl.BlockSpec((1, 1, n), lambda i: (i, 0, 0), memory_space=pltpu.SMEM),
                  pl.BlockSpec((1, 1, n), lambda i: (jnp.minimum(i + 1, nt - 1), 0, 0),
                               memory_space=pltpu.SMEM),
                  pl.BlockSpec(memory_space=pl.ANY),
                  pl.BlockSpec((tm, D), tok),
                  pl.BlockSpec((tm, ROUTE_LANES), tok),
                  pl.BlockSpec((1, 1, D), lambda i: (i // tiles_per_seq, 0, 0)),
                  pl.BlockSpec((1, D), c2),
                  pl.BlockSpec((1, D), c2)],
        out_specs=pl.BlockSpec((tm, D), tok),
        out_shape=jax.ShapeDtypeStruct((T, D), F32),
        scratch_shapes=[pltpu.VMEM((n * ROW_TILE, LANES), F32),
                        pltpu.VMEM((n * ROW_TILE, LANES), F32),
                        pltpu.SemaphoreType.DMA((2,))],
        compiler_params=pltpu.CompilerParams(
            dimension_semantics=("arbitrary",), vmem_limit_bytes=VMEM_LIMIT),
        name="final",
    )(dest, dest, rows, x1, gates, g2, lng, lnb)


def _routing(top_idx, rank, counts):
    T = top_idx.shape[0]
    A = T * TOP_K
    nb = MOE_BLOCK
    nblk = A // nb + N_EXP
    e_flat = top_idx.reshape(A)
    rank = rank.reshape(A)
    padded = (counts + nb - 1) // nb * nb
    padded_end = jnp.cumsum(padded)
    padded_start = padded_end - padded
    dest = padded_start[e_flat] + rank
    blk_row0 = jnp.arange(nblk, dtype=jnp.int32) * nb
    block_expert = jnp.minimum(
        jnp.sum((padded_end[None, :] <= blk_row0[:, None]).astype(jnp.int32), axis=1),
        N_EXP - 1)
    order = jnp.argsort(e_flat, stable=True).astype(jnp.int32)
    starts = jnp.cumsum(counts) - counts
    local = (blk_row0 - padded_start[block_expert])[:, None] + jnp.arange(nb, dtype=jnp.int32)
    valid = local < counts[block_expert][:, None]
    src = jnp.clip(starts[block_expert][:, None] + local, 0, A - 1)
    row_tok = jnp.where(valid, order[src] // TOP_K, 0)
    n_used = (padded_end[-1] // nb).astype(jnp.int32).reshape(1)
    last_e = block_expert[jnp.maximum(n_used[0] - 1, 0)]
    block_expert = jnp.where(jnp.arange(nblk) < n_used[0], block_expert, last_e)
    return (block_expert, n_used, row_tok.astype(jnp.int32).reshape(nblk, 1, nb),
            dest.reshape(T, TOP_K))


def kernel(x, c, w_ada, b_ada, w_in, fox_f_bias, w_gla_gate, b_gla_gate, gla_norm_g,
           w_branch_a, w_branch_b, w_out, ln1_g, ln1_b, w_router, b_router,
           w_up, b_up, w_down, b_down, ln2_g, ln2_b):
    B, S, _ = x.shape
    T = B * S
    l = 0

    c_pad = jnp.zeros((8, D), F32).at[:B].set(c)
    mod = _ada(c_pad, w_ada[l], b_ada[l][None, :])[:B]
    mod6 = mod.reshape(B, 6, D)
    mod8 = jnp.concatenate([mod6, jnp.zeros((B, 2, D), F32)], axis=1)
    sh1, sc1 = mod6[:, 0:1], mod6[:, 1:2]
    g2 = mod6[:, 5:6]

    w = w_in[l]
    o = 0
    parts = []
    for width in (FOX_W, FOX_W, FOX_W, FOX_H, GLA_KW, GLA_KW, GLA_VW, GLA_VW, GLA_RANK, D, D):
        parts.append(w[:, o:o + width])
        o += width
    wq, wk, wv, wff, wgq, wgk, wgv, wgr, wglr, wga, wgb = parts

    wfox = jnp.concatenate([wq * (FOX_DH ** -0.5 * LOG2E), wk, wv], axis=1).astype(BF16)
    wgla = jnp.concatenate([wgq * GLA_DK ** -0.5, wgk, wgv, wgr], axis=1).astype(BF16)
    wsm = jnp.zeros((D, 2 * LANES), F32).at[:, :FOX_H].set(wff)
    wsm = wsm.at[:, LANES:LANES + GLA_RANK].set(wglr).astype(BF16)
    fb = jnp.zeros((1, LANES), F32).at[0, :FOX_H].set(fox_f_bias[l])
    tri = jnp.asarray(np.tril(np.ones((TM_IN, TM_IN), np.float32)), dtype=BF16)

    qa, ka, va, gq, gk, gv, gr, glr = _inproj(x, sh1, sc1, wfox, wgla, wsm, fb, tri)

    ya = _fox(qa, ka, va)

    cm3, masks = _gla_tables()
    yb = _gla(gq, gk, gv, gr, glr, w_gla_gate[l], b_gla_gate[l][None, :],
              gla_norm_g[l][None, :], jnp.asarray(cm3, dtype=BF16), jnp.asarray(masks))

    wgate = jnp.concatenate([wga, wgb], axis=1).astype(BF16)
    wr = jnp.zeros((D, LANES), F32).at[:, :N_EXP].set(w_router[l])
    wr_hi = wr.astype(BF16)
    wr = jnp.stack([wr_hi, (wr - wr_hi.astype(F32)).astype(BF16)])
    br = jnp.full((1, LANES), NEG, F32).at[0, :N_EXP].set(b_router[l])
    x1, u2, topv, topi, rank, counts = _merge(
        x, ya, yb, mod8, wgate, w_branch_a[l].astype(BF16), w_branch_b[l].astype(BF16),
        w_out[l].astype(BF16), ln1_g[l][None, :], ln1_b[l][None, :], wr, br)

    block_expert, n_used, row_tok, dest = _routing(
        topi.reshape(T, ROUTE_LANES)[:, :TOP_K], rank.reshape(T, ROUTE_LANES)[:, :TOP_K],
        counts[0, :N_EXP].astype(jnp.int32))
    rows = _moe(block_expert, n_used, row_tok, u2,
                w_up[l], b_up[l][:, None, :], w_down[l], b_down[l][:, None, :])

    nt = T // TM_FINAL
    dest_t = dest.reshape(nt, TM_FINAL, TOP_K).transpose(0, 2, 1).reshape(nt, 1, TOP_K * TM_FINAL)
    out = _final(dest_t, rows, x1.reshape(T, D), topv.reshape(T, ROUTE_LANES), g2,
                 ln2_g[l][None, :], ln2_b[l][None, :], S // TM_FINAL)
    return out.reshape(B, S, D)
```

```python
import functools

import numpy as np
import jax
import jax.numpy as jnp
from jax import lax
from jax.experimental import pallas as pl
from jax.experimental.pallas import tpu as pltpu

F32 = jnp.float32
BF16 = jnp.bfloat16
HIGHEST = lax.Precision.HIGHEST

D = 1024
FOX_H = 8
FOX_DH = 64
FOX_W = FOX_H * FOX_DH
GLA_H = 4
GLA_DK = 128
GLA_DV = 256
GLA_KW = GLA_H * GLA_DK
GLA_VW = GLA_H * GLA_DV
GLA_RANK = 16
GLA_TAU = 16.0
N_EXP = 32
TOP_K = 4
SWIGLU_LIMIT = 7.0
SWIGLU_ALPHA = 1.702
EPS = 1e-5
DEPTH = 1
ALPHA = (2 * DEPTH) ** 0.25
LANES = 128
ROW_TILE = 8
ROUTE_LANES = 8

GLA_CHUNK = 64
GLA_BLOCK = 512
GLA_UNROLL = 2
MOE_BLOCK = 512
TM_IN = 256
TM_MERGE = 512
TM_FINAL = 256
TQ = 512
VMEM_LIMIT = 56 * 1024 * 1024
VMEM_LIMIT_MOE = 60 * 1024 * 1024

NEG = -1e30
LOG2E = 1.4426950408889634


def _ln(x):
    mu = jnp.mean(x, axis=-1, keepdims=True)
    xc = x - mu
    var = jnp.mean(xc * xc, axis=-1, keepdims=True)
    return xc * lax.rsqrt(var + EPS)


def _sigmoid(x):
    return 1.0 / (1.0 + jnp.exp(-x))


def _log_sigmoid(x):
    return jnp.minimum(x, 0.0) - jnp.log(1.0 + jnp.exp(-jnp.abs(x)))


def _split3(x):
    hi = x.astype(BF16)
    r = x - hi.astype(F32)
    mid = r.astype(BF16)
    lo = (r - mid.astype(F32)).astype(BF16)
    return hi, mid, lo


def _lane_col(x, idx, lane):
    return jnp.sum(jnp.where(lane == idx, x, 0.0), axis=1, keepdims=True)


def _dot(a, b):
    return jnp.dot(a, b, preferred_element_type=F32)


def _dot_nt(a, b):
    return lax.dot_general(a, b, (((1,), (1,)), ((), ())), preferred_element_type=F32)


def _dot_tn(a, b):
    return lax.dot_general(a, b, (((0,), (0,)), ((), ())), preferred_element_type=F32)


def _ada_kernel(c_ref, w_ref, b_ref, o_ref):
    c = c_ref[...]
    ca = c * _sigmoid(c)
    o_ref[...] = jnp.dot(ca, w_ref[...], precision=HIGHEST,
                         preferred_element_type=F32) + b_ref[...]


def _ada(c_pad, w, b):
    n = w.shape[1]
    tn = 1536
    return pl.pallas_call(
        _ada_kernel,
        grid=(n // tn,),
        in_specs=[pl.BlockSpec((8, D), lambda j: (0, 0)),
                  pl.BlockSpec((D, tn), lambda j: (0, j)),
                  pl.BlockSpec((1, tn), lambda j: (0, j))],
        out_specs=pl.BlockSpec((8, tn), lambda j: (0, j)),
        out_shape=jax.ShapeDtypeStruct((8, n), F32),
        compiler_params=pltpu.CompilerParams(
            dimension_semantics=("arbitrary",), vmem_limit_bytes=VMEM_LIMIT),
        name="ada",
    )(c_pad, w, b)


def _inproj_kernel(x_ref, sh_ref, sc_ref, wfox_ref, wgla_ref, wsm_ref, fb_ref, tri_ref,
                   qa_ref, ka_ref, vt_ref, gq_ref, gk_ref, gv_ref, gr_ref, glr_ref,
                   carry_ref):
    tm = x_ref.shape[1]

    @pl.when(pl.program_id(1) == 0)
    def _():
        carry_ref[...] = jnp.zeros_like(carry_ref)

    u = _ln(x_ref[0]) * (1.0 + sc_ref[0]) + sh_ref[0]
    ub = u.astype(BF16)

    sm = _dot(ub, wsm_ref[...])
    glr_ref[0] = sm[:, LANES:LANES + GLA_RANK]
    lane = lax.broadcasted_iota(jnp.int32, (tm, LANES), 1)
    lf = jnp.where(lane < FOX_H, _log_sigmoid(sm[:, :LANES] + fb_ref[...]), 0.0)
    hi, mid, lo = _split3(lf)
    tri = tri_ref[...]
    cum = _dot(tri, hi) + _dot(tri, mid) + _dot(tri, lo) + carry_ref[...]
    carry_ref[...] = cum[tm - 1:tm, :]
    chi, cmid, clo = _split3(cum * LOG2E)
    chi, cmid, clo = chi.astype(F32), cmid.astype(F32), clo.astype(F32)

    ex_v = jnp.where(lane == FOX_DH, 1.0, 0.0)
    is_q1 = (lane >= FOX_DH + 3) & (lane < FOX_DH + 6)
    is_k1 = (lane >= FOX_DH) & (lane < FOX_DH + 3)
    low = lane < FOX_DH
    q_all = _dot(ub, wfox_ref[:, 0:FOX_W])
    k_all = _dot(ub, wfox_ref[:, FOX_W:2 * FOX_W])
    v_all = _dot(ub, wfox_ref[:, 2 * FOX_W:3 * FOX_W])
    for hp in range(FOX_H // 2):
        pair = slice(LANES * hp, LANES * hp + LANES)
        for hh in range(2):
            h = 2 * hp + hh
            c0 = _lane_col(chi, h, lane)
            c1 = _lane_col(cmid, h, lane)
            c2 = _lane_col(clo, h, lane)
            ex_q = jnp.where(lane == FOX_DH, c0,
                             jnp.where(lane == FOX_DH + 1, c1,
                                       jnp.where(lane == FOX_DH + 2, c2,
                                                 jnp.where(is_q1, 1.0, 0.0))))
            ex_k = jnp.where(lane == FOX_DH + 3, -c0,
                             jnp.where(lane == FOX_DH + 4, -c1,
                                       jnp.where(lane == FOX_DH + 5, -c2,
                                                 jnp.where(is_k1, 1.0, 0.0))))
            qh, kh, vh = q_all[:, pair], k_all[:, pair], v_all[:, pair]
            if hh == 1:
                qh, kh, vh = (pltpu.roll(t, FOX_DH, 1) for t in (qh, kh, vh))
            qa_ref[0, h] = jnp.where(low, qh, ex_q).astype(BF16)
            ka_ref[0, h] = jnp.where(low, kh, ex_k).astype(BF16)
            vt_ref[0, h, 0] = jnp.where(low, vh, ex_v).T.astype(BF16)

    for j in range(GLA_KW // 256):
        gq_ref[0, :, 256 * j:256 * j + 256] = _dot(
            ub, wgla_ref[:, 256 * j:256 * j + 256]).astype(BF16)
        gk_ref[0, :, 256 * j:256 * j + 256] = _dot(
            ub, wgla_ref[:, GLA_KW + 256 * j:GLA_KW + 256 * j + 256]).astype(BF16)
    for j in range(GLA_VW // 256):
        o = 2 * GLA_KW + 256 * j
        gv_ref[0, :, 256 * j:256 * j + 256] = _dot(ub, wgla_ref[:, o:o + 256]).astype(BF16)
        o = 2 * GLA_KW + GLA_VW + 256 * j
        gr_ref[0, :, 256 * j:256 * j + 256] = _dot(ub, wgla_ref[:, o:o + 256]).astype(BF16)


def _inproj(x, sh1, sc1, wfox, wgla, wsm, fb, tri):
    B, S, _ = x.shape
    tm = TM_IN
    const = lambda b, s: (0, 0)
    tok = lambda b, s: (b, s, 0)
    head = lambda b, s: (b, 0, s, 0)
    vec = lambda b, s: (b, 0, 0)
    hs = jax.ShapeDtypeStruct((B, FOX_H, S, LANES), BF16)
    per_q = TQ // tm
    return pl.pallas_call(
        _inproj_kernel,
        grid=(B, S // tm),
        in_specs=[pl.BlockSpec((1, tm, D), tok),
                  pl.BlockSpec((1, 1, D), vec),
                  pl.BlockSpec((1, 1, D), vec),
                  pl.BlockSpec(wfox.shape, const),
                  pl.BlockSpec(wgla.shape, const),
                  pl.BlockSpec(wsm.shape, const),
                  pl.BlockSpec((1, LANES), const),
                  pl.BlockSpec((tm, tm), const)],
        out_specs=[pl.BlockSpec((1, FOX_H, tm, LANES), head),
                   pl.BlockSpec((1, FOX_H, tm, LANES), head),
                   pl.BlockSpec((1, FOX_H, 1, LANES, tm),
                                lambda b, s: (b, 0, s // per_q, 0, s % per_q)),
                   pl.BlockSpec((1, tm, GLA_KW), tok),
                   pl.BlockSpec((1, tm, GLA_KW), tok),
                   pl.BlockSpec((1, tm, GLA_VW), tok),
                   pl.BlockSpec((1, tm, GLA_VW), tok),
                   pl.BlockSpec((1, tm, GLA_RANK), tok)],
        out_shape=[hs, hs, jax.ShapeDtypeStruct((B, FOX_H, S // TQ, LANES, TQ), BF16),
                   jax.ShapeDtypeStruct((B, S, GLA_KW), BF16),
                   jax.ShapeDtypeStruct((B, S, GLA_KW), BF16),
                   jax.ShapeDtypeStruct((B, S, GLA_VW), BF16),
                   jax.ShapeDtypeStruct((B, S, GLA_VW), BF16),
                   jax.ShapeDtypeStruct((B, S, GLA_RANK), F32)],
        scratch_shapes=[pltpu.VMEM((1, LANES), F32)],
        compiler_params=pltpu.CompilerParams(
            dimension_semantics=("parallel", "arbitrary"), vmem_limit_bytes=VMEM_LIMIT),
        name="inproj",
    )(x, sh1, sc1, wfox, wgla, wsm, fb, tri)


def _fox_kernel(q_ref, k_ref, vt_ref, o_ref, m_ref, acc_ref, sa_ref, sb_ref):
    qi = pl.program_id(2)
    tq = q_ref.shape[2]
    m_ref[...] = jnp.full(m_ref.shape, -jnp.inf, F32)
    acc_ref[...] = jnp.zeros_like(acc_ref)

    def scores(j, s_ref):
        k0 = pl.multiple_of(j * tq, tq)
        for hh in range(2):
            s_ref[hh] = _dot_nt(k_ref[0, hh, pl.ds(k0, tq), :], q_ref[0, hh])

    def update(j, s_ref, masked):
        for hh in range(2):
            s_t = s_ref[hh]
            if masked:
                key = lax.broadcasted_iota(jnp.int32, (tq, tq), 0)
                qry = lax.broadcasted_iota(jnp.int32, (tq, tq), 1)
                s_t = jnp.where(key <= qry, s_t, -jnp.inf)
            m_old = m_ref[hh]
            m_new = jnp.maximum(m_old, jnp.max(s_t, axis=0, keepdims=True))
            p_t = jnp.exp2(s_t - m_new).astype(BF16)
            acc_ref[hh] = (jnp.exp2(m_old - m_new) * acc_ref[hh]
                           + _dot(vt_ref[0, hh, j], p_t))
            m_ref[hh] = m_new

    scores(0, sa_ref)

    def body(t, carry):
        j = 2 * t
        scores(j + 1, sb_ref)
        update(j, sa_ref, False)
        scores(j + 2, sa_ref)
        update(j + 1, sb_ref, False)
        return carry

    lax.fori_loop(0, qi // 2, body, 0)

    @pl.when(qi % 2 == 0)
    def _():
        update(qi, sa_ref, True)

    @pl.when(qi % 2 == 1)
    def _():
        scores(qi, sb_ref)
        update(qi - 1, sa_ref, False)
        update(qi, sb_ref, True)

    outs = []
    for hh in range(2):
        acc = acc_ref[hh]
        outs.append((acc / acc[FOX_DH:FOX_DH + 1, :])[:FOX_DH])
    o_ref[0] = jnp.concatenate(outs, axis=0).T.astype(BF16)


def _fox(qa, ka, vt):
    B, H, S, _ = qa.shape
    tq = TQ
    return pl.pallas_call(
        _fox_kernel,
        grid=(B, H // 2, S // tq),
        in_specs=[pl.BlockSpec((1, 2, tq, LANES), lambda b, h, q: (b, h, q, 0)),
                  pl.BlockSpec((1, 2, S, LANES), lambda b, h, q: (b, h, 0, 0)),
                  pl.BlockSpec((1, 2, S // tq, LANES, tq), lambda b, h, q: (b, h, 0, 0, 0))],
        out_specs=pl.BlockSpec((1, tq, LANES), lambda b, h, q: (b, q, h)),
        out_shape=jax.ShapeDtypeStruct((B, S, FOX_W), BF16),
        scratch_shapes=[pltpu.VMEM((2, 1, tq), F32), pltpu.VMEM((2, LANES, tq), F32),
                        pltpu.VMEM((2, tq, tq), F32), pltpu.VMEM((2, tq, tq), F32)],
        compiler_params=pltpu.CompilerParams(
            dimension_semantics=("parallel", "parallel", "arbitrary"),
            vmem_limit_bytes=VMEM_LIMIT),
        name="fox",
    )(qa, ka, vt)


def _gla_tables():
    C = GLA_CHUNK
    t = np.arange(C)[:, None]
    j = np.arange(C)[None, :]
    slabs = [(j <= t), (j > t)]
    masks = [np.eye(C, dtype=bool)]
    m = C // 2
    while m >= 1:
        g0 = (t // (2 * m)) * (2 * m)
        piv = g0 + m - 1
        upper = (t - g0) >= m
        slabs.append(np.where(upper, (j > piv) & (j <= t), (j > t) & (j <= piv)))
        s = np.arange(C)[None, :]
        masks.append(upper & ((s // (2 * m)) == (t // (2 * m))) & ((s % (2 * m)) < m))
        m //= 2
    cm = np.concatenate(slabs, axis=0).astype(np.float32)
    cm3 = np.concatenate([cm, cm, cm], axis=1)
    return cm3, np.stack(masks).astype(np.float32)


def _gla_kernel(gq_ref, gk_ref, gv_ref, gr_ref, glr_ref, wg_ref, bg_ref, ng_ref,
                cm_ref, mask_ref, o_ref, st_ref, la3_ref):
    C = GLA_CHUNK
    L = gq_ref.shape[1]
    n_lvl = mask_ref.shape[0] - 1

    @pl.when(pl.program_id(1) == 0)
    def _():
        st_ref[...] = jnp.zeros_like(st_ref)

    xg = jnp.dot(glr_ref[0], wg_ref[...], precision=HIGHEST,
                 preferred_element_type=F32) + bg_ref[...]
    la = _log_sigmoid(xg) * (1.0 / GLA_TAU)
    hi, mid, lo = _split3(la)
    for c in range(L // C):
        la3_ref[3 * C * c:3 * C * c + C, :] = hi[C * c:C * c + C]
        la3_ref[3 * C * c + C:3 * C * c + 2 * C, :] = mid[C * c:C * c + C]
        la3_ref[3 * C * c + 2 * C:3 * C * c + 3 * C, :] = lo[C * c:C * c + C]

    def chunk_group(gi, carry):
        pairs = [(u, h) for u in range(GLA_UNROLL) for h in range(GLA_H)]
        r0 = [pl.multiple_of((gi * GLA_UNROLL + u) * C, C) for u in range(GLA_UNROLL)]
        ks = [slice(GLA_DK * h, GLA_DK * h + GLA_DK) for h in range(GLA_H)]
        vs = [slice(GLA_DV * h, GLA_DV * h + GLA_DV) for h in range(GLA_H)]

        w, q, k = [], [], []
        for u in range(GLA_UNROLL):
            a0 = pl.multiple_of((gi * GLA_UNROLL + u) * 3 * C, 3 * C)
            w.append(jnp.exp(_dot(cm_ref[...], la3_ref[pl.ds(a0, 3 * C), :])))
            q.append(gq_ref[0, pl.ds(r0[u], C), :].astype(F32))
            k.append(gk_ref[0, pl.ds(r0[u], C), :].astype(F32))

        q_in, k_out, dec, q_lv, k_lv, v = {}, {}, {}, {}, {}, {}
        for u, h in pairs:
            wh, qh, kh = w[u][:, ks[h]], q[u][:, ks[h]], k[u][:, ks[h]]
            q_in[u, h] = (qh * wh[0:C]).astype(BF16)
            k_out[u, h] = (kh * wh[C:2 * C]).astype(BF16)
            dec[u, h] = wh[C - 1:C, :]
            q_lv[u, h] = [qh.astype(BF16)] + [(qh * wh[(2 + lv) * C:(3 + lv) * C]).astype(BF16)
                                              for lv in range(n_lvl)]
            k_lv[u, h] = [kh.astype(BF16)] + [(kh * wh[(2 + lv) * C:(3 + lv) * C]).astype(BF16)
                                              for lv in range(n_lvl)]
            v[u, h] = gv_ref[0, pl.ds(r0[u], C), vs[h]]

        upd = {p: _dot_tn(v[p], k_out[p]) for p in pairs}
        sc_parts = {p: [_dot_nt(a, b) for a, b in zip(q_lv[p], k_lv[p])] for p in pairs}

        inter = {}
        for h in range(GLA_H):
            st = st_ref[h]
            for u in range(GLA_UNROLL):
                inter[u, h] = _dot_nt(q_in[u, h], st.astype(BF16))
                st = st * dec[u, h] + upd[u, h]
            st_ref[h] = st

        sc = {}
        for p in pairs:
            acc = mask_ref[0] * sc_parts[p][0]
            for lv in range(n_lvl):
                acc = acc + mask_ref[1 + lv] * sc_parts[p][1 + lv]
            sc[p] = acc.astype(BF16)
        intra = {p: _dot(sc[p], v[p]) for p in pairs}
        for u, h in pairs:
            o = inter[u, h] + intra[u, h]
            y = o * lax.rsqrt(jnp.mean(o * o, axis=1, keepdims=True) + EPS)
            g = gr_ref[0, pl.ds(r0[u], C), vs[h]].astype(F32)
            o_ref[0, pl.ds(r0[u], C), vs[h]] = (
                y * ng_ref[:, vs[h]] * (g * _sigmoid(g))).astype(BF16)
        return carry

    lax.fori_loop(0, L // (C * GLA_UNROLL), chunk_group, 0)


def _gla(gq, gk, gv, gr, glr, wg, bg, ng, cm3, masks):
    B, S, _ = gq.shape
    L = GLA_BLOCK
    tok = lambda b, s: (b, s, 0)
    c2 = lambda b, s: (0, 0)
    return pl.pallas_call(
        _gla_kernel,
        grid=(B, S // L),
        in_specs=[pl.BlockSpec((1, L, GLA_KW), tok),
                  pl.BlockSpec((1, L, GLA_KW), tok),
                  pl.BlockSpec((1, L, GLA_VW), tok),
                  pl.BlockSpec((1, L, GLA_VW), tok),
                  pl.BlockSpec((1, L, GLA_RANK), tok),
                  pl.BlockSpec(wg.shape, c2),
                  pl.BlockSpec(bg.shape, c2),
                  pl.BlockSpec(ng.shape, c2),
                  pl.BlockSpec(cm3.shape, c2),
                  pl.BlockSpec(masks.shape, lambda b, s: (0, 0, 0))],
        out_specs=pl.BlockSpec((1, L, GLA_VW), tok),
        out_shape=jax.ShapeDtypeStruct((B, S, GLA_VW), BF16),
        scratch_shapes=[pltpu.VMEM((GLA_H, GLA_DV, GLA_DK), F32),
                        pltpu.VMEM((3 * L, GLA_KW), BF16)],
        compiler_params=pltpu.CompilerParams(
            dimension_semantics=("parallel", "arbitrary"), vmem_limit_bytes=VMEM_LIMIT),
        name="gla",
    )(gq, gk, gv, gr, glr, wg, bg, ng, cm3, masks)


def _merge_kernel(x_ref, ya_ref, yb_ref, mod_ref, wgate_ref, wa_ref, wb_ref, wo_ref,
                  lng_ref, lnb_ref, wr_ref, br_ref, tri_ref,
                  x1_ref, u2_ref, topv_ref, topi_ref, rank_ref, cnt_ref, carry_ref):
    tm = x_ref.shape[1]
    x = x_ref[0]
    mod = mod_ref[0]
    sh1, sc1, g1 = mod[0:1], mod[1:2], mod[2:3]
    sh2, sc2 = mod[3:4], mod[4:5]
    ub = (_ln(x) * (1.0 + sc1) + sh1).astype(BF16)
    br_a = _dot(ya_ref[0], wa_ref[...])
    br_b = _dot(yb_ref[0], wb_ref[...])
    merged = (_sigmoid(_dot(ub, wgate_ref[:, :D])) * br_a
              + _sigmoid(_dot(ub, wgate_ref[:, D:])) * br_b)
    mix = _dot(merged.astype(BF16), wo_ref[...])
    x1 = _ln(ALPHA * x + (1.0 + g1) * mix) * lng_ref[...] + lnb_ref[...]
    x1_ref[0] = x1
    u2 = _ln(x1) * (1.0 + sc2) + sh2
    _store_token_tiles(u2_ref, u2)

    u_hi = u2.astype(BF16)
    u_lo = (u2 - u_hi.astype(F32)).astype(BF16)
    logits = (_dot(u_hi, wr_ref[0]) + _dot(u_lo, wr_ref[0]) + _dot(u_hi, wr_ref[1])
              + br_ref[...])
    lane = lax.broadcasted_iota(jnp.int32, (tm, LANES), 1)
    vals = jnp.zeros((tm, LANES), F32)
    idxs = jnp.zeros((tm, LANES), jnp.int32)
    picked = jnp.zeros((tm, LANES), F32)
    cur = logits
    ixs = []
    for k in range(TOP_K):
        mx = jnp.max(cur, axis=1, keepdims=True)
        ix = jnp.min(jnp.where(cur == mx, lane, LANES), axis=1, keepdims=True)
        vals = jnp.where(lane == k, mx, vals)
        idxs = jnp.where(lane == k, ix, idxs)
        picked = jnp.where(lane == ix, 1.0, picked)
        cur = jnp.where(lane == ix, -jnp.inf, cur)
        ixs.append(ix)
    v0 = jnp.max(jnp.where(lane < TOP_K, vals, -jnp.inf), axis=1, keepdims=True)
    e = jnp.where(lane < TOP_K, jnp.exp(vals - v0), 0.0)
    topv_ref[0] = (e / jnp.sum(e, axis=1, keepdims=True))[:, :ROUTE_LANES]
    topi_ref[0] = idxs[:, :ROUTE_LANES]

    @pl.when((pl.program_id(0) == 0) & (pl.program_id(1) == 0))
    def _():
        carry_ref[...] = jnp.zeros_like(carry_ref)

    before = _dot(tri_ref[...], picked.astype(BF16)) + carry_ref[...]
    ranks = jnp.zeros((tm, LANES), F32)
    for k in range(TOP_K):
        rk = jnp.sum(jnp.where(lane == ixs[k], before, 0.0), axis=1, keepdims=True)
        ranks = jnp.where(lane == k, rk, ranks)
    rank_ref[0] = ranks.astype(jnp.int32)[:, :ROUTE_LANES]
    total = carry_ref[...] + jnp.sum(picked, axis=0, keepdims=True)
    carry_ref[...] = total
    cnt_ref[...] = total


def _merge(x, ya, yb, mod, wgate, wa, wb, wo, lng, lnb, wr, br):
    B, S, _ = x.shape
    tm = TM_MERGE
    tri = jnp.asarray(np.tril(np.ones((tm, tm), np.float32), -1), dtype=BF16)
    tok = lambda b, s: (b, s, 0)
    c2 = lambda b, s: (0, 0)
    nst = S // tm
    return pl.pallas_call(
        _merge_kernel,
        grid=(B, nst),
        in_specs=[pl.BlockSpec((1, tm, D), tok),
                  pl.BlockSpec((1, tm, FOX_W), tok),
                  pl.BlockSpec((1, tm, GLA_VW), tok),
                  pl.BlockSpec((1, 8, D), lambda b, s: (b, 0, 0)),
                  pl.BlockSpec(wgate.shape, c2, pipeline_mode=pl.Buffered(1)),
                  pl.BlockSpec(wa.shape, c2, pipeline_mode=pl.Buffered(1)),
                  pl.BlockSpec(wb.shape, c2, pipeline_mode=pl.Buffered(1)),
                  pl.BlockSpec(wo.shape, c2, pipeline_mode=pl.Buffered(1)),
                  pl.BlockSpec((1, D), c2),
                  pl.BlockSpec((1, D), c2),
                  pl.BlockSpec(wr.shape, lambda b, s: (0, 0, 0)),
                  pl.BlockSpec((1, LANES), c2),
                  pl.BlockSpec((tm, tm), c2)],
        out_specs=[pl.BlockSpec((1, tm, D), tok),
                   pl.BlockSpec((tm * ROW_TILE, LANES), lambda b, s: (b * nst + s, 0)),
                   pl.BlockSpec((1, tm, ROUTE_LANES), tok),
                   pl.BlockSpec((1, tm, ROUTE_LANES), tok),
                   pl.BlockSpec((1, tm, ROUTE_LANES), tok),
                   pl.BlockSpec((1, LANES), c2)],
        out_shape=[jax.ShapeDtypeStruct((B, S, D), F32),
                   jax.ShapeDtypeStruct((B * S * ROW_TILE, LANES), F32),
                   jax.ShapeDtypeStruct((B, S, ROUTE_LANES), F32),
                   jax.ShapeDtypeStruct((B, S, ROUTE_LANES), jnp.int32),
                   jax.ShapeDtypeStruct((B, S, ROUTE_LANES), jnp.int32),
                   jax.ShapeDtypeStruct((1, LANES), F32)],
        scratch_shapes=[pltpu.VMEM((1, LANES), F32)],
        compiler_params=pltpu.CompilerParams(
            dimension_semantics=("arbitrary", "arbitrary"), vmem_limit_bytes=VMEM_LIMIT),
        name="merge",
    )(x, ya, yb, mod, wgate, wa, wb, wo, lng, lnb, wr, br, tri)


GATHER_UNROLL = 8


def _store_token_tiles(ref, x):
    n = x.shape[0]
    for c in range(ROW_TILE):
        ref[pl.ds(c, n, stride=ROW_TILE), :] = x[:, LANES * c:LANES * c + LANES]


def _token_tile_cols(ref, row0, n, c):
    return ref[pl.ds(row0 * ROW_TILE + c, n, stride=ROW_TILE), :]


def _row_copy(src_hbm, row, buf, r, sem):
    src = src_hbm.at[pl.ds(pl.multiple_of(row * ROW_TILE, ROW_TILE), ROW_TILE), :]
    return pltpu.make_async_copy(src, buf.at[pl.ds(r * ROW_TILE, ROW_TILE), :], sem)


def _issue_rows(idx_ref, src_hbm, buf, sem):
    def issue(g, carry):
        for u in range(GATHER_UNROLL):
            r = g * GATHER_UNROLL + u
            src = src_hbm.at[pl.ds(pl.multiple_of(idx_ref[0, 0, r] * ROW_TILE, ROW_TILE),
                                   ROW_TILE), :]
            dst = buf.at[pl.ds(pl.multiple_of(r * ROW_TILE, ROW_TILE), ROW_TILE), :]
            pltpu.make_async_copy(src, dst, sem).start()
        return carry
    lax.fori_loop(0, buf.shape[0] // (ROW_TILE * GATHER_UNROLL), issue, 0)


def _wait_rows(src_hbm, buf, sem):
    pltpu.make_async_copy(src_hbm.at[pl.ds(0, buf.shape[0]), :], buf, sem).wait()


def _gather_step(step, n_steps, idx_ref, idx_next_ref, src_hbm, bufs, anchor_buf, sems,
                 compute, row_priority):
    @pl.when(step == 0)
    def _():
        _issue_rows(idx_ref, src_hbm, bufs[0], sems.at[0])

    for par in range(2):
        @pl.when((step < n_steps) & (step % 2 == par))
        def _():
            nxt, cur = bufs[1 - par], bufs[par]
            _wait_rows(src_hbm, cur, sems.at[par])
            for r in range(nxt.shape[0] // ROW_TILE):
                _row_copy(src_hbm, idx_next_ref[0, 0, r], nxt, r,
                          sems.at[1 - par]).start(priority=row_priority(r))
            anchor = pltpu.make_async_copy(src_hbm.at[pl.ds(0, ROW_TILE), :], anchor_buf,
                                           sems.at[2])
            anchor.start()
            compute(cur, anchor.wait)

            @pl.when(step == n_steps - 1)
            def _():
                _wait_rows(src_hbm, nxt, sems.at[1 - par])


def _moe_kernel(be_ref, nu_ref, tok_ref, tok_next_ref, u2_hbm, wup_ref, bup_ref, wdn_ref,
                bdn_ref, o_ref, buf0, buf1, anchor_buf, wup_bf, wdn_bf, sems):
    i = pl.program_id(0)
    n_used = nu_ref[0]
    nb = o_ref.shape[0] // ROW_TILE

    @pl.when((i < n_used) & ((i == 0) | (be_ref[i] != be_ref[jnp.maximum(i - 1, 0)])))
    def _():
        wup_bf[...] = wup_ref[0].astype(BF16)
        wdn_bf[...] = wdn_ref[0].astype(BF16)

    def compute(buf, anchor_wait):
        xb = jnp.concatenate([_token_tile_cols(buf, 0, nb, c).astype(BF16)
                              for c in range(ROW_TILE)], axis=1)
        h = _dot(xb, wup_bf[...]) + bup_ref[0]
        anchor_wait()
        h_glu = jnp.minimum(h[:, :D], SWIGLU_LIMIT)
        h_lin = jnp.clip(h[:, D:], -SWIGLU_LIMIT, SWIGLU_LIMIT)
        act = h_glu * _sigmoid(SWIGLU_ALPHA * h_glu) * (h_lin + 1.0)
        _store_token_tiles(o_ref, _dot(act.astype(BF16), wdn_bf[...]) + bdn_ref[0])

    _gather_step(i, n_used, tok_ref, tok_next_ref, u2_hbm, (buf0, buf1), anchor_buf, sems,
                 compute, lambda r: r % 2)

    @pl.when(i >= n_used)
    def _():
        o_ref[...] = jnp.zeros_like(o_ref)


def _moe(block_expert, n_used, row_tok, u2, wup, bup, wdn, bdn):
    nblk = block_expert.shape[0]
    nb = MOE_BLOCK
    ex = lambda i, be, nu: (be[i], 0, 0)
    grid_spec = pltpu.PrefetchScalarGridSpec(
        num_scalar_prefetch=2,
        grid=(nblk,),
        in_specs=[pl.BlockSpec((1, 1, nb), lambda i, be, nu: (i, 0, 0),
                               memory_space=pltpu.SMEM),
                  pl.BlockSpec((1, 1, nb),
                               lambda i, be, nu: (jnp.maximum(jnp.minimum(i + 1, nu[0] - 1), 0), 0, 0),
                               memory_space=pltpu.SMEM),
                  pl.BlockSpec(memory_space=pl.ANY),
                  pl.BlockSpec((1, D, 2 * D), ex),
                  pl.BlockSpec((1, 1, 2 * D), ex),
                  pl.BlockSpec((1, D, D), ex),
                  pl.BlockSpec((1, 1, D), ex)],
        out_specs=pl.BlockSpec((nb * ROW_TILE, LANES), lambda i, be, nu: (i, 0)),
        scratch_shapes=[pltpu.VMEM((nb * ROW_TILE, LANES), F32),
                        pltpu.VMEM((nb * ROW_TILE, LANES), F32),
                        pltpu.VMEM((ROW_TILE, LANES), F32),
                        pltpu.VMEM((D, 2 * D), BF16), pltpu.VMEM((D, D), BF16),
                        pltpu.SemaphoreType.DMA((3,))],
    )
    return pl.pallas_call(
        _moe_kernel,
        grid_spec=grid_spec,
        out_shape=jax.ShapeDtypeStruct((nblk * nb * ROW_TILE, LANES), F32),
        compiler_params=pltpu.CompilerParams(
            dimension_semantics=("arbitrary",), vmem_limit_bytes=VMEM_LIMIT_MOE),
        name="moe",
    )(block_expert, n_used, row_tok, row_tok, u2, wup, bup, wdn, bdn)


def _final_kernel(dest_ref, dest_next_ref, rows_hbm, x1_ref, gate_ref, g2_ref, lng_ref,
                  lnb_ref, o_ref, buf0, buf1, anchor_buf, sems):
    tm = x1_ref.shape[0]

    def compute(buf, anchor_wait):
        gates = gate_ref[...]
        lane = lax.broadcasted_iota(jnp.int32, gates.shape, 1)
        g = [_lane_col(gates, j, lane) for j in range(TOP_K)]
        cols = []
        for c in range(ROW_TILE):
            acc = g[0] * _token_tile_cols(buf, 0, tm, c)
            for j in range(1, TOP_K):
                acc = acc + g[j] * _token_tile_cols(buf, tm * j, tm, c)
            cols.append(acc)
        ffn = jnp.concatenate(cols, axis=1)
        anchor_wait()
        z = ALPHA * x1_ref[...] + (1.0 + g2_ref[0]) * ffn
        o_ref[...] = _ln(z) * lng_ref[...] + lnb_ref[...]

    _gather_step(pl.program_id(0), pl.num_programs(0), dest_ref, dest_next_ref,
                 rows_hbm, (buf0, buf1), anchor_buf, sems, compute, lambda r: r % 2)


def _final(dest, rows, x1, gates, g2, lng, lnb, tiles_per_seq):
    T = x1.shape[0]
    tm = TM_FINAL
    nt = T // tm
    tok = lambda i: (i, 0)
    c2 = lambda i: (0, 0)
    n = TOP_K * tm
    return pl.pallas_call(
        _final_kernel,
        grid=(nt,),
        in_specs=[pl.BlockSpec((1, 1, n), lambda i: (i, 0, 0), memory_space=pltpu.SMEM),
                  pl.BlockSpec((1, 1, n), lambda i: (jnp.minimum(i + 1, nt - 1), 0, 0),
                               memory_space=pltpu.SMEM),
                  pl.BlockSpec(memory_space=pl.ANY),
                  pl.BlockSpec((tm, D), tok),
                  pl.BlockSpec((tm, ROUTE_LANES), tok),
                  pl.BlockSpec((1, 1, D), lambda i: (i // tiles_per_seq, 0, 0)),
                  pl.BlockSpec((1, D), c2),
                  pl.BlockSpec((1, D), c2)],
        out_specs=pl.BlockSpec((tm, D), tok),
        out_shape=jax.ShapeDtypeStruct((T, D), F32),
        scratch_shapes=[pltpu.VMEM((n * ROW_TILE, LANES), F32),
                        pltpu.VMEM((n * ROW_TILE, LANES), F32),
                        pltpu.VMEM((ROW_TILE, LANES), F32),
                        pltpu.SemaphoreType.DMA((3,))],
        compiler_params=pltpu.CompilerParams(
            dimension_semantics=("arbitrary",), vmem_limit_bytes=VMEM_LIMIT),
        name="final",
    )(dest, dest, rows, x1, gates, g2, lng, lnb)


def _routing(top_idx, rank, counts):
    T = top_idx.shape[0]
    A = T * TOP_K
    nb = MOE_BLOCK
    nblk = A // nb + N_EXP
    e_flat = top_idx.reshape(A)
    rank = rank.reshape(A)
    padded = (counts + nb - 1) // nb * nb
    padded_end = jnp.cumsum(padded)
    padded_start = padded_end - padded
    dest = padded_start[e_flat] + rank
    blk_row0 = jnp.arange(nblk, dtype=jnp.int32) * nb
    block_expert = jnp.minimum(
        jnp.sum((padded_end[None, :] <= blk_row0[:, None]).astype(jnp.int32), axis=1),
        N_EXP - 1)
    order = jnp.argsort(e_flat, stable=True).astype(jnp.int32)
    starts = jnp.cumsum(counts) - counts
    local = (blk_row0 - padded_start[block_expert])[:, None] + jnp.arange(nb, dtype=jnp.int32)
    valid = local < counts[block_expert][:, None]
    src = jnp.clip(starts[block_expert][:, None] + local, 0, A - 1)
    row_tok = jnp.where(valid, order[src] // TOP_K, 0)
    n_used = (padded_end[-1] // nb).astype(jnp.int32).reshape(1)
    last_e = block_expert[jnp.maximum(n_used[0] - 1, 0)]
    block_expert = jnp.where(jnp.arange(nblk) < n_used[0], block_expert, last_e)
    return (block_expert, n_used, row_tok.astype(jnp.int32).reshape(nblk, 1, nb),
            dest.reshape(T, TOP_K))


def kernel(x, c, w_ada, b_ada, w_in, fox_f_bias, w_gla_gate, b_gla_gate, gla_norm_g,
           w_branch_a, w_branch_b, w_out, ln1_g, ln1_b, w_router, b_router,
           w_up, b_up, w_down, b_down, ln2_g, ln2_b):
    B, S, _ = x.shape
    T = B * S
    l = 0

    c_pad = jnp.zeros((8, D), F32).at[:B].set(c)
    mod = _ada(c_pad, w_ada[l], b_ada[l][None, :])[:B]
    mod6 = mod.reshape(B, 6, D)
    mod8 = jnp.concatenate([mod6, jnp.zeros((B, 2, D), F32)], axis=1)
    sh1, sc1 = mod6[:, 0:1], mod6[:, 1:2]
    g2 = mod6[:, 5:6]

    w = w_in[l]
    o = 0
    parts = []
    for width in (FOX_W, FOX_W, FOX_W, FOX_H, GLA_KW, GLA_KW, GLA_VW, GLA_VW, GLA_RANK, D, D):
        parts.append(w[:, o:o + width])
        o += width
    wq, wk, wv, wff, wgq, wgk, wgv, wgr, wglr, wga, wgb = parts

    wfox = jnp.concatenate([wq * (FOX_DH ** -0.5 * LOG2E), wk, wv], axis=1).astype(BF16)
    wgla = jnp.concatenate([wgq * GLA_DK ** -0.5, wgk, wgv, wgr], axis=1).astype(BF16)
    wsm = jnp.zeros((D, 2 * LANES), F32).at[:, :FOX_H].set(wff)
    wsm = wsm.at[:, LANES:LANES + GLA_RANK].set(wglr).astype(BF16)
    fb = jnp.zeros((1, LANES), F32).at[0, :FOX_H].set(fox_f_bias[l])
    tri = jnp.asarray(np.tril(np.ones((TM_IN, TM_IN), np.float32)), dtype=BF16)

    qa, ka, va, gq, gk, gv, gr, glr = _inproj(x, sh1, sc1, wfox, wgla, wsm, fb, tri)

    ya = _fox(qa, ka, va)

    cm3, masks = _gla_tables()
    yb = _gla(gq, gk, gv, gr, glr, w_gla_gate[l], b_gla_gate[l][None, :],
              gla_norm_g[l][None, :], jnp.asarray(cm3, dtype=BF16), jnp.asarray(masks))

    wgate = jnp.concatenate([wga, wgb], axis=1).astype(BF16)
    wr = jnp.zeros((D, LANES), F32).at[:, :N_EXP].set(w_router[l])
    wr_hi = wr.astype(BF16)
    wr = jnp.stack([wr_hi, (wr - wr_hi.astype(F32)).astype(BF16)])
    br = jnp.full((1, LANES), NEG, F32).at[0, :N_EXP].set(b_router[l])
    x1, u2, topv, topi, rank, counts = _merge(
        x, ya, yb, mod8, wgate, w_branch_a[l].astype(BF16), w_branch_b[l].astype(BF16),
        w_out[l].astype(BF16), ln1_g[l][None, :], ln1_b[l][None, :], wr, br)

    block_expert, n_used, row_tok, dest = _routing(
        topi.reshape(T, ROUTE_LANES)[:, :TOP_K], rank.reshape(T, ROUTE_LANES)[:, :TOP_K],
        counts[0, :N_EXP].astype(jnp.int32))
    rows = _moe(block_expert, n_used, row_tok, u2,
                w_up[l], b_up[l][:, None, :], w_down[l], b_down[l][:, None, :])

    nt = T // TM_FINAL
    dest_t = dest.reshape(nt, TM_FINAL, TOP_K).transpose(0, 2, 1).reshape(nt, 1, TOP_K * TM_FINAL)
    out = _final(dest_t, rows, x1.reshape(T, D), topv.reshape(T, ROUTE_LANES), g2,
                 ln2_g[l][None, :], ln2_b[l][None, :], S // TM_FINAL)
    return out.reshape(B, S, D)
```

```python
import functools

import numpy as np
import jax
import jax.numpy as jnp
from jax import lax
from jax.experimental import pallas as pl
from jax.experimental.pallas import tpu as pltpu

F32 = jnp.float32
BF16 = jnp.bfloat16
HIGHEST = lax.Precision.HIGHEST

D = 1024
FOX_H = 8
FOX_DH = 64
FOX_W = FOX_H * FOX_DH
GLA_H = 4
GLA_DK = 128
GLA_DV = 256
GLA_KW = GLA_H * GLA_DK
GLA_VW = GLA_H * GLA_DV
GLA_RANK = 16
GLA_TAU = 16.0
N_EXP = 32
TOP_K = 4
SWIGLU_LIMIT = 7.0
SWIGLU_ALPHA = 1.702
EPS = 1e-5
DEPTH = 1
ALPHA = (2 * DEPTH) ** 0.25
LANES = 128
ROW_TILE = 8
ROUTE_LANES = 8

GLA_CHUNK = 64
GLA_BLOCK = 512
GLA_UNROLL = 2
MOE_BLOCK = 512
TM_IN = 512
TM_MERGE = 512
TM_FINAL = 256
TQ = 512
VMEM_LIMIT = 56 * 1024 * 1024
VMEM_LIMIT_MOE = 60 * 1024 * 1024

NEG = -1e30
LOG2E = 1.4426950408889634


def _ln(x):
    mu = jnp.mean(x, axis=-1, keepdims=True)
    xc = x - mu
    var = jnp.mean(xc * xc, axis=-1, keepdims=True)
    return xc * lax.rsqrt(var + EPS)


def _sigmoid(x):
    return 1.0 / (1.0 + jnp.exp(-x))


def _log_sigmoid(x):
    return jnp.minimum(x, 0.0) - jnp.log(1.0 + jnp.exp(-jnp.abs(x)))


def _split3(x):
    hi = x.astype(BF16)
    r = x - hi.astype(F32)
    mid = r.astype(BF16)
    lo = (r - mid.astype(F32)).astype(BF16)
    return hi, mid, lo


def _lane_col(x, idx, lane):
    return jnp.sum(jnp.where(lane == idx, x, 0.0), axis=1, keepdims=True)


def _dot(a, b):
    return jnp.dot(a, b, preferred_element_type=F32)


def _dot_nt(a, b):
    return lax.dot_general(a, b, (((1,), (1,)), ((), ())), preferred_element_type=F32)


def _dot_tn(a, b):
    return lax.dot_general(a, b, (((0,), (0,)), ((), ())), preferred_element_type=F32)


def _ada_kernel(c_ref, w_ref, b_ref, o_ref):
    c = c_ref[...]
    ca = c * _sigmoid(c)
    o_ref[...] = jnp.dot(ca, w_ref[...], precision=HIGHEST,
                         preferred_element_type=F32) + b_ref[...]


def _ada(c_pad, w, b):
    n = w.shape[1]
    tn = 1536
    return pl.pallas_call(
        _ada_kernel,
        grid=(n // tn,),
        in_specs=[pl.BlockSpec((8, D), lambda j: (0, 0)),
                  pl.BlockSpec((D, tn), lambda j: (0, j)),
                  pl.BlockSpec((1, tn), lambda j: (0, j))],
        out_specs=pl.BlockSpec((8, tn), lambda j: (0, j)),
        out_shape=jax.ShapeDtypeStruct((8, n), F32),
        compiler_params=pltpu.CompilerParams(
            dimension_semantics=("arbitrary",), vmem_limit_bytes=VMEM_LIMIT),
        name="ada",
    )(c_pad, w, b)


def _inproj_kernel(x_ref, sh_ref, sc_ref, wfox_ref, wgla_ref, wsm_ref, fb_ref, tri_ref,
                   qa_ref, ka_ref, vt_ref, gq_ref, gk_ref, gv_ref, gr_ref, glr_ref,
                   carry_ref):
    tm = x_ref.shape[1]

    @pl.when(pl.program_id(1) == 0)
    def _():
        carry_ref[...] = jnp.zeros_like(carry_ref)

    u = _ln(x_ref[0]) * (1.0 + sc_ref[0]) + sh_ref[0]
    ub = u.astype(BF16)

    sm = _dot(ub, wsm_ref[...])
    glr_ref[0] = sm[:, LANES:LANES + GLA_RANK]
    lane = lax.broadcasted_iota(jnp.int32, (tm, LANES), 1)
    lf = jnp.where(lane < FOX_H, _log_sigmoid(sm[:, :LANES] + fb_ref[...]), 0.0)
    hi, mid, lo = _split3(lf)
    tri = tri_ref[...]
    cum = _dot(tri, hi) + _dot(tri, mid) + _dot(tri, lo) + carry_ref[...]
    carry_ref[...] = cum[tm - 1:tm, :]
    chi, cmid, clo = _split3(cum * LOG2E)
    chi, cmid, clo = chi.astype(F32), cmid.astype(F32), clo.astype(F32)

    ex_v = jnp.where(lane == FOX_DH, 1.0, 0.0)
    is_q1 = (lane >= FOX_DH + 3) & (lane < FOX_DH + 6)
    is_k1 = (lane >= FOX_DH) & (lane < FOX_DH + 3)
    low = lane < FOX_DH
    q_all = _dot(ub, wfox_ref[:, 0:FOX_W])
    k_all = _dot(ub, wfox_ref[:, FOX_W:2 * FOX_W])
    v_all = _dot(ub, wfox_ref[:, 2 * FOX_W:3 * FOX_W])
    for hp in range(FOX_H // 2):
        pair = slice(LANES * hp, LANES * hp + LANES)
        for hh in range(2):
            h = 2 * hp + hh
            c0 = _lane_col(chi, h, lane)
            c1 = _lane_col(cmid, h, lane)
            c2 = _lane_col(clo, h, lane)
            ex_q = jnp.where(lane == FOX_DH, c0,
                             jnp.where(lane == FOX_DH + 1, c1,
                                       jnp.where(lane == FOX_DH + 2, c2,
                                                 jnp.where(is_q1, 1.0, 0.0))))
            ex_k = jnp.where(lane == FOX_DH + 3, -c0,
                             jnp.where(lane == FOX_DH + 4, -c1,
                                       jnp.where(lane == FOX_DH + 5, -c2,
                                                 jnp.where(is_k1, 1.0, 0.0))))
            qh, kh, vh = q_all[:, pair], k_all[:, pair], v_all[:, pair]
            if hh == 1:
                qh, kh, vh = (pltpu.roll(t, FOX_DH, 1) for t in (qh, kh, vh))
            qa_ref[0, h] = jnp.where(low, qh, ex_q).astype(BF16)
            ka_ref[0, h] = jnp.where(low, kh, ex_k).astype(BF16)
            vt_ref[0, h, 0] = jnp.where(low, vh, ex_v).T.astype(BF16)

    for j in range(GLA_KW // 256):
        gq_ref[0, :, 256 * j:256 * j + 256] = _dot(
            ub, wgla_ref[:, 256 * j:256 * j + 256]).astype(BF16)
        gk_ref[0, :, 256 * j:256 * j + 256] = _dot(
            ub, wgla_ref[:, GLA_KW + 256 * j:GLA_KW + 256 * j + 256]).astype(BF16)
    for j in range(GLA_VW // 256):
        o = 2 * GLA_KW + 256 * j
        gv_ref[0, :, 256 * j:256 * j + 256] = _dot(ub, wgla_ref[:, o:o + 256]).astype(BF16)
        o = 2 * GLA_KW + GLA_VW + 256 * j
        gr_ref[0, :, 256 * j:256 * j + 256] = _dot(ub, wgla_ref[:, o:o + 256]).astype(BF16)


def _inproj(x, sh1, sc1, wfox, wgla, wsm, fb, tri):
    B, S, _ = x.shape
    tm = TM_IN
    const = lambda b, s: (0, 0)
    tok = lambda b, s: (b, s, 0)
    head = lambda b, s: (b, 0, s, 0)
    vec = lambda b, s: (b, 0, 0)
    hs = jax.ShapeDtypeStruct((B, FOX_H, S, LANES), BF16)
    per_q = TQ // tm
    return pl.pallas_call(
        _inproj_kernel,
        grid=(B, S // tm),
        in_specs=[pl.BlockSpec((1, tm, D), tok),
                  pl.BlockSpec((1, 1, D), vec),
                  pl.BlockSpec((1, 1, D), vec),
                  pl.BlockSpec(wfox.shape, const),
                  pl.BlockSpec(wgla.shape, const),
                  pl.BlockSpec(wsm.shape, const),
                  pl.BlockSpec((1, LANES), const),
                  pl.BlockSpec((tm, tm), const)],
        out_specs=[pl.BlockSpec((1, FOX_H, tm, LANES), head),
                   pl.BlockSpec((1, FOX_H, tm, LANES), head),
                   pl.BlockSpec((1, FOX_H, 1, LANES, tm),
                                lambda b, s: (b, 0, s // per_q, 0, s % per_q)),
                   pl.BlockSpec((1, tm, GLA_KW), tok),
                   pl.BlockSpec((1, tm, GLA_KW), tok),
                   pl.BlockSpec((1, tm, GLA_VW), tok),
                   pl.BlockSpec((1, tm, GLA_VW), tok),
                   pl.BlockSpec((1, tm, GLA_RANK), tok)],
        out_shape=[hs, hs, jax.ShapeDtypeStruct((B, FOX_H, S // TQ, LANES, TQ), BF16),
                   jax.ShapeDtypeStruct((B, S, GLA_KW), BF16),
                   jax.ShapeDtypeStruct((B, S, GLA_KW), BF16),
                   jax.ShapeDtypeStruct((B, S, GLA_VW), BF16),
                   jax.ShapeDtypeStruct((B, S, GLA_VW), BF16),
                   jax.ShapeDtypeStruct((B, S, GLA_RANK), F32)],
        scratch_shapes=[pltpu.VMEM((1, LANES), F32)],
        compiler_params=pltpu.CompilerParams(
            dimension_semantics=("parallel", "arbitrary"), vmem_limit_bytes=VMEM_LIMIT),
        name="inproj",
    )(x, sh1, sc1, wfox, wgla, wsm, fb, tri)


def _fox_kernel(q_ref, k_ref, vt_ref, o_ref, m_ref, acc_ref, sa_ref, sb_ref):
    qi = pl.program_id(2)
    tq = q_ref.shape[2]
    m_ref[...] = jnp.full(m_ref.shape, -jnp.inf, F32)
    acc_ref[...] = jnp.zeros_like(acc_ref)

    def scores(j, s_ref):
        k0 = pl.multiple_of(j * tq, tq)
        for hh in range(2):
            s_ref[hh] = _dot_nt(k_ref[0, hh, pl.ds(k0, tq), :], q_ref[0, hh])

    def update(j, s_ref, masked):
        for hh in range(2):
            s_t = s_ref[hh]
            if masked:
                key = lax.broadcasted_iota(jnp.int32, (tq, tq), 0)
                qry = lax.broadcasted_iota(jnp.int32, (tq, tq), 1)
                s_t = jnp.where(key <= qry, s_t, -jnp.inf)
            m_old = m_ref[hh]
            m_new = jnp.maximum(m_old, jnp.max(s_t, axis=0, keepdims=True))
            p_t = jnp.exp2(s_t - m_new).astype(BF16)
            acc_ref[hh] = (jnp.exp2(m_old - m_new) * acc_ref[hh]
                           + _dot(vt_ref[0, hh, j], p_t))
            m_ref[hh] = m_new

    scores(0, sa_ref)

    def body(t, carry):
        j = 2 * t
        scores(j + 1, sb_ref)
        update(j, sa_ref, False)
        scores(j + 2, sa_ref)
        update(j + 1, sb_ref, False)
        return carry

    lax.fori_loop(0, qi // 2, body, 0)

    @pl.when(qi % 2 == 0)
    def _():
        update(qi, sa_ref, True)

    @pl.when(qi % 2 == 1)
    def _():
        scores(qi, sb_ref)
        update(qi - 1, sa_ref, False)
        update(qi, sb_ref, True)

    outs = []
    for hh in range(2):
        acc = acc_ref[hh]
        outs.append((acc / acc[FOX_DH:FOX_DH + 1, :])[:FOX_DH])
    o_ref[0] = jnp.concatenate(outs, axis=0).T.astype(BF16)


def _fox(qa, ka, vt):
    B, H, S, _ = qa.shape
    tq = TQ
    return pl.pallas_call(
        _fox_kernel,
        grid=(B, H // 2, S // tq),
        in_specs=[pl.BlockSpec((1, 2, tq, LANES), lambda b, h, q: (b, h, q, 0)),
                  pl.BlockSpec((1, 2, S, LANES), lambda b, h, q: (b, h, 0, 0)),
                  pl.BlockSpec((1, 2, S // tq, LANES, tq), lambda b, h, q: (b, h, 0, 0, 0))],
        out_specs=pl.BlockSpec((1, tq, LANES), lambda b, h, q: (b, q, h)),
        out_shape=jax.ShapeDtypeStruct((B, S, FOX_W), BF16),
        scratch_shapes=[pltpu.VMEM((2, 1, tq), F32), pltpu.VMEM((2, LANES, tq), F32),
                        pltpu.VMEM((2, tq, tq), F32), pltpu.VMEM((2, tq, tq), F32)],
        compiler_params=pltpu.CompilerParams(
            dimension_semantics=("parallel", "parallel", "arbitrary"),
            vmem_limit_bytes=VMEM_LIMIT),
        name="fox",
    )(qa, ka, vt)


def _gla_tables():
    C = GLA_CHUNK
    t = np.arange(C)[:, None]
    j = np.arange(C)[None, :]
    slabs = [(j <= t), (j > t)]
    masks = [np.eye(C, dtype=bool)]
    m = C // 2
    while m >= 1:
        g0 = (t // (2 * m)) * (2 * m)
        piv = g0 + m - 1
        upper = (t - g0) >= m
        slabs.append(np.where(upper, (j > piv) & (j <= t), (j > t) & (j <= piv)))
        s = np.arange(C)[None, :]
        masks.append(upper & ((s // (2 * m)) == (t // (2 * m))) & ((s % (2 * m)) < m))
        m //= 2
    cm = np.concatenate(slabs, axis=0).astype(np.float32)
    cm3 = np.concatenate([cm, cm, cm], axis=1)
    return cm3, np.stack(masks).astype(np.float32)


def _gla_kernel(gq_ref, gk_ref, gv_ref, gr_ref, glr_ref, wg_ref, bg_ref, ng_ref,
                cm_ref, mask_ref, o_ref, st_ref, la3_ref):
    C = GLA_CHUNK
    L = gq_ref.shape[1]
    n_lvl = mask_ref.shape[0] - 1

    @pl.when(pl.program_id(1) == 0)
    def _():
        st_ref[...] = jnp.zeros_like(st_ref)

    xg = jnp.dot(glr_ref[0], wg_ref[...], precision=HIGHEST,
                 preferred_element_type=F32) + bg_ref[...]
    la = _log_sigmoid(xg) * (1.0 / GLA_TAU)
    hi, mid, lo = _split3(la)
    for c in range(L // C):
        la3_ref[3 * C * c:3 * C * c + C, :] = hi[C * c:C * c + C]
        la3_ref[3 * C * c + C:3 * C * c + 2 * C, :] = mid[C * c:C * c + C]
        la3_ref[3 * C * c + 2 * C:3 * C * c + 3 * C, :] = lo[C * c:C * c + C]

    def chunk_group(gi, carry):
        pairs = [(u, h) for u in range(GLA_UNROLL) for h in range(GLA_H)]
        r0 = [pl.multiple_of((gi * GLA_UNROLL + u) * C, C) for u in range(GLA_UNROLL)]
        ks = [slice(GLA_DK * h, GLA_DK * h + GLA_DK) for h in range(GLA_H)]
        vs = [slice(GLA_DV * h, GLA_DV * h + GLA_DV) for h in range(GLA_H)]

        w, q, k = [], [], []
        for u in range(GLA_UNROLL):
            a0 = pl.multiple_of((gi * GLA_UNROLL + u) * 3 * C, 3 * C)
            w.append(jnp.exp(_dot(cm_ref[...], la3_ref[pl.ds(a0, 3 * C), :])))
            q.append(gq_ref[0, pl.ds(r0[u], C), :].astype(F32))
            k.append(gk_ref[0, pl.ds(r0[u], C), :].astype(F32))

        q_in, k_out, dec, q_lv, k_lv, v = {}, {}, {}, {}, {}, {}
        for u, h in pairs:
            wh, qh, kh = w[u][:, ks[h]], q[u][:, ks[h]], k[u][:, ks[h]]
            q_in[u, h] = (qh * wh[0:C]).astype(BF16)
            k_out[u, h] = (kh * wh[C:2 * C]).astype(BF16)
            dec[u, h] = wh[C - 1:C, :]
            q_lv[u, h] = [qh.astype(BF16)] + [(qh * wh[(2 + lv) * C:(3 + lv) * C]).astype(BF16)
                                              for lv in range(n_lvl)]
            k_lv[u, h] = [kh.astype(BF16)] + [(kh * wh[(2 + lv) * C:(3 + lv) * C]).astype(BF16)
                                              for lv in range(n_lvl)]
            v[u, h] = gv_ref[0, pl.ds(r0[u], C), vs[h]]

        upd = {p: _dot_tn(v[p], k_out[p]) for p in pairs}
        sc_parts = {p: [_dot_nt(a, b) for a, b in zip(q_lv[p], k_lv[p])] for p in pairs}

        inter = {}
        for h in range(GLA_H):
            st = st_ref[h]
            for u in range(GLA_UNROLL):
                inter[u, h] = _dot_nt(q_in[u, h], st.astype(BF16))
                st = st * dec[u, h] + upd[u, h]
            st_ref[h] = st

        sc = {}
        for p in pairs:
            acc = mask_ref[0] * sc_parts[p][0]
            for lv in range(n_lvl):
                acc = acc + mask_ref[1 + lv] * sc_parts[p][1 + lv]
            sc[p] = acc.astype(BF16)
        intra = {p: _dot(sc[p], v[p]) for p in pairs}
        for u, h in pairs:
            o = inter[u, h] + intra[u, h]
            y = o * lax.rsqrt(jnp.mean(o * o, axis=1, keepdims=True) + EPS)
            g = gr_ref[0, pl.ds(r0[u], C), vs[h]].astype(F32)
            o_ref[0, pl.ds(r0[u], C), vs[h]] = (
                y * ng_ref[:, vs[h]] * (g * _sigmoid(g))).astype(BF16)
        return carry

    lax.fori_loop(0, L // (C * GLA_UNROLL), chunk_group, 0)


def _gla(gq, gk, gv, gr, glr, wg, bg, ng, cm3, masks):
    B, S, _ = gq.shape
    L = GLA_BLOCK
    tok = lambda b, s: (b, s, 0)
    c2 = lambda b, s: (0, 0)
    return pl.pallas_call(
        _gla_kernel,
        grid=(B, S // L),
        in_specs=[pl.BlockSpec((1, L, GLA_KW), tok),
                  pl.BlockSpec((1, L, GLA_KW), tok),
                  pl.BlockSpec((1, L, GLA_VW), tok),
                  pl.BlockSpec((1, L, GLA_VW), tok),
                  pl.BlockSpec((1, L, GLA_RANK), tok),
                  pl.BlockSpec(wg.shape, c2),
                  pl.BlockSpec(bg.shape, c2),
                  pl.BlockSpec(ng.shape, c2),
                  pl.BlockSpec(cm3.shape, c2),
                  pl.BlockSpec(masks.shape, lambda b, s: (0, 0, 0))],
        out_specs=pl.BlockSpec((1, L, GLA_VW), tok),
        out_shape=jax.ShapeDtypeStruct((B, S, GLA_VW), BF16),
        scratch_shapes=[pltpu.VMEM((GLA_H, GLA_DV, GLA_DK), F32),
                        pltpu.VMEM((3 * L, GLA_KW), BF16)],
        compiler_params=pltpu.CompilerParams(
            dimension_semantics=("parallel", "arbitrary"), vmem_limit_bytes=VMEM_LIMIT),
        name="gla",
    )(gq, gk, gv, gr, glr, wg, bg, ng, cm3, masks)


def _merge_kernel(x_ref, ya_ref, yb_ref, mod_ref, wgate_ref, wa_ref, wb_ref, wo_ref,
                  lng_ref, lnb_ref, wr_ref, br_ref, tri_ref,
                  x1_ref, u2_ref, topv_ref, topi_ref, rank_ref, cnt_ref, carry_ref):
    tm = x_ref.shape[1]
    x = x_ref[0]
    mod = mod_ref[0]
    sh1, sc1, g1 = mod[0:1], mod[1:2], mod[2:3]
    sh2, sc2 = mod[3:4], mod[4:5]
    ub = (_ln(x) * (1.0 + sc1) + sh1).astype(BF16)
    br_a = _dot(ya_ref[0], wa_ref[...])
    br_b = _dot(yb_ref[0], wb_ref[...])
    merged = (_sigmoid(_dot(ub, wgate_ref[:, :D])) * br_a
              + _sigmoid(_dot(ub, wgate_ref[:, D:])) * br_b)
    mix = _dot(merged.astype(BF16), wo_ref[...])
    x1 = _ln(ALPHA * x + (1.0 + g1) * mix) * lng_ref[...] + lnb_ref[...]
    x1_ref[0] = x1
    u2 = _ln(x1) * (1.0 + sc2) + sh2
    _store_token_tiles(u2_ref, u2)

    u_hi = u2.astype(BF16)
    u_lo = (u2 - u_hi.astype(F32)).astype(BF16)
    logits = (_dot(u_hi, wr_ref[0]) + _dot(u_lo, wr_ref[0]) + _dot(u_hi, wr_ref[1])
              + br_ref[...])
    lane = lax.broadcasted_iota(jnp.int32, (tm, LANES), 1)
    vals = jnp.zeros((tm, LANES), F32)
    idxs = jnp.zeros((tm, LANES), jnp.int32)
    picked = jnp.zeros((tm, LANES), F32)
    cur = logits
    ixs = []
    for k in range(TOP_K):
        mx = jnp.max(cur, axis=1, keepdims=True)
        ix = jnp.min(jnp.where(cur == mx, lane, LANES), axis=1, keepdims=True)
        vals = jnp.where(lane == k, mx, vals)
        idxs = jnp.where(lane == k, ix, idxs)
        picked = jnp.where(lane == ix, 1.0, picked)
        cur = jnp.where(lane == ix, -jnp.inf, cur)
        ixs.append(ix)
    v0 = jnp.max(jnp.where(lane < TOP_K, vals, -jnp.inf), axis=1, keepdims=True)
    e = jnp.where(lane < TOP_K, jnp.exp(vals - v0), 0.0)
    topv_ref[0] = (e / jnp.sum(e, axis=1, keepdims=True))[:, :ROUTE_LANES]
    topi_ref[0] = idxs[:, :ROUTE_LANES]

    @pl.when((pl.program_id(0) == 0) & (pl.program_id(1) == 0))
    def _():
        carry_ref[...] = jnp.zeros_like(carry_ref)

    before = _dot(tri_ref[...], picked.astype(BF16)) + carry_ref[...]
    ranks = jnp.zeros((tm, LANES), F32)
    for k in range(TOP_K):
        rk = jnp.sum(jnp.where(lane == ixs[k], before, 0.0), axis=1, keepdims=True)
        ranks = jnp.where(lane == k, rk, ranks)
    rank_ref[0] = ranks.astype(jnp.int32)[:, :ROUTE_LANES]
    total = carry_ref[...] + jnp.sum(picked, axis=0, keepdims=True)
    carry_ref[...] = total
    cnt_ref[...] = total


def _merge(x, ya, yb, mod, wgate, wa, wb, wo, lng, lnb, wr, br):
    B, S, _ = x.shape
    tm = TM_MERGE
    tri = jnp.asarray(np.tril(np.ones((tm, tm), np.float32), -1), dtype=BF16)
    tok = lambda b, s: (b, s, 0)
    c2 = lambda b, s: (0, 0)
    nst = S // tm
    return pl.pallas_call(
        _merge_kernel,
        grid=(B, nst),
        in_specs=[pl.BlockSpec((1, tm, D), tok),
                  pl.BlockSpec((1, tm, FOX_W), tok),
                  pl.BlockSpec((1, tm, GLA_VW), tok),
                  pl.BlockSpec((1, 8, D), lambda b, s: (b, 0, 0)),
                  pl.BlockSpec(wgate.shape, c2, pipeline_mode=pl.Buffered(1)),
                  pl.BlockSpec(wa.shape, c2, pipeline_mode=pl.Buffered(1)),
                  pl.BlockSpec(wb.shape, c2, pipeline_mode=pl.Buffered(1)),
                  pl.BlockSpec(wo.shape, c2, pipeline_mode=pl.Buffered(1)),
                  pl.BlockSpec((1, D), c2),
                  pl.BlockSpec((1, D), c2),
                  pl.BlockSpec(wr.shape, lambda b, s: (0, 0, 0)),
                  pl.BlockSpec((1, LANES), c2),
                  pl.BlockSpec((tm, tm), c2)],
        out_specs=[pl.BlockSpec((1, tm, D), tok),
                   pl.BlockSpec((tm * ROW_TILE, LANES), lambda b, s: (b * nst + s, 0)),
                   pl.BlockSpec((1, tm, ROUTE_LANES), tok),
                   pl.BlockSpec((1, tm, ROUTE_LANES), tok),
                   pl.BlockSpec((1, tm, ROUTE_LANES), tok),
                   pl.BlockSpec((1, LANES), c2)],
        out_shape=[jax.ShapeDtypeStruct((B, S, D), F32),
                   jax.ShapeDtypeStruct((B * S * ROW_TILE, LANES), F32),
                   jax.ShapeDtypeStruct((B, S, ROUTE_LANES), F32),
                   jax.ShapeDtypeStruct((B, S, ROUTE_LANES), jnp.int32),
                   jax.ShapeDtypeStruct((B, S, ROUTE_LANES), jnp.int32),
                   jax.ShapeDtypeStruct((1, LANES), F32)],
        scratch_shapes=[pltpu.VMEM((1, LANES), F32)],
        compiler_params=pltpu.CompilerParams(
            dimension_semantics=("arbitrary", "arbitrary"), vmem_limit_bytes=VMEM_LIMIT),
        name="merge",
    )(x, ya, yb, mod, wgate, wa, wb, wo, lng, lnb, wr, br, tri)


GATHER_UNROLL = 8


def _store_token_tiles(ref, x):
    n = x.shape[0]
    for c in range(ROW_TILE):
        ref[pl.ds(c, n, stride=ROW_TILE), :] = x[:, LANES * c:LANES * c + LANES]


def _token_tile_cols(ref, row0, n, c):
    return ref[pl.ds(row0 * ROW_TILE + c, n, stride=ROW_TILE), :]


def _row_copy(src_hbm, row, buf, r, sem):
    src = src_hbm.at[pl.ds(pl.multiple_of(row * ROW_TILE, ROW_TILE), ROW_TILE), :]
    return pltpu.make_async_copy(src, buf.at[pl.ds(r * ROW_TILE, ROW_TILE), :], sem)


def _issue_rows(idx_ref, src_hbm, buf, sem):
    def issue(g, carry):
        for u in range(GATHER_UNROLL):
            r = g * GATHER_UNROLL + u
            src = src_hbm.at[pl.ds(pl.multiple_of(idx_ref[0, 0, r] * ROW_TILE, ROW_TILE),
                                   ROW_TILE), :]
            dst = buf.at[pl.ds(pl.multiple_of(r * ROW_TILE, ROW_TILE), ROW_TILE), :]
            pltpu.make_async_copy(src, dst, sem).start()
        return carry
    lax.fori_loop(0, buf.shape[0] // (ROW_TILE * GATHER_UNROLL), issue, 0)


def _wait_rows(src_hbm, buf, sem):
    pltpu.make_async_copy(src_hbm.at[pl.ds(0, buf.shape[0]), :], buf, sem).wait()


def _gather_step(step, n_steps, idx_ref, idx_next_ref, src_hbm, bufs, sems, compute,
                 row_priority):
    @pl.when(step == 0)
    def _():
        _issue_rows(idx_ref, src_hbm, bufs[0], sems.at[0])

    for par in range(2):
        @pl.when((step < n_steps) & (step % 2 == par))
        def _():
            nxt, cur = bufs[1 - par], bufs[par]
            _wait_rows(src_hbm, cur, sems.at[par])
            for r in range(nxt.shape[0] // ROW_TILE):
                _row_copy(src_hbm, idx_next_ref[0, 0, r], nxt, r,
                          sems.at[1 - par]).start(priority=row_priority(r))
            compute(cur)

            @pl.when(step == n_steps - 1)
            def _():
                _wait_rows(src_hbm, nxt, sems.at[1 - par])


def _moe_kernel(be_ref, nu_ref, tok_ref, tok_next_ref, u2_hbm, wup_ref, bup_ref, wdn_ref,
                bdn_ref, o_ref, buf0, buf1, wup_bf, wdn_bf, sems):
    i = pl.program_id(0)
    n_used = nu_ref[0]
    nb = o_ref.shape[0] // ROW_TILE

    @pl.when((i < n_used) & ((i == 0) | (be_ref[i] != be_ref[jnp.maximum(i - 1, 0)])))
    def _():
        wup_bf[...] = wup_ref[0].astype(BF16)
        wdn_bf[...] = wdn_ref[0].astype(BF16)

    def compute(buf):
        xb = jnp.concatenate([_token_tile_cols(buf, 0, nb, c).astype(BF16)
                              for c in range(ROW_TILE)], axis=1)
        h = _dot(xb, wup_bf[...]) + bup_ref[0]
        h_glu = jnp.minimum(h[:, :D], SWIGLU_LIMIT)
        h_lin = jnp.clip(h[:, D:], -SWIGLU_LIMIT, SWIGLU_LIMIT)
        act = h_glu * _sigmoid(SWIGLU_ALPHA * h_glu) * (h_lin + 1.0)
        _store_token_tiles(o_ref, _dot(act.astype(BF16), wdn_bf[...]) + bdn_ref[0])

    _gather_step(i, n_used, tok_ref, tok_next_ref, u2_hbm, (buf0, buf1), sems, compute,
                 lambda r: r % 2)

    @pl.when(i >= n_used)
    def _():
        o_ref[...] = jnp.zeros_like(o_ref)


def _moe(block_expert, n_used, row_tok, u2, wup, bup, wdn, bdn):
    nblk = block_expert.shape[0]
    nb = MOE_BLOCK
    ex = lambda i, be, nu: (be[i], 0, 0)
    grid_spec = pltpu.PrefetchScalarGridSpec(
        num_scalar_prefetch=2,
        grid=(nblk,),
        in_specs=[pl.BlockSpec((1, 1, nb), lambda i, be, nu: (i, 0, 0),
                               memory_space=pltpu.SMEM),
                  pl.BlockSpec((1, 1, nb),
                               lambda i, be, nu: (jnp.maximum(jnp.minimum(i + 1, nu[0] - 1), 0), 0, 0),
                               memory_space=pltpu.SMEM),
                  pl.BlockSpec(memory_space=pl.ANY),
                  pl.BlockSpec((1, D, 2 * D), ex),
                  pl.BlockSpec((1, 1, 2 * D), ex),
                  pl.BlockSpec((1, D, D), ex),
                  pl.BlockSpec((1, 1, D), ex)],
        out_specs=pl.BlockSpec((nb * ROW_TILE, LANES), lambda i, be, nu: (i, 0)),
        scratch_shapes=[pltpu.VMEM((nb * ROW_TILE, LANES), F32),
                        pltpu.VMEM((nb * ROW_TILE, LANES), F32),
                        pltpu.VMEM((D, 2 * D), BF16), pltpu.VMEM((D, D), BF16),
                        pltpu.SemaphoreType.DMA((2,))],
    )
    return pl.pallas_call(
        _moe_kernel,
        grid_spec=grid_spec,
        out_shape=jax.ShapeDtypeStruct((nblk * nb * ROW_TILE, LANES), F32),
        compiler_params=pltpu.CompilerParams(
            dimension_semantics=("arbitrary",), vmem_limit_bytes=VMEM_LIMIT_MOE),
        name="moe",
    )(block_expert, n_used, row_tok, row_tok, u2, wup, bup, wdn, bdn)


def _final_kernel(dest_ref, dest_next_ref, rows_hbm, x1_ref, gate_ref, g2_ref, lng_ref,
                  lnb_ref, o_ref, buf0, buf1, sems):
    tm = x1_ref.shape[0]

    def compute(buf):
        gates = gate_ref[...]
        lane = lax.broadcasted_iota(jnp.int32, gates.shape, 1)
        g = [_lane_col(gates, j, lane) for j in range(TOP_K)]
        cols = []
        for c in range(ROW_TILE):
            acc = g[0] * _token_tile_cols(buf, 0, tm, c)
            for j in range(1, TOP_K):
                acc = acc + g[j] * _token_tile_cols(buf, tm * j, tm, c)
            cols.append(acc)
        ffn = jnp.concatenate(cols, axis=1)
        z = ALPHA * x1_ref[...] + (1.0 + g2_ref[0]) * ffn
        o_ref[...] = _ln(z) * lng_ref[...] + lnb_ref[...]

    _gather_step(pl.program_id(0), pl.num_programs(0), dest_ref, dest_next_ref,
                 rows_hbm, (buf0, buf1), sems, compute, lambda r: r % 2)


def _final(dest, rows, x1, gates, g2, lng, lnb, tiles_per_seq):
    T = x1.shape[0]
    tm = TM_FINAL
    nt = T // tm
    tok = lambda i: (i, 0)
    c2 = lambda i: (0, 0)
    n = TOP_K * tm
    return pl.pallas_call(
        _final_kernel,
        grid=(nt,),
        in_specs=[pl.BlockSpec((1, 1, n), lambda i: (i, 0, 0), memory_space=pltpu.SMEM),
                  pl.BlockSpec((1, 1, n), lambda i: (jnp.minimum(i + 1, nt - 1), 0, 0),
                               memory_space=pltpu.SMEM),
                  pl.BlockSpec(memory_space=pl.ANY),
                  pl.BlockSpec((tm, D), tok),
                  pl.BlockSpec((tm, ROUTE_LANES), tok),
                  pl.BlockSpec((1, 1, D), lambda i: (i // tiles_per_seq, 0, 0)),
                  pl.BlockSpec((1, D), c2),
                  pl.BlockSpec((1, D), c2)],
        out_specs=pl.BlockSpec((tm, D), tok),
        out_shape=jax.ShapeDtypeStruct((T, D), F32),
        scratch_shapes=[pltpu.VMEM((n * ROW_TILE, LANES), F32),
                        pltpu.VMEM((n * ROW_TILE, LANES), F32),
                        pltpu.SemaphoreType.DMA((2,))],
        compiler_params=pltpu.CompilerParams(
            dimension_semantics=("arbitrary",), vmem_limit_bytes=VMEM_LIMIT),
        name="final",
    )(dest, dest, rows, x1, gates, g2, lng, lnb)


def _routing(top_idx, rank, counts):
    T = top_idx.shape[0]
    A = T * TOP_K
    nb = MOE_BLOCK
    nblk = A // nb + N_EXP
    e_flat = top_idx.reshape(A)
    rank = rank.reshape(A)
    padded = (counts + nb - 1) // nb * nb
    padded_end = jnp.cumsum(padded)
    padded_start = padded_end - padded
    dest = padded_start[e_flat] + rank
    blk_row0 = jnp.arange(nblk, dtype=jnp.int32) * nb
    block_expert = jnp.minimum(
        jnp.sum((padded_end[None, :] <= blk_row0[:, None]).astype(jnp.int32), axis=1),
        N_EXP - 1)
    order = jnp.argsort(e_flat, stable=True).astype(jnp.int32)
    starts = jnp.cumsum(counts) - counts
    local = (blk_row0 - padded_start[block_expert])[:, None] + jnp.arange(nb, dtype=jnp.int32)
    valid = local < counts[block_expert][:, None]
    src = jnp.clip(starts[block_expert][:, None] + local, 0, A - 1)
    row_tok = jnp.where(valid, order[src] // TOP_K, 0)
    n_used = (padded_end[-1] // nb).astype(jnp.int32).reshape(1)
    last_e = block_expert[jnp.maximum(n_used[0] - 1, 0)]
    block_expert = jnp.where(jnp.arange(nblk) < n_used[0], block_expert, last_e)
    return (block_expert, n_used, row_tok.astype(jnp.int32).reshape(nblk, 1, nb),
            dest.reshape(T, TOP_K))


def kernel(x, c, w_ada, b_ada, w_in, fox_f_bias, w_gla_gate, b_gla_gate, gla_norm_g,
           w_branch_a, w_branch_b, w_out, ln1_g, ln1_b, w_router, b_router,
           w_up, b_up, w_down, b_down, ln2_g, ln2_b):
    B, S, _ = x.shape
    T = B * S
    l = 0

    c_pad = jnp.zeros((8, D), F32).at[:B].set(c)
    mod = _ada(c_pad, w_ada[l], b_ada[l][None, :])[:B]
    mod6 = mod.reshape(B, 6, D)
    mod8 = jnp.concatenate([mod6, jnp.zeros((B, 2, D), F32)], axis=1)
    sh1, sc1 = mod6[:, 0:1], mod6[:, 1:2]
    g2 = mod6[:, 5:6]

    w = w_in[l]
    o = 0
    parts = []
    for width in (FOX_W, FOX_W, FOX_W, FOX_H, GLA_KW, GLA_KW, GLA_VW, GLA_VW, GLA_RANK, D, D):
        parts.append(w[:, o:o + width])
        o += width
    wq, wk, wv, wff, wgq, wgk, wgv, wgr, wglr, wga, wgb = parts

    wfox = jnp.concatenate([wq * (FOX_DH ** -0.5 * LOG2E), wk, wv], axis=1).astype(BF16)
    wgla = jnp.concatenate([wgq * GLA_DK ** -0.5, wgk, wgv, wgr], axis=1).astype(BF16)
    wsm = jnp.zeros((D, 2 * LANES), F32).at[:, :FOX_H].set(wff)
    wsm = wsm.at[:, LANES:LANES + GLA_RANK].set(wglr).astype(BF16)
    fb = jnp.zeros((1, LANES), F32).at[0, :FOX_H].set(fox_f_bias[l])
    tri = jnp.asarray(np.tril(np.ones((TM_IN, TM_IN), np.float32)), dtype=BF16)

    qa, ka, va, gq, gk, gv, gr, glr = _inproj(x, sh1, sc1, wfox, wgla, wsm, fb, tri)

    ya = _fox(qa, ka, va)

    cm3, masks = _gla_tables()
    yb = _gla(gq, gk, gv, gr, glr, w_gla_gate[l], b_gla_gate[l][None, :],
              gla_norm_g[l][None, :], jnp.asarray(cm3, dtype=BF16), jnp.asarray(masks))

    wgate = jnp.concatenate([wga, wgb], axis=1).astype(BF16)
    wr = jnp.zeros((D, LANES), F32).at[:, :N_EXP].set(w_router[l])
    wr_hi = wr.astype(BF16)
    wr = jnp.stack([wr_hi, (wr - wr_hi.astype(F32)).astype(BF16)])
    br = jnp.full((1, LANES), NEG, F32).at[0, :N_EXP].set(b_router[l])
    x1, u2, topv, topi, rank, counts = _merge(
        x, ya, yb, mod8, wgate, w_branch_a[l].astype(BF16), w_branch_b[l].astype(BF16),
        w_out[l].astype(BF16), ln1_g[l][None, :], ln1_b[l][None, :], wr, br)

    block_expert, n_used, row_tok, dest = _routing(
        topi.reshape(T, ROUTE_LANES)[:, :TOP_K], rank.reshape(T, ROUTE_LANES)[:, :TOP_K],
        counts[0, :N_EXP].astype(jnp.int32))
    rows = _moe(block_expert, n_used, row_tok, u2,
                w_up[l], b_up[l][:, None, :], w_down[l], b_down[l][:, None, :])

    nt = T // TM_FINAL
    dest_t = dest.reshape(nt, TM_FINAL, TOP_K).transpose(0, 2, 1).reshape(nt, 1, TOP_K * TM_FINAL)
    out = _final(dest_t, rows, x1.reshape(T, D), topv.reshape(T, ROUTE_LANES), g2,
                 ln2_g[l][None, :], ln2_b[l][None, :], S // TM_FINAL)
    return out.reshape(B, S, D)
```

```python
import functools

import numpy as np
import jax
import jax.numpy as jnp
from jax import lax
from jax.experimental import pallas as pl
from jax.experimental.pallas import tpu as pltpu

F32 = jnp.float32
BF16 = jnp.bfloat16
HIGHEST = lax.Precision.HIGHEST

D = 1024
FOX_H = 8
FOX_DH = 64
FOX_W = FOX_H * FOX_DH
GLA_H = 4
GLA_DK = 128
GLA_DV = 256
GLA_KW = GLA_H * GLA_DK
GLA_VW = GLA_H * GLA_DV
GLA_RANK = 16
GLA_TAU = 16.0
N_EXP = 32
TOP_K = 4
SWIGLU_LIMIT = 7.0
SWIGLU_ALPHA = 1.702
EPS = 1e-5
DEPTH = 1
ALPHA = (2 * DEPTH) ** 0.25
LANES = 128
ROW_TILE = 8
ROUTE_LANES = 8

GLA_CHUNK = 64
GLA_BLOCK = 512
GLA_UNROLL = 4
MOE_BLOCK = 512
TM_IN = 512
TM_MERGE = 512
TM_FINAL = 256
TQ = 512
VMEM_LIMIT = 56 * 1024 * 1024
VMEM_LIMIT_MOE = 60 * 1024 * 1024

NEG = -1e30
LOG2E = 1.4426950408889634


def _ln(x):
    mu = jnp.mean(x, axis=-1, keepdims=True)
    xc = x - mu
    var = jnp.mean(xc * xc, axis=-1, keepdims=True)
    return xc * lax.rsqrt(var + EPS)


def _sigmoid(x):
    return 1.0 / (1.0 + jnp.exp(-x))


def _log_sigmoid(x):
    return jnp.minimum(x, 0.0) - jnp.log(1.0 + jnp.exp(-jnp.abs(x)))


def _split3(x):
    hi = x.astype(BF16)
    r = x - hi.astype(F32)
    mid = r.astype(BF16)
    lo = (r - mid.astype(F32)).astype(BF16)
    return hi, mid, lo


def _lane_col(x, idx, lane):
    return jnp.sum(jnp.where(lane == idx, x, 0.0), axis=1, keepdims=True)


def _dot(a, b):
    return jnp.dot(a, b, preferred_element_type=F32)


def _dot_nt(a, b):
    return lax.dot_general(a, b, (((1,), (1,)), ((), ())), preferred_element_type=F32)


def _dot_tn(a, b):
    return lax.dot_general(a, b, (((0,), (0,)), ((), ())), preferred_element_type=F32)


def _ada_kernel(c_ref, w_ref, b_ref, o_ref):
    c = c_ref[...]
    ca = c * _sigmoid(c)
    o_ref[...] = jnp.dot(ca, w_ref[...], precision=HIGHEST,
                         preferred_element_type=F32) + b_ref[...]


def _ada(c_pad, w, b):
    n = w.shape[1]
    tn = 1536
    return pl.pallas_call(
        _ada_kernel,
        grid=(n // tn,),
        in_specs=[pl.BlockSpec((8, D), lambda j: (0, 0)),
                  pl.BlockSpec((D, tn), lambda j: (0, j)),
                  pl.BlockSpec((1, tn), lambda j: (0, j))],
        out_specs=pl.BlockSpec((8, tn), lambda j: (0, j)),
        out_shape=jax.ShapeDtypeStruct((8, n), F32),
        compiler_params=pltpu.CompilerParams(
            dimension_semantics=("arbitrary",), vmem_limit_bytes=VMEM_LIMIT),
        name="ada",
    )(c_pad, w, b)


def _inproj_kernel(x_ref, sh_ref, sc_ref, wfox_ref, wgla_ref, wsm_ref, fb_ref, tri_ref,
                   qa_ref, ka_ref, vt_ref, gq_ref, gk_ref, gv_ref, gr_ref, glr_ref,
                   carry_ref):
    tm = x_ref.shape[1]

    @pl.when(pl.program_id(1) == 0)
    def _():
        carry_ref[...] = jnp.zeros_like(carry_ref)

    u = _ln(x_ref[0]) * (1.0 + sc_ref[0]) + sh_ref[0]
    ub = u.astype(BF16)

    sm = _dot(ub, wsm_ref[...])
    glr_ref[0] = sm[:, LANES:LANES + GLA_RANK]
    lane = lax.broadcasted_iota(jnp.int32, (tm, LANES), 1)
    lf = jnp.where(lane < FOX_H, _log_sigmoid(sm[:, :LANES] + fb_ref[...]), 0.0)
    hi, mid, lo = _split3(lf)
    tri = tri_ref[...]
    cum = _dot(tri, hi) + _dot(tri, mid) + _dot(tri, lo) + carry_ref[...]
    carry_ref[...] = cum[tm - 1:tm, :]
    chi, cmid, clo = _split3(cum * LOG2E)
    chi, cmid, clo = chi.astype(F32), cmid.astype(F32), clo.astype(F32)

    ex_v = jnp.where(lane == FOX_DH, 1.0, 0.0)
    is_q1 = (lane >= FOX_DH + 3) & (lane < FOX_DH + 6)
    is_k1 = (lane >= FOX_DH) & (lane < FOX_DH + 3)
    low = lane < FOX_DH
    q_all = _dot(ub, wfox_ref[:, 0:FOX_W])
    k_all = _dot(ub, wfox_ref[:, FOX_W:2 * FOX_W])
    v_all = _dot(ub, wfox_ref[:, 2 * FOX_W:3 * FOX_W])
    for hp in range(FOX_H // 2):
        pair = slice(LANES * hp, LANES * hp + LANES)
        for hh in range(2):
            h = 2 * hp + hh
            c0 = _lane_col(chi, h, lane)
            c1 = _lane_col(cmid, h, lane)
            c2 = _lane_col(clo, h, lane)
            ex_q = jnp.where(lane == FOX_DH, c0,
                             jnp.where(lane == FOX_DH + 1, c1,
                                       jnp.where(lane == FOX_DH + 2, c2,
                                                 jnp.where(is_q1, 1.0, 0.0))))
            ex_k = jnp.where(lane == FOX_DH + 3, -c0,
                             jnp.where(lane == FOX_DH + 4, -c1,
                                       jnp.where(lane == FOX_DH + 5, -c2,
                                                 jnp.where(is_k1, 1.0, 0.0))))
            qh, kh, vh = q_all[:, pair], k_all[:, pair], v_all[:, pair]
            if hh == 1:
                qh, kh, vh = (pltpu.roll(t, FOX_DH, 1) for t in (qh, kh, vh))
            qa_ref[0, h] = jnp.where(low, qh, ex_q).astype(BF16)
            ka_ref[0, h] = jnp.where(low, kh, ex_k).astype(BF16)
            vt_ref[0, h, 0] = jnp.where(low, vh, ex_v).T.astype(BF16)

    for j in range(GLA_KW // 256):
        gq_ref[0, :, 256 * j:256 * j + 256] = _dot(
            ub, wgla_ref[:, 256 * j:256 * j + 256]).astype(BF16)
        gk_ref[0, :, 256 * j:256 * j + 256] = _dot(
            ub, wgla_ref[:, GLA_KW + 256 * j:GLA_KW + 256 * j + 256]).astype(BF16)
    for j in range(GLA_VW // 256):
        o = 2 * GLA_KW + 256 * j
        gv_ref[0, :, 256 * j:256 * j + 256] = _dot(ub, wgla_ref[:, o:o + 256]).astype(BF16)
        o = 2 * GLA_KW + GLA_VW + 256 * j
        gr_ref[0, :, 256 * j:256 * j + 256] = _dot(ub, wgla_ref[:, o:o + 256]).astype(BF16)


def _inproj(x, sh1, sc1, wfox, wgla, wsm, fb, tri):
    B, S, _ = x.shape
    tm = TM_IN
    const = lambda b, s: (0, 0)
    tok = lambda b, s: (b, s, 0)
    head = lambda b, s: (b, 0, s, 0)
    vec = lambda b, s: (b, 0, 0)
    hs = jax.ShapeDtypeStruct((B, FOX_H, S, LANES), BF16)
    per_q = TQ // tm
    return pl.pallas_call(
        _inproj_kernel,
        grid=(B, S // tm),
        in_specs=[pl.BlockSpec((1, tm, D), tok),
                  pl.BlockSpec((1, 1, D), vec),
                  pl.BlockSpec((1, 1, D), vec),
                  pl.BlockSpec(wfox.shape, const),
                  pl.BlockSpec(wgla.shape, const),
                  pl.BlockSpec(wsm.shape, const),
                  pl.BlockSpec((1, LANES), const),
                  pl.BlockSpec((tm, tm), const)],
        out_specs=[pl.BlockSpec((1, FOX_H, tm, LANES), head),
                   pl.BlockSpec((1, FOX_H, tm, LANES), head),
                   pl.BlockSpec((1, FOX_H, 1, LANES, tm),
                                lambda b, s: (b, 0, s // per_q, 0, s % per_q)),
                   pl.BlockSpec((1, tm, GLA_KW), tok),
                   pl.BlockSpec((1, tm, GLA_KW), tok),
                   pl.BlockSpec((1, tm, GLA_VW), tok),
                   pl.BlockSpec((1, tm, GLA_VW), tok),
                   pl.BlockSpec((1, tm, GLA_RANK), tok)],
        out_shape=[hs, hs, jax.ShapeDtypeStruct((B, FOX_H, S // TQ, LANES, TQ), BF16),
                   jax.ShapeDtypeStruct((B, S, GLA_KW), BF16),
                   jax.ShapeDtypeStruct((B, S, GLA_KW), BF16),
                   jax.ShapeDtypeStruct((B, S, GLA_VW), BF16),
                   jax.ShapeDtypeStruct((B, S, GLA_VW), BF16),
                   jax.ShapeDtypeStruct((B, S, GLA_RANK), F32)],
        scratch_shapes=[pltpu.VMEM((1, LANES), F32)],
        compiler_params=pltpu.CompilerParams(
            dimension_semantics=("parallel", "arbitrary"), vmem_limit_bytes=VMEM_LIMIT),
        name="inproj",
    )(x, sh1, sc1, wfox, wgla, wsm, fb, tri)


def _fox_kernel(q_ref, k_ref, vt_ref, o_ref, m_ref, acc_ref, sa_ref, sb_ref):
    qi = pl.program_id(2)
    tq = q_ref.shape[2]
    m_ref[...] = jnp.full(m_ref.shape, -jnp.inf, F32)
    acc_ref[...] = jnp.zeros_like(acc_ref)

    def scores(j, s_ref):
        k0 = pl.multiple_of(j * tq, tq)
        for hh in range(2):
            s_ref[hh] = _dot_nt(k_ref[0, hh, pl.ds(k0, tq), :], q_ref[0, hh])

    def update(j, s_ref, masked):
        for hh in range(2):
            s_t = s_ref[hh]
            if masked:
                key = lax.broadcasted_iota(jnp.int32, (tq, tq), 0)
                qry = lax.broadcasted_iota(jnp.int32, (tq, tq), 1)
                s_t = jnp.where(key <= qry, s_t, -jnp.inf)
            m_old = m_ref[hh]
            m_new = jnp.maximum(m_old, jnp.max(s_t, axis=0, keepdims=True))
            p_t = jnp.exp2(s_t - m_new).astype(BF16)
            acc_ref[hh] = (jnp.exp2(m_old - m_new) * acc_ref[hh]
                           + _dot(vt_ref[0, hh, j], p_t))
            m_ref[hh] = m_new

    scores(0, sa_ref)

    def body(t, carry):
        j = 2 * t
        scores(j + 1, sb_ref)
        update(j, sa_ref, False)
        scores(j + 2, sa_ref)
        update(j + 1, sb_ref, False)
        return carry

    lax.fori_loop(0, qi // 2, body, 0)

    @pl.when(qi % 2 == 0)
    def _():
        update(qi, sa_ref, True)

    @pl.when(qi % 2 == 1)
    def _():
        scores(qi, sb_ref)
        update(qi - 1, sa_ref, False)
        update(qi, sb_ref, True)

    outs = []
    for hh in range(2):
        acc = acc_ref[hh]
        outs.append((acc / acc[FOX_DH:FOX_DH + 1, :])[:FOX_DH])
    o_ref[0] = jnp.concatenate(outs, axis=0).T.astype(BF16)


def _fox(qa, ka, vt):
    B, H, S, _ = qa.shape
    tq = TQ
    return pl.pallas_call(
        _fox_kernel,
        grid=(B, H // 2, S // tq),
        in_specs=[pl.BlockSpec((1, 2, tq, LANES), lambda b, h, q: (b, h, q, 0)),
                  pl.BlockSpec((1, 2, S, LANES), lambda b, h, q: (b, h, 0, 0)),
                  pl.BlockSpec((1, 2, S // tq, LANES, tq), lambda b, h, q: (b, h, 0, 0, 0))],
        out_specs=pl.BlockSpec((1, tq, LANES), lambda b, h, q: (b, q, h)),
        out_shape=jax.ShapeDtypeStruct((B, S, FOX_W), BF16),
        scratch_shapes=[pltpu.VMEM((2, 1, tq), F32), pltpu.VMEM((2, LANES, tq), F32),
                        pltpu.VMEM((2, tq, tq), F32), pltpu.VMEM((2, tq, tq), F32)],
        compiler_params=pltpu.CompilerParams(
            dimension_semantics=("parallel", "parallel", "arbitrary"),
            vmem_limit_bytes=VMEM_LIMIT),
        name="fox",
    )(qa, ka, vt)


def _gla_tables():
    C = GLA_CHUNK
    t = np.arange(C)[:, None]
    j = np.arange(C)[None, :]
    slabs = [(j <= t), (j > t)]
    masks = [np.eye(C, dtype=bool)]
    m = C // 2
    while m >= 1:
        g0 = (t // (2 * m)) * (2 * m)
        piv = g0 + m - 1
        upper = (t - g0) >= m
        slabs.append(np.where(upper, (j > piv) & (j <= t), (j > t) & (j <= piv)))
        s = np.arange(C)[None, :]
        masks.append(upper & ((s // (2 * m)) == (t // (2 * m))) & ((s % (2 * m)) < m))
        m //= 2
    cm = np.concatenate(slabs, axis=0).astype(np.float32)
    cm3 = np.concatenate([cm, cm, cm], axis=1)
    return cm3, np.stack(masks).astype(np.float32)


def _gla_kernel(gq_ref, gk_ref, gv_ref, gr_ref, glr_ref, wg_ref, bg_ref, ng_ref,
                cm_ref, mask_ref, o_ref, st_ref, la3_ref):
    C = GLA_CHUNK
    L = gq_ref.shape[1]
    n_lvl = mask_ref.shape[0] - 1

    @pl.when(pl.program_id(1) == 0)
    def _():
        st_ref[...] = jnp.zeros_like(st_ref)

    xg = jnp.dot(glr_ref[0], wg_ref[...], precision=HIGHEST,
                 preferred_element_type=F32) + bg_ref[...]
    la = _log_sigmoid(xg) * (1.0 / GLA_TAU)
    hi, mid, lo = _split3(la)
    for c in range(L // C):
        la3_ref[3 * C * c:3 * C * c + C, :] = hi[C * c:C * c + C]
        la3_ref[3 * C * c + C:3 * C * c + 2 * C, :] = mid[C * c:C * c + C]
        la3_ref[3 * C * c + 2 * C:3 * C * c + 3 * C, :] = lo[C * c:C * c + C]

    def chunk_group(gi, carry):
        pairs = [(u, h) for u in range(GLA_UNROLL) for h in range(GLA_H)]
        r0 = [pl.multiple_of((gi * GLA_UNROLL + u) * C, C) for u in range(GLA_UNROLL)]
        ks = [slice(GLA_DK * h, GLA_DK * h + GLA_DK) for h in range(GLA_H)]
        vs = [slice(GLA_DV * h, GLA_DV * h + GLA_DV) for h in range(GLA_H)]

        w, q, k = [], [], []
        for u in range(GLA_UNROLL):
            a0 = pl.multiple_of((gi * GLA_UNROLL + u) * 3 * C, 3 * C)
            w.append(jnp.exp(_dot(cm_ref[...], la3_ref[pl.ds(a0, 3 * C), :])))
            q.append(gq_ref[0, pl.ds(r0[u], C), :].astype(F32))
            k.append(gk_ref[0, pl.ds(r0[u], C), :].astype(F32))

        q_in, k_out, dec, q_lv, k_lv, v = {}, {}, {}, {}, {}, {}
        for u, h in pairs:
            wh, qh, kh = w[u][:, ks[h]], q[u][:, ks[h]], k[u][:, ks[h]]
            q_in[u, h] = (qh * wh[0:C]).astype(BF16)
            k_out[u, h] = (kh * wh[C:2 * C]).astype(BF16)
            dec[u, h] = wh[C - 1:C, :]
            q_lv[u, h] = [qh.astype(BF16)] + [(qh * wh[(2 + lv) * C:(3 + lv) * C]).astype(BF16)
                                              for lv in range(n_lvl)]
            k_lv[u, h] = [kh.astype(BF16)] + [(kh * wh[(2 + lv) * C:(3 + lv) * C]).astype(BF16)
                                              for lv in range(n_lvl)]
            v[u, h] = gv_ref[0, pl.ds(r0[u], C), vs[h]]

        upd = {p: _dot_tn(v[p], k_out[p]) for p in pairs}
        sc_parts = {p: [_dot_nt(a, b) for a, b in zip(q_lv[p], k_lv[p])] for p in pairs}

        inter = {}
        for h in range(GLA_H):
            st = st_ref[h]
            for u in range(GLA_UNROLL):
                inter[u, h] = _dot_nt(q_in[u, h], st.astype(BF16))
                st = st * dec[u, h] + upd[u, h]
            st_ref[h] = st

        sc = {}
        for p in pairs:
            acc = mask_ref[0] * sc_parts[p][0]
            for lv in range(n_lvl):
                acc = acc + mask_ref[1 + lv] * sc_parts[p][1 + lv]
            sc[p] = acc.astype(BF16)
        intra = {p: _dot(sc[p], v[p]) for p in pairs}
        for u, h in pairs:
            o = inter[u, h] + intra[u, h]
            y = o * lax.rsqrt(jnp.mean(o * o, axis=1, keepdims=True) + EPS)
            g = gr_ref[0, pl.ds(r0[u], C), vs[h]].astype(F32)
            o_ref[0, pl.ds(r0[u], C), vs[h]] = (
                y * ng_ref[:, vs[h]] * (g * _sigmoid(g))).astype(BF16)
        return carry

    lax.fori_loop(0, L // (C * GLA_UNROLL), chunk_group, 0)


def _gla(gq, gk, gv, gr, glr, wg, bg, ng, cm3, masks):
    B, S, _ = gq.shape
    L = GLA_BLOCK
    tok = lambda b, s: (b, s, 0)
    c2 = lambda b, s: (0, 0)
    return pl.pallas_call(
        _gla_kernel,
        grid=(B, S // L),
        in_specs=[pl.BlockSpec((1, L, GLA_KW), tok),
                  pl.BlockSpec((1, L, GLA_KW), tok),
                  pl.BlockSpec((1, L, GLA_VW), tok),
                  pl.BlockSpec((1, L, GLA_VW), tok),
                  pl.BlockSpec((1, L, GLA_RANK), tok),
                  pl.BlockSpec(wg.shape, c2),
                  pl.BlockSpec(bg.shape, c2),
                  pl.BlockSpec(ng.shape, c2),
                  pl.BlockSpec(cm3.shape, c2),
                  pl.BlockSpec(masks.shape, lambda b, s: (0, 0, 0))],
        out_specs=pl.BlockSpec((1, L, GLA_VW), tok),
        out_shape=jax.ShapeDtypeStruct((B, S, GLA_VW), BF16),
        scratch_shapes=[pltpu.VMEM((GLA_H, GLA_DV, GLA_DK), F32),
                        pltpu.VMEM((3 * L, GLA_KW), BF16)],
        compiler_params=pltpu.CompilerParams(
            dimension_semantics=("parallel", "arbitrary"), vmem_limit_bytes=VMEM_LIMIT),
        name="gla",
    )(gq, gk, gv, gr, glr, wg, bg, ng, cm3, masks)


def _merge_kernel(x_ref, ya_ref, yb_ref, mod_ref, wgate_ref, wa_ref, wb_ref, wo_ref,
                  lng_ref, lnb_ref, wr_ref, br_ref, tri_ref,
                  x1_ref, u2_ref, topv_ref, topi_ref, rank_ref, cnt_ref, carry_ref):
    tm = x_ref.shape[1]
    x = x_ref[0]
    mod = mod_ref[0]
    sh1, sc1, g1 = mod[0:1], mod[1:2], mod[2:3]
    sh2, sc2 = mod[3:4], mod[4:5]
    ub = (_ln(x) * (1.0 + sc1) + sh1).astype(BF16)
    br_a = _dot(ya_ref[0], wa_ref[...])
    br_b = _dot(yb_ref[0], wb_ref[...])
    merged = (_sigmoid(_dot(ub, wgate_ref[:, :D])) * br_a
              + _sigmoid(_dot(ub, wgate_ref[:, D:])) * br_b)
    mix = _dot(merged.astype(BF16), wo_ref[...])
    x1 = _ln(ALPHA * x + (1.0 + g1) * mix) * lng_ref[...] + lnb_ref[...]
    x1_ref[0] = x1
    u2 = _ln(x1) * (1.0 + sc2) + sh2
    _store_token_tiles(u2_ref, u2)

    u_hi = u2.astype(BF16)
    u_lo = (u2 - u_hi.astype(F32)).astype(BF16)
    logits = (_dot(u_hi, wr_ref[0]) + _dot(u_lo, wr_ref[0]) + _dot(u_hi, wr_ref[1])
              + br_ref[...])
    lane = lax.broadcasted_iota(jnp.int32, (tm, LANES), 1)
    vals = jnp.zeros((tm, LANES), F32)
    idxs = jnp.zeros((tm, LANES), jnp.int32)
    picked = jnp.zeros((tm, LANES), F32)
    cur = logits
    ixs = []
    for k in range(TOP_K):
        mx = jnp.max(cur, axis=1, keepdims=True)
        ix = jnp.min(jnp.where(cur == mx, lane, LANES), axis=1, keepdims=True)
        vals = jnp.where(lane == k, mx, vals)
        idxs = jnp.where(lane == k, ix, idxs)
        picked = jnp.where(lane == ix, 1.0, picked)
        cur = jnp.where(lane == ix, -jnp.inf, cur)
        ixs.append(ix)
    v0 = jnp.max(jnp.where(lane < TOP_K, vals, -jnp.inf), axis=1, keepdims=True)
    e = jnp.where(lane < TOP_K, jnp.exp(vals - v0), 0.0)
    topv_ref[0] = (e / jnp.sum(e, axis=1, keepdims=True))[:, :ROUTE_LANES]
    topi_ref[0] = idxs[:, :ROUTE_LANES]

    @pl.when((pl.program_id(0) == 0) & (pl.program_id(1) == 0))
    def _():
        carry_ref[...] = jnp.zeros_like(carry_ref)

    before = _dot(tri_ref[...], picked.astype(BF16)) + carry_ref[...]
    ranks = jnp.zeros((tm, LANES), F32)
    for k in range(TOP_K):
        rk = jnp.sum(jnp.where(lane == ixs[k], before, 0.0), axis=1, keepdims=True)
        ranks = jnp.where(lane == k, rk, ranks)
    rank_ref[0] = ranks.astype(jnp.int32)[:, :ROUTE_LANES]
    total = carry_ref[...] + jnp.sum(picked, axis=0, keepdims=True)
    carry_ref[...] = total
    cnt_ref[...] = total


def _merge(x, ya, yb, mod, wgate, wa, wb, wo, lng, lnb, wr, br):
    B, S, _ = x.shape
    tm = TM_MERGE
    tri = jnp.asarray(np.tril(np.ones((tm, tm), np.float32), -1), dtype=BF16)
    tok = lambda b, s: (b, s, 0)
    c2 = lambda b, s: (0, 0)
    nst = S // tm
    return pl.pallas_call(
        _merge_kernel,
        grid=(B, nst),
        in_specs=[pl.BlockSpec((1, tm, D), tok),
                  pl.BlockSpec((1, tm, FOX_W), tok),
                  pl.BlockSpec((1, tm, GLA_VW), tok),
                  pl.BlockSpec((1, 8, D), lambda b, s: (b, 0, 0)),
                  pl.BlockSpec(wgate.shape, c2, pipeline_mode=pl.Buffered(1)),
                  pl.BlockSpec(wa.shape, c2, pipeline_mode=pl.Buffered(1)),
                  pl.BlockSpec(wb.shape, c2, pipeline_mode=pl.Buffered(1)),
                  pl.BlockSpec(wo.shape, c2, pipeline_mode=pl.Buffered(1)),
                  pl.BlockSpec((1, D), c2),
                  pl.BlockSpec((1, D), c2),
                  pl.BlockSpec(wr.shape, lambda b, s: (0, 0, 0)),
                  pl.BlockSpec((1, LANES), c2),
                  pl.BlockSpec((tm, tm), c2)],
        out_specs=[pl.BlockSpec((1, tm, D), tok),
                   pl.BlockSpec((tm * ROW_TILE, LANES), lambda b, s: (b * nst + s, 0)),
                   pl.BlockSpec((1, tm, ROUTE_LANES), tok),
                   pl.BlockSpec((1, tm, ROUTE_LANES), tok),
                   pl.BlockSpec((1, tm, ROUTE_LANES), tok),
                   pl.BlockSpec((1, LANES), c2)],
        out_shape=[jax.ShapeDtypeStruct((B, S, D), F32),
                   jax.ShapeDtypeStruct((B * S * ROW_TILE, LANES), F32),
                   jax.ShapeDtypeStruct((B, S, ROUTE_LANES), F32),
                   jax.ShapeDtypeStruct((B, S, ROUTE_LANES), jnp.int32),
                   jax.ShapeDtypeStruct((B, S, ROUTE_LANES), jnp.int32),
                   jax.ShapeDtypeStruct((1, LANES), F32)],
        scratch_shapes=[pltpu.VMEM((1, LANES), F32)],
        compiler_params=pltpu.CompilerParams(
            dimension_semantics=("arbitrary", "arbitrary"), vmem_limit_bytes=VMEM_LIMIT),
        name="merge",
    )(x, ya, yb, mod, wgate, wa, wb, wo, lng, lnb, wr, br, tri)


GATHER_UNROLL = 8


def _store_token_tiles(ref, x):
    n = x.shape[0]
    for c in range(ROW_TILE):
        ref[pl.ds(c, n, stride=ROW_TILE), :] = x[:, LANES * c:LANES * c + LANES]


def _token_tile_cols(ref, row0, n, c):
    return ref[pl.ds(row0 * ROW_TILE + c, n, stride=ROW_TILE), :]


def _row_copy(src_hbm, row, buf, r, sem):
    src = src_hbm.at[pl.ds(pl.multiple_of(row * ROW_TILE, ROW_TILE), ROW_TILE), :]
    return pltpu.make_async_copy(src, buf.at[pl.ds(r * ROW_TILE, ROW_TILE), :], sem)


def _issue_rows(idx_ref, src_hbm, buf, sem):
    def issue(g, carry):
        for u in range(GATHER_UNROLL):
            r = g * GATHER_UNROLL + u
            src = src_hbm.at[pl.ds(pl.multiple_of(idx_ref[0, 0, r] * ROW_TILE, ROW_TILE),
                                   ROW_TILE), :]
            dst = buf.at[pl.ds(pl.multiple_of(r * ROW_TILE, ROW_TILE), ROW_TILE), :]
            pltpu.make_async_copy(src, dst, sem).start()
        return carry
    lax.fori_loop(0, buf.shape[0] // (ROW_TILE * GATHER_UNROLL), issue, 0)


def _wait_rows(src_hbm, buf, sem):
    pltpu.make_async_copy(src_hbm.at[pl.ds(0, buf.shape[0]), :], buf, sem).wait()


def _gather_step(step, n_steps, idx_ref, idx_next_ref, src_hbm, bufs, sems, compute,
                 row_priority):
    @pl.when(step == 0)
    def _():
        _issue_rows(idx_ref, src_hbm, bufs[0], sems.at[0])

    for par in range(2):
        @pl.when((step < n_steps) & (step % 2 == par))
        def _():
            nxt, cur = bufs[1 - par], bufs[par]
            _wait_rows(src_hbm, cur, sems.at[par])
            for r in range(nxt.shape[0] // ROW_TILE):
                _row_copy(src_hbm, idx_next_ref[0, 0, r], nxt, r,
                          sems.at[1 - par]).start(priority=row_priority(r))
            compute(cur)

            @pl.when(step == n_steps - 1)
            def _():
                _wait_rows(src_hbm, nxt, sems.at[1 - par])


def _moe_kernel(be_ref, nu_ref, tok_ref, tok_next_ref, u2_hbm, wup_ref, bup_ref, wdn_ref,
                bdn_ref, o_ref, buf0, buf1, wup_bf, wdn_bf, sems):
    i = pl.program_id(0)
    n_used = nu_ref[0]
    nb = o_ref.shape[0] // ROW_TILE

    @pl.when((i < n_used) & ((i == 0) | (be_ref[i] != be_ref[jnp.maximum(i - 1, 0)])))
    def _():
        wup_bf[...] = wup_ref[0].astype(BF16)
        wdn_bf[...] = wdn_ref[0].astype(BF16)

    def compute(buf):
        xb = jnp.concatenate([_token_tile_cols(buf, 0, nb, c).astype(BF16)
                              for c in range(ROW_TILE)], axis=1)
        h = _dot(xb, wup_bf[...]) + bup_ref[0]
        h_glu = jnp.minimum(h[:, :D], SWIGLU_LIMIT)
        h_lin = jnp.clip(h[:, D:], -SWIGLU_LIMIT, SWIGLU_LIMIT)
        act = h_glu * _sigmoid(SWIGLU_ALPHA * h_glu) * (h_lin + 1.0)
        _store_token_tiles(o_ref, _dot(act.astype(BF16), wdn_bf[...]) + bdn_ref[0])

    _gather_step(i, n_used, tok_ref, tok_next_ref, u2_hbm, (buf0, buf1), sems, compute,
                 lambda r: r % 2)

    @pl.when(i >= n_used)
    def _():
        o_ref[...] = jnp.zeros_like(o_ref)


def _moe(block_expert, n_used, row_tok, u2, wup, bup, wdn, bdn):
    nblk = block_expert.shape[0]
    nb = MOE_BLOCK
    ex = lambda i, be, nu: (be[i], 0, 0)
    grid_spec = pltpu.PrefetchScalarGridSpec(
        num_scalar_prefetch=2,
        grid=(nblk,),
        in_specs=[pl.BlockSpec((1, 1, nb), lambda i, be, nu: (i, 0, 0),
                               memory_space=pltpu.SMEM),
                  pl.BlockSpec((1, 1, nb),
                               lambda i, be, nu: (jnp.maximum(jnp.minimum(i + 1, nu[0] - 1), 0), 0, 0),
                               memory_space=pltpu.SMEM),
                  pl.BlockSpec(memory_space=pl.ANY),
                  pl.BlockSpec((1, D, 2 * D), ex),
                  pl.BlockSpec((1, 1, 2 * D), ex),
                  pl.BlockSpec((1, D, D), ex),
                  pl.BlockSpec((1, 1, D), ex)],
        out_specs=pl.BlockSpec((nb * ROW_TILE, LANES), lambda i, be, nu: (i, 0)),
        scratch_shapes=[pltpu.VMEM((nb * ROW_TILE, LANES), F32),
                        pltpu.VMEM((nb * ROW_TILE, LANES), F32),
                        pltpu.VMEM((D, 2 * D), BF16), pltpu.VMEM((D, D), BF16),
                        pltpu.SemaphoreType.DMA((2,))],
    )
    return pl.pallas_call(
        _moe_kernel,
        grid_spec=grid_spec,
        out_shape=jax.ShapeDtypeStruct((nblk * nb * ROW_TILE, LANES), F32),
        compiler_params=pltpu.CompilerParams(
            dimension_semantics=("arbitrary",), vmem_limit_bytes=VMEM_LIMIT_MOE),
        name="moe",
    )(block_expert, n_used, row_tok, row_tok, u2, wup, bup, wdn, bdn)


def _final_kernel(dest_ref, dest_next_ref, rows_hbm, x1_ref, gate_ref, g2_ref, lng_ref,
                  lnb_ref, o_ref, buf0, buf1, sems):
    tm = x1_ref.shape[0]

    def compute(buf):
        gates = gate_ref[...]
        lane = lax.broadcasted_iota(jnp.int32, gates.shape, 1)
        g = [_lane_col(gates, j, lane) for j in range(TOP_K)]
        cols = []
        for c in range(ROW_TILE):
            acc = g[0] * _token_tile_cols(buf, 0, tm, c)
            for j in range(1, TOP_K):
                acc = acc + g[j] * _token_tile_cols(buf, tm * j, tm, c)
            cols.append(acc)
        ffn = jnp.concatenate(cols, axis=1)
        z = ALPHA * x1_ref[...] + (1.0 + g2_ref[0]) * ffn
        o_ref[...] = _ln(z) * lng_ref[...] + lnb_ref[...]

    _gather_step(pl.program_id(0), pl.num_programs(0), dest_ref, dest_next_ref,
                 rows_hbm, (buf0, buf1), sems, compute, lambda r: r % 2)


def _final(dest, rows, x1, gates, g2, lng, lnb, tiles_per_seq):
    T = x1.shape[0]
    tm = TM_FINAL
    nt = T // tm
    tok = lambda i: (i, 0)
    c2 = lambda i: (0, 0)
    n = TOP_K * tm
    return pl.pallas_call(
        _final_kernel,
        grid=(nt,),
        in_specs=[pl.BlockSpec((1, 1, n), lambda i: (i, 0, 0), memory_space=pltpu.SMEM),
                  pl.BlockSpec((1, 1, n), lambda i: (jnp.minimum(i + 1, nt - 1), 0, 0),
                               memory_space=pltpu.SMEM),
                  pl.BlockSpec(memory_space=pl.ANY),
                  pl.BlockSpec((tm, D), tok),
                  pl.BlockSpec((tm, ROUTE_LANES), tok),
                  pl.BlockSpec((1, 1, D), lambda i: (i // tiles_per_seq, 0, 0)),
                  pl.BlockSpec((1, D), c2),
                  pl.BlockSpec((1, D), c2)],
        out_specs=pl.BlockSpec((tm, D), tok),
        out_shape=jax.ShapeDtypeStruct((T, D), F32),
        scratch_shapes=[pltpu.VMEM((n * ROW_TILE, LANES), F32),
                        pltpu.VMEM((n * ROW_TILE, LANES), F32),
                        pltpu.SemaphoreType.DMA((2,))],
        compiler_params=pltpu.CompilerParams(
            dimension_semantics=("arbitrary",), vmem_limit_bytes=VMEM_LIMIT),
        name="final",
    )(dest, dest, rows, x1, gates, g2, lng, lnb)


def _routing(top_idx, rank, counts):
    T = top_idx.shape[0]
    A = T * TOP_K
    nb = MOE_BLOCK
    nblk = A // nb + N_EXP
    e_flat = top_idx.reshape(A)
    rank = rank.reshape(A)
    padded = (counts + nb - 1) // nb * nb
    padded_end = jnp.cumsum(padded)
    padded_start = padded_end - padded
    dest = padded_start[e_flat] + rank
    blk_row0 = jnp.arange(nblk, dtype=jnp.int32) * nb
    block_expert = jnp.minimum(
        jnp.sum((padded_end[None, :] <= blk_row0[:, None]).astype(jnp.int32), axis=1),
        N_EXP - 1)
    order = jnp.argsort(e_flat, stable=True).astype(jnp.int32)
    starts = jnp.cumsum(counts) - counts
    local = (blk_row0 - padded_start[block_expert])[:, None] + jnp.arange(nb, dtype=jnp.int32)
    valid = local < counts[block_expert][:, None]
    src = jnp.clip(starts[block_expert][:, None] + local, 0, A - 1)
    row_tok = jnp.where(valid, order[src] // TOP_K, 0)
    n_used = (padded_end[-1] // nb).astype(jnp.int32).reshape(1)
    last_e = block_expert[jnp.maximum(n_used[0] - 1, 0)]
    block_expert = jnp.where(jnp.arange(nblk) < n_used[0], block_expert, last_e)
    return (block_expert, n_used, row_tok.astype(jnp.int32).reshape(nblk, 1, nb),
            dest.reshape(T, TOP_K))


def kernel(x, c, w_ada, b_ada, w_in, fox_f_bias, w_gla_gate, b_gla_gate, gla_norm_g,
           w_branch_a, w_branch_b, w_out, ln1_g, ln1_b, w_router, b_router,
           w_up, b_up, w_down, b_down, ln2_g, ln2_b):
    B, S, _ = x.shape
    T = B * S
    l = 0

    c_pad = jnp.zeros((8, D), F32).at[:B].set(c)
    mod = _ada(c_pad, w_ada[l], b_ada[l][None, :])[:B]
    mod6 = mod.reshape(B, 6, D)
    mod8 = jnp.concatenate([mod6, jnp.zeros((B, 2, D), F32)], axis=1)
    sh1, sc1 = mod6[:, 0:1], mod6[:, 1:2]
    g2 = mod6[:, 5:6]

    w = w_in[l]
    o = 0
    parts = []
    for width in (FOX_W, FOX_W, FOX_W, FOX_H, GLA_KW, GLA_KW, GLA_VW, GLA_VW, GLA_RANK, D, D):
        parts.append(w[:, o:o + width])
        o += width
    wq, wk, wv, wff, wgq, wgk, wgv, wgr, wglr, wga, wgb = parts

    wfox = jnp.concatenate([wq * (FOX_DH ** -0.5 * LOG2E), wk, wv], axis=1).astype(BF16)
    wgla = jnp.concatenate([wgq * GLA_DK ** -0.5, wgk, wgv, wgr], axis=1).astype(BF16)
    wsm = jnp.zeros((D, 2 * LANES), F32).at[:, :FOX_H].set(wff)
    wsm = wsm.at[:, LANES:LANES + GLA_RANK].set(wglr).astype(BF16)
    fb = jnp.zeros((1, LANES), F32).at[0, :FOX_H].set(fox_f_bias[l])
    tri = jnp.asarray(np.tril(np.ones((TM_IN, TM_IN), np.float32)), dtype=BF16)

    qa, ka, va, gq, gk, gv, gr, glr = _inproj(x, sh1, sc1, wfox, wgla, wsm, fb, tri)

    ya = _fox(qa, ka, va)

    cm3, masks = _gla_tables()
    yb = _gla(gq, gk, gv, gr, glr, w_gla_gate[l], b_gla_gate[l][None, :],
              gla_norm_g[l][None, :], jnp.asarray(cm3, dtype=BF16), jnp.asarray(masks))

    wgate = jnp.concatenate([wga, wgb], axis=1).astype(BF16)
    wr = jnp.zeros((D, LANES), F32).at[:, :N_EXP].set(w_router[l])
    wr_hi = wr.astype(BF16)
    wr = jnp.stack([wr_hi, (wr - wr_hi.astype(F32)).astype(BF16)])
    br = jnp.full((1, LANES), NEG, F32).at[0, :N_EXP].set(b_router[l])
    x1, u2, topv, topi, rank, counts = _merge(
        x, ya, yb, mod8, wgate, w_branch_a[l].astype(BF16), w_branch_b[l].astype(BF16),
        w_out[l].astype(BF16), ln1_g[l][None, :], ln1_b[l][None, :], wr, br)

    block_expert, n_used, row_tok, dest = _routing(
        topi.reshape(T, ROUTE_LANES)[:, :TOP_K], rank.reshape(T, ROUTE_LANES)[:, :TOP_K],
        counts[0, :N_EXP].astype(jnp.int32))
    rows = _moe(block_expert, n_used, row_tok, u2,
                w_up[l], b_up[l][:, None, :], w_down[l], b_down[l][:, None, :])

    nt = T // TM_FINAL
    dest_t = dest.reshape(nt, TM_FINAL, TOP_K).transpose(0, 2, 1).reshape(nt, 1, TOP_K * TM_FINAL)
    out = _final(dest_t, rows, x1.reshape(T, D), topv.reshape(T, ROUTE_LANES), g2,
                 ln2_g[l][None, :], ln2_b[l][None, :], S // TM_FINAL)
    return out.reshape(B, S, D)
```
